```python
import math
import jax, jax.numpy as jnp
from jax import lax
import numpy as np

D_MODEL = 1024
BATCH = 16
SEQ = 256
DEPTH = 1
DEC_BATCH = 8
DEC_SEQ = 1024
PAST_LEN = 512

GRID_W = 64
N_DIFF_HEADS = 6
DIFF_HEAD_DIM = 64
DIFF_V_DIM = 2 * DIFF_HEAD_DIM
N_FOURIER_HEADS = 4
FOURIER_HEAD_DIM = 64
D_FOURIER = N_FOURIER_HEADS * FOURIER_HEAD_DIM
D_ATTN = N_DIFF_HEADS * DIFF_V_DIM
D_MIX = D_FOURIER + D_ATTN
D_QK = N_DIFF_HEADS * 2 * DIFF_HEAD_DIM
D_IN = D_FOURIER + 2 * D_QK + D_ATTN
D_FF = 2816
CONV_W = 3
ROPE_THETA = 10000.0
EPS = 1e-6
Q_BLOCK = 128

kernel_name = 'hybrid_fnet_diffattn_dit_step'


def rmsnorm(x, g):
    xf = x.astype(jnp.float32)
    y = xf * lax.rsqrt(jnp.mean(xf * xf, axis=-1, keepdims=True) + EPS)
    return (y * g.astype(jnp.float32)).astype(x.dtype)


def ada_modulation(cond, w_ada, b_ada):
    m = jax.nn.silu(cond) @ w_ada + b_ada
    return jnp.split(m[:, None, :], 6, axis=-1)


def axial_rope_tables(n_tokens):
    rows = n_tokens // GRID_W
    row = jnp.repeat(jnp.arange(rows, dtype=jnp.float32), GRID_W)
    col = jnp.tile(jnp.arange(GRID_W, dtype=jnp.float32), rows)
    n_freq = DIFF_HEAD_DIM // 4
    inv = ROPE_THETA ** (-jnp.arange(n_freq, dtype=jnp.float32) / n_freq)
    ang_r = (row[:, None] * inv)[:, None, :]
    ang_c = (col[:, None] * inv)[:, None, :]
    return (jnp.cos(ang_r), jnp.sin(ang_r), jnp.cos(ang_c), jnp.sin(ang_c))


def rope_half(x, cos, sin):
    x1, x2 = jnp.split(x, 2, axis=-1)
    cos = cos.astype(x.dtype)
    sin = sin.astype(x.dtype)
    return jnp.concatenate([x1 * cos - x2 * sin, x2 * cos + x1 * sin], axis=-1)


def apply_axial_rope(x, cos_r, sin_r, cos_c, sin_c):
    xr, xc = jnp.split(x, 2, axis=-1)
    return jnp.concatenate([rope_half(xr, cos_r, sin_r), rope_half(xc, cos_c, sin_c)], axis=-1)


def fourier_mix(xf):
    B, S, _ = xf.shape
    z = xf.reshape(B, S, N_FOURIER_HEADS, FOURIER_HEAD_DIM).astype(jnp.float32)
    y = jnp.fft.fftn(z, axes=(1, 3), norm='ortho').real
    return y.reshape(B, S, D_FOURIER).astype(xf.dtype)


def diff_attention(q, k, v, lam):
    B, H, Sq = q.shape[:3]
    nb = Sq // Q_BLOCK
    qb = q.reshape(B, H, nb, Q_BLOCK, 2, DIFF_HEAD_DIM).transpose(2, 0, 1, 3, 4, 5)
    scale = DIFF_HEAD_DIM ** -0.5

    def block(qblk):
        s = jnp.einsum('bhqcd,bhkcd->bhcqk', qblk, k).astype(jnp.float32) * scale
        p = jax.nn.softmax(s, axis=-1)
        a = p[:, :, 0] - lam * p[:, :, 1]
        return jnp.einsum('bhqk,bhkd->bhqd', a.astype(v.dtype), v)

    o = lax.map(block, qb)
    return o.transpose(1, 2, 0, 3, 4).reshape(B, H, Sq, DIFF_V_DIM)


def conv_glu(h, w_gate, w_up, conv_w, conv_b, w_down):
    g = h @ w_gate
    gp = jnp.pad(g, ((0, 0), (1, 1), (0, 0)))
    g = gp[:, :-2] * conv_w[0] + gp[:, 1:-1] * conv_w[1] + gp[:, 2:] * conv_w[2] + conv_b
    return (jax.nn.silu(g) * (h @ w_up)) @ w_down


def trunk_layer(x, cond, lp, layer_idx, rope, k_ctx, v_ctx):
    (n1, n2, w_ada, b_ada, w_in, qg, kg, lq1, lk1, lq2, lk2, subg, w_out,
     w_gate, w_up, conv_w, conv_b, w_down) = lp
    B, S, _ = x.shape
    shift1, scale1, gate1, shift2, scale2, gate2 = ada_modulation(cond, w_ada, b_ada)

    h = rmsnorm(x, n1) * (1 + scale1) + shift1
    proj = h @ w_in
    xf, q, k, v = jnp.split(proj, [D_FOURIER, D_FOURIER + D_QK, D_FOURIER + 2 * D_QK], axis=-1)
    q = rmsnorm(q.reshape(B, S, N_DIFF_HEADS, 2, DIFF_HEAD_DIM), qg).transpose(0, 2, 1, 3, 4)
    k = rmsnorm(k.reshape(B, S, N_DIFF_HEADS, 2, DIFF_HEAD_DIM), kg).transpose(0, 2, 1, 3, 4)
    v = v.reshape(B, S, N_DIFF_HEADS, DIFF_V_DIM).transpose(0, 2, 1, 3)
    if rope is not None:
        q = apply_axial_rope(q, *rope)
        k = apply_axial_rope(k, *rope)
    if k_ctx is not None:
        k_all = jnp.concatenate([k, k_ctx.astype(k.dtype)], axis=2)
        v_all = jnp.concatenate([v, v_ctx.astype(v.dtype)], axis=2)
    else:
        k_all, v_all = k, v
    lam_init = 0.8 - 0.6 * math.exp(-0.3 * layer_idx)
    lam = (jnp.exp(jnp.sum(lq1.astype(jnp.float32) * lk1.astype(jnp.float32)))
           - jnp.exp(jnp.sum(lq2.astype(jnp.float32) * lk2.astype(jnp.float32))) + lam_init)
    o = diff_attention(q, k_all, v_all, lam)
    o = rmsnorm(o, subg) * (1.0 - lam_init)
    o = o.transpose(0, 2, 1, 3).reshape(B, S, D_ATTN)
    mix = jnp.concatenate([fourier_mix(xf), o], axis=-1) @ w_out
    x = x + gate1 * mix

    h2 = rmsnorm(x, n2) * (1 + scale2) + shift2
    x = x + gate2 * conv_glu(h2, w_gate, w_up, conv_w, conv_b, w_down)
    return x, k, v


def setup_inputs(seed: int = 0) -> dict:
    key = jax.random.key(seed)
    ks = jax.random.split(key, 26)
    f32 = jnp.float32

    def nrm(k, shape, s):
        return jax.random.normal(k, shape, f32) * s

    return {
        'x_prompt': nrm(ks[0], (BATCH, SEQ, D_MODEL), 1.0),
        'x_sample': nrm(ks[1], (DEC_BATCH, DEC_SEQ, D_MODEL), 1.0),
        'cache_k': nrm(ks[2], (DEC_BATCH, DEPTH, N_DIFF_HEADS, PAST_LEN, 2, DIFF_HEAD_DIM), 1.0),
        'cache_v': nrm(ks[3], (DEC_BATCH, DEPTH, N_DIFF_HEADS, PAST_LEN, DIFF_V_DIM), 1.0),
        'c': nrm(ks[4], (DEC_BATCH, D_MODEL), 1.0),
        'c_ctx': nrm(ks[5], (D_MODEL,), 1.0),
        'norm1_g': 1.0 + nrm(ks[6], (DEPTH, D_MODEL), 0.02),
        'norm2_g': 1.0 + nrm(ks[7], (DEPTH, D_MODEL), 0.02),
        'w_ada': nrm(ks[8], (DEPTH, D_MODEL, 6 * D_MODEL), 0.5 * D_MODEL ** -0.5),
        'b_ada': nrm(ks[9], (DEPTH, 6 * D_MODEL), 0.02),
        'w_in': nrm(ks[10], (DEPTH, D_MODEL, D_IN), D_MODEL ** -0.5),
        'q_norm_g': 1.0 + nrm(ks[11], (DEPTH, DIFF_HEAD_DIM), 0.02),
        'k_norm_g': 1.0 + nrm(ks[12], (DEPTH, DIFF_HEAD_DIM), 0.02),
        'lam_q1': nrm(ks[13], (DEPTH, DIFF_HEAD_DIM), 0.1),
        'lam_k1': nrm(ks[14], (DEPTH, DIFF_HEAD_DIM), 0.1),
        'lam_q2': nrm(ks[15], (DEPTH, DIFF_HEAD_DIM), 0.1),
        'lam_k2': nrm(ks[16], (DEPTH, DIFF_HEAD_DIM), 0.1),
        'subln_g': 1.0 + nrm(ks[17], (DEPTH, DIFF_V_DIM), 0.02),
        'w_out': nrm(ks[18], (DEPTH, D_MIX, D_MODEL), D_MIX ** -0.5),
        'w_gate': nrm(ks[19], (DEPTH, D_MODEL, D_FF), D_MODEL ** -0.5),
        'w_up': nrm(ks[20], (DEPTH, D_MODEL, D_FF), D_MODEL ** -0.5),
        'conv_w': nrm(ks[21], (DEPTH, CONV_W, D_FF), CONV_W ** -0.5),
        'conv_b': nrm(ks[22], (DEPTH, D_FF), 0.02),
        'w_down': nrm(ks[23], (DEPTH, D_FF, D_MODEL), D_FF ** -0.5),
    }


def reference(x_prompt, x_sample, cache_k, cache_v, c, c_ctx,
              norm1_g, norm2_g, w_ada, b_ada, w_in, q_norm_g, k_norm_g,
              lam_q1, lam_k1, lam_q2, lam_k2, subln_g, w_out,
              w_gate, w_up, conv_w, conv_b, w_down):
    rope = axial_rope_tables(x_sample.shape[1])
    y_p = x_prompt
    y_s = x_sample
    k_list, v_list = [], []
    for i in range(DEPTH):
        lp = (norm1_g[i], norm2_g[i], w_ada[i], b_ada[i], w_in[i], q_norm_g[i], k_norm_g[i],
              lam_q1[i], lam_k1[i], lam_q2[i], lam_k2[i], subln_g[i], w_out[i],
              w_gate[i], w_up[i], conv_w[i], conv_b[i], w_down[i])
        y_p, k_i, v_i = trunk_layer(y_p, c_ctx[None, :], lp, i, None, None, None)
        k_list.append(k_i)
        v_list.append(v_i)
        y_s, _, _ = trunk_layer(y_s, c, lp, i, rope, cache_k[:, i], cache_v[:, i])
    new_k = jnp.stack(k_list, axis=1)
    new_v = jnp.stack(v_list, axis=1)
    return (y_p, y_s, new_k, new_v)
```

```python
import functools
import math

import jax
import jax.numpy as jnp
from jax import lax
from jax.experimental import pallas as pl
from jax.experimental.pallas import tpu as pltpu

F32 = jnp.float32
BF16 = jnp.bfloat16

D_MODEL = 1024
GRID_W = 64
N_HEADS = 6
HEAD_DIM = 64
V_DIM = 2 * HEAD_DIM
D_FOURIER = 256
FOURIER_HEAD_DIM = 64
D_QK = N_HEADS * V_DIM
D_ATTN = N_HEADS * V_DIM
D_IN = D_FOURIER + 2 * D_QK + D_ATTN
D_FF = 2816
ROPE_THETA = 10000.0
EPS = 1e-6
LAM_INIT = 0.8 - 0.6 * math.exp(-0.3 * 0)
N_MOD = 6
MOD_ROWS = 16

FF_CHUNK = 256
FFN_ROWS = 1024
VMEM_LIMIT = 56 * 1024 * 1024


def _params(n_axes):
    return pltpu.CompilerParams(
        dimension_semantics=("arbitrary",) * n_axes,
        vmem_limit_bytes=VMEM_LIMIT,
    )


def _ada_kernel(cond_ref, w_ref, b_ref, o_ref):
    c = cond_ref[...]
    s = c * jax.nn.sigmoid(c)
    o_ref[...] = jnp.dot(s.astype(BF16), w_ref[...].astype(BF16),
                         preferred_element_type=F32) + b_ref[...]


def _ada(cond, w_ada, b_ada):
    n = w_ada.shape[1]
    tn = 1536
    return pl.pallas_call(
        _ada_kernel,
        out_shape=jax.ShapeDtypeStruct((MOD_ROWS, n), F32),
        grid=(n // tn,),
        in_specs=[
            pl.BlockSpec((MOD_ROWS, D_MODEL), lambda j: (0, 0)),
            pl.BlockSpec((D_MODEL, tn), lambda j: (0, j)),
            pl.BlockSpec((1, tn), lambda j: (0, j)),
        ],
        out_specs=pl.BlockSpec((MOD_ROWS, tn), lambda j: (0, j)),
        compiler_params=_params(1),
        name="ada_mod",
    )(cond, w_ada, b_ada.reshape(1, n))


def _rms_rows(x, gain):
    ms = jnp.mean(x * x, axis=-1, keepdims=True)
    return x * lax.rsqrt(ms + EPS) * gain


def _comp_rmsnorm(x, gain, lo_mask):
    sq = x * x
    tot = jnp.sum(sq, axis=-1, keepdims=True)
    lo = jnp.sum(jnp.where(lo_mask, sq, 0.0), axis=-1, keepdims=True)
    inv_lo = lax.rsqrt(lo * (1.0 / HEAD_DIM) + EPS)
    inv_hi = lax.rsqrt((tot - lo) * (1.0 / HEAD_DIM) + EPS)
    return x * jnp.where(lo_mask, inv_lo, inv_hi) * gain


def _rope(x, cos, sin_dn, sin_up):
    return (x * cos + pltpu.roll(x, V_DIM - 16, 1) * sin_dn
            + pltpu.roll(x, 16, 1) * sin_up)


def _inproj_kernel(x_ref, mod_ref, n1_ref, w_ref, qg_ref, kg_ref, *rest,
                   use_rope, kv_dtype):
    if use_rope:
        cos_ref, sdn_ref, sup_ref, xf_ref, q_ref, k_ref, v_ref = rest
    else:
        xf_ref, q_ref, k_ref, v_ref = rest
    x = x_ref[0]
    mod = mod_ref[0]
    h = _rms_rows(x, n1_ref[...]) * (1.0 + mod[1:2]) + mod[0:1]
    proj = jnp.dot(h.astype(BF16), w_ref[...], preferred_element_type=F32)
    xf_ref[0] = proj[:, :D_FOURIER].astype(xf_ref.dtype)

    lane = lax.broadcasted_iota(jnp.int32, (1, V_DIM), 1)
    lo_mask = lane < HEAD_DIM
    qg = qg_ref[...]
    kg = kg_ref[...]
    for hd in range(N_HEADS):
        c0 = D_FOURIER + hd * V_DIM
        q = _comp_rmsnorm(proj[:, c0:c0 + V_DIM], qg, lo_mask)
        k = _comp_rmsnorm(proj[:, c0 + D_QK:c0 + D_QK + V_DIM], kg, lo_mask)
        v = proj[:, c0 + 2 * D_QK:c0 + 2 * D_QK + V_DIM]
        if use_rope:
            k_out = _rope(k, cos_ref[...], sdn_ref[...], sup_ref[...])
            q = _rope(q, cos_ref[...], sdn_ref[...], sup_ref[...])
        else:
            k_out = k
        q_ref[0, hd] = (q * (HEAD_DIM ** -0.5)).astype(q_ref.dtype)
        k_ref[0, hd] = k_out.astype(kv_dtype)
        v_ref[0, hd] = v.astype(kv_dtype)


def _inproj(x, mod, mod_row, n1, w_in, qg2, kg2, rope, tm, kv_dtype):
    b, s, d = x.shape
    use_rope = rope is not None
    in_specs = [
        pl.BlockSpec((1, tm, d), lambda i, t: (i, t, 0)),
        pl.BlockSpec((1, N_MOD, d), lambda i, t: (mod_row(i), 0, 0)),
        pl.BlockSpec((1, d), lambda i, t: (0, 0)),
        pl.BlockSpec((d, D_IN), lambda i, t: (0, 0)),
        pl.BlockSpec((1, V_DIM), lambda i, t: (0, 0)),
        pl.BlockSpec((1, V_DIM), lambda i, t: (0, 0)),
    ]
    args = [x, mod, n1, w_in, qg2, kg2]
    if use_rope:
        in_specs += [pl.BlockSpec((tm, V_DIM), lambda i, t: (t, 0))] * 3
        args += list(rope)
    head_spec = pl.BlockSpec((1, N_HEADS, tm, V_DIM), lambda i, t: (i, 0, t, 0))
    return pl.pallas_call(
        functools.partial(_inproj_kernel, use_rope=use_rope, kv_dtype=kv_dtype),
        out_shape=(
            jax.ShapeDtypeStruct((b, s, D_FOURIER), BF16),
            jax.ShapeDtypeStruct((b, N_HEADS, s, V_DIM), BF16),
            jax.ShapeDtypeStruct((b, N_HEADS, s, V_DIM), kv_dtype),
            jax.ShapeDtypeStruct((b, N_HEADS, s, V_DIM), kv_dtype),
        ),
        grid=(b, s // tm),
        in_specs=in_specs,
        out_specs=(
            pl.BlockSpec((1, tm, D_FOURIER), lambda i, t: (i, t, 0)),
            head_spec, head_spec, head_spec,
        ),
        compiler_params=_params(2),
        name="inproj_rope" if use_rope else "inproj",
    )(*args)


def _fourier_kernel(x_ref, cc_ref, sc_ref, cs_ref, ss_ref, o_ref):
    x = x_ref[0]
    xc = jnp.dot(x, cc_ref[...], preferred_element_type=F32).astype(BF16)
    xs = jnp.dot(x, sc_ref[...], preferred_element_type=F32).astype(BF16)
    y = (jnp.dot(cs_ref[...], xc, preferred_element_type=F32)
         - jnp.dot(ss_ref[...], xs, preferred_element_type=F32))
    o_ref[0] = y.astype(o_ref.dtype)


def _dft_tables(n, scale):
    idx = jnp.arange(n, dtype=jnp.int32)
    ang = ((idx[:, None] * idx[None, :]) % n).astype(F32) * (2.0 * math.pi / n)
    return jnp.cos(ang) * scale, jnp.sin(ang) * scale


def _fourier(xf):
    b, s, c = xf.shape
    c64, s64 = _dft_tables(FOURIER_HEAD_DIM, (s * FOURIER_HEAD_DIM) ** -0.5)
    eye = jnp.eye(c // FOURIER_HEAD_DIM, dtype=F32)
    cc = jnp.kron(eye, c64).astype(BF16)
    sc = jnp.kron(eye, s64).astype(BF16)
    cs, ss = _dft_tables(s, 1.0)
    const = lambda shape: pl.BlockSpec(shape, lambda i: (0, 0))
    return pl.pallas_call(
        _fourier_kernel,
        out_shape=jax.ShapeDtypeStruct((b, s, c), BF16),
        grid=(b,),
        in_specs=[
            pl.BlockSpec((1, s, c), lambda i: (i, 0, 0)),
            const((c, c)), const((c, c)), const((s, s)), const((s, s)),
        ],
        out_specs=pl.BlockSpec((1, s, c), lambda i: (i, 0, 0)),
        compiler_params=_params(1),
        name="fourier_mix",
    )(xf, cc, sc, cs.astype(BF16), ss.astype(BF16))


def _scores(q, k):
    return lax.dot_general(q, k, (((1,), (1,)), ((), ())),
                           preferred_element_type=F32)


def _attn_kernel(lam_ref, subg_ref, q_ref, k_ref, v_ref, *rest, heads, has_ctx):
    if has_ctx:
        ck_ref, cv_ref, o_ref = rest
    else:
        (o_ref,) = rest
    lv = lam_ref[...]
    e1 = jnp.exp(jnp.sum(lv[0:1] * lv[1:2], axis=-1, keepdims=True))
    e2 = jnp.exp(jnp.sum(lv[2:3] * lv[3:4], axis=-1, keepdims=True))
    lam = e1 - e2 + LAM_INIT
    lane = lax.broadcasted_iota(jnp.int32, (1, V_DIM), 1)
    lo_mask = lane < HEAD_DIM
    subg = subg_ref[...]
    for hd in range(heads):
        q = q_ref[0, hd]
        zero = jnp.zeros_like(q)
        qs = (jnp.where(lo_mask, q, zero), jnp.where(lo_mask, zero, q))
        ks = [k_ref[0, hd].astype(BF16)]
        vs = [v_ref[0, hd].astype(BF16)]
        if has_ctx:
            ks.append(ck_ref[0, hd].astype(BF16))
            vs.append(cv_ref[0, hd].astype(BF16))
        probs = []
        for qc in qs:
            ss = [_scores(qc, kk) for kk in ks]
            m = ss[0].max(axis=-1, keepdims=True)
            for sx in ss[1:]:
                m = jnp.maximum(m, sx.max(axis=-1, keepdims=True))
            ps = [jnp.exp(sx - m) for sx in ss]
            l = ps[0].sum(axis=-1, keepdims=True)
            for px in ps[1:]:
                l = l + px.sum(axis=-1, keepdims=True)
            probs.append((ps, l))
        (p0, l0), (p1, l1) = probs
        r0 = 1.0 / l0
        r1 = lam / l1
        o = None
        for pa, pb, vv in zip(p0, p1, vs):
            a = (pa * r0 - pb * r1).astype(BF16)
            t = jnp.dot(a, vv, preferred_element_type=F32)
            o = t if o is None else o + t
        ms = jnp.mean(o * o, axis=-1, keepdims=True)
        o = o * lax.rsqrt(ms + EPS) * subg * (1.0 - LAM_INIT)
        o_ref[0, :, hd * V_DIM:(hd + 1) * V_DIM] = o.astype(o_ref.dtype)


def _attention(lamv, subg2, q, k, v, ctx, tq, heads):
    b, nh, s, _ = q.shape
    has_ctx = ctx is not None
    hg = nh // heads
    qspec = pl.BlockSpec((1, heads, tq, V_DIM), lambda i, g, t: (i, g, t, 0))
    kvspec = pl.BlockSpec((1, heads, s, V_DIM), lambda i, g, t: (i, g, 0, 0))
    in_specs = [
        pl.BlockSpec((4, HEAD_DIM), lambda i, g, t: (0, 0)),
        pl.BlockSpec((1, V_DIM), lambda i, g, t: (0, 0)),
        qspec, kvspec, kvspec,
    ]
    args = [lamv, subg2, q, k, v]
    if has_ctx:
        p = ctx[0].shape[2]
        cspec = pl.BlockSpec((1, heads, p, V_DIM), lambda i, g, t: (i, g, 0, 0))
        in_specs += [cspec, cspec]
        args += list(ctx)
    return pl.pallas_call(
        functools.partial(_attn_kernel, heads=heads, has_ctx=has_ctx),
        out_shape=jax.ShapeDtypeStruct((b, s, nh * V_DIM), BF16),
        grid=(b, hg, s // tq),
        in_specs=in_specs,
        out_specs=pl.BlockSpec((1, tq, heads * V_DIM), lambda i, g, t: (i, t, g)),
        compiler_params=_params(3),
        name="diff_attn_ctx" if has_ctx else "diff_attn",
    )(*args)


def _mixffn_kernel(x_ref, yf_ref, o_ref, mod_ref, n2_ref, wo_ref, wg_ref, wu_ref,
                   cw_ref, cb_ref, wd_ref, y_ref, x1_ref, h2_ref, acc_ref, *, seq):
    j = pl.program_id(1)
    mod = mod_ref[0]

    @pl.when(j == 0)
    def _():
        mix = (jnp.dot(yf_ref[...], wo_ref[:D_FOURIER, :], preferred_element_type=F32)
               + jnp.dot(o_ref[...], wo_ref[D_FOURIER:, :], preferred_element_type=F32))
        x1 = x_ref[...] + mod[2:3] * mix
        x1_ref[...] = x1
        h2 = _rms_rows(x1, n2_ref[...]) * (1.0 + mod[4:5]) + mod[3:4]
        h2_ref[...] = h2.astype(BF16)
        acc_ref[...] = jnp.zeros_like(acc_ref)

    h2 = h2_ref[...]
    g = jnp.dot(h2, wg_ref[...], preferred_element_type=F32)
    u = jnp.dot(h2, wu_ref[...], preferred_element_type=F32)
    rows = g.shape[0]
    pos = lax.broadcasted_iota(jnp.int32, (rows, 1), 0) % seq
    g_prev = jnp.where(pos == 0, 0.0, pltpu.roll(g, 1, 0))
    g_next = jnp.where(pos == seq - 1, 0.0, pltpu.roll(g, rows - 1, 0))
    cw = cw_ref[...]
    gc = g_prev * cw[0:1] + g * cw[1:2] + g_next * cw[2:3] + cb_ref[...]
    act = (gc * jax.nn.sigmoid(gc) * u).astype(BF16)
    acc_ref[...] += jnp.dot(act, wd_ref[...], preferred_element_type=F32)

    @pl.when(j == pl.num_programs(1) - 1)
    def _():
        y_ref[...] = x1_ref[...] + mod[5:6] * acc_ref[...]


def _mixffn(x2, yf2, o2, mod, mod_row, n2, w_out, w_gate, w_up, conv_w, conv_b2,
            w_down, seq):
    m, d = x2.shape
    tm = FFN_ROWS
    tf = FF_CHUNK
    row = lambda i, j: (i, 0)
    return pl.pallas_call(
        functools.partial(_mixffn_kernel, seq=seq),
        out_shape=jax.ShapeDtypeStruct((m, d), F32),
        grid=(m // tm, D_FF // tf),
        in_specs=[
            pl.BlockSpec((tm, d), row),
            pl.BlockSpec((tm, D_FOURIER), row),
            pl.BlockSpec((tm, D_ATTN), row),
            pl.BlockSpec((1, N_MOD, d), lambda i, j: (mod_row(i), 0, 0)),
            pl.BlockSpec((1, d), lambda i, j: (0, 0)),
            pl.BlockSpec((d, d), lambda i, j: (0, 0)),
            pl.BlockSpec((d, tf), lambda i, j: (0, j)),
            pl.BlockSpec((d, tf), lambda i, j: (0, j)),
            pl.BlockSpec((3, tf), lambda i, j: (0, j)),
            pl.BlockSpec((1, tf), lambda i, j: (0, j)),
            pl.BlockSpec((tf, d), lambda i, j: (j, 0)),
        ],
        out_specs=pl.BlockSpec((tm, d), row),
        scratch_shapes=[
            pltpu.VMEM((tm, d), F32),
            pltpu.VMEM((tm, d), BF16),
            pltpu.VMEM((tm, d), F32),
        ],
        compiler_params=_params(2),
        name="mix_ffn",
    )(x2, yf2, o2, mod, n2, w_out, w_gate, w_up, conv_w, conv_b2, w_down)


def _rope_tables(n_tokens):
    rows = n_tokens // GRID_W
    row = jnp.repeat(jnp.arange(rows, dtype=F32), GRID_W)
    col = jnp.tile(jnp.arange(GRID_W, dtype=F32), rows)
    n_freq = HEAD_DIM // 4
    inv = ROPE_THETA ** (-jnp.arange(n_freq, dtype=F32) / n_freq)
    ang_r = row[:, None] * inv
    ang_c = col[:, None] * inv
    zero = jnp.zeros_like(ang_r)
    cos64 = jnp.concatenate([jnp.cos(ang_r)] * 2 + [jnp.cos(ang_c)] * 2, axis=-1)
    sdn64 = jnp.concatenate([-jnp.sin(ang_r), zero, -jnp.sin(ang_c), zero], axis=-1)
    sup64 = jnp.concatenate([zero, jnp.sin(ang_r), zero, jnp.sin(ang_c)], axis=-1)
    two = lambda t: jnp.concatenate([t, t], axis=-1)
    return two(cos64), two(sdn64), two(sup64)


def _layer(x, mod, mod_row_b, mod_row_tile, wts, lamv, rope, ctx, tm, tq, heads,
           kv_dtype):
    (n1, n2, w_in, qg2, kg2, subg2, w_out, w_gate, w_up, conv_w, conv_b2, w_down) = wts
    b, s, d = x.shape
    xf, q, k, v = _inproj(x, mod, mod_row_b, n1, w_in, qg2, kg2, rope, tm, kv_dtype)
    yf = _fourier(xf)
    o = _attention(lamv, subg2, q, k, v, ctx, tq, heads)
    y = _mixffn(x.reshape(b * s, d), yf.reshape(b * s, D_FOURIER),
                o.reshape(b * s, D_ATTN), mod, mod_row_tile, n2, w_out, w_gate,
                w_up, conv_w, conv_b2, w_down, s)
    return y.reshape(b, s, d), k, v


def kernel(x_prompt, x_sample, cache_k, cache_v, c, c_ctx, norm1_g, norm2_g, w_ada,
           b_ada, w_in, q_norm_g, k_norm_g, lam_q1, lam_k1, lam_q2, lam_k2, subln_g,
           w_out, w_gate, w_up, conv_w, conv_b, w_down):
    bp, sp, d = x_prompt.shape
    bs, ss, _ = x_sample.shape
    assert FFN_ROWS % sp == 0 and FFN_ROWS == ss

    cond = jnp.zeros((MOD_ROWS, d), F32).at[0].set(c_ctx).at[1:1 + bs].set(c)
    mod = _ada(cond, w_ada[0], b_ada[0]).reshape(MOD_ROWS, N_MOD, d)

    two = lambda t: jnp.concatenate([t, t], axis=-1)[None, :]
    wts = (norm1_g[0][None, :], norm2_g[0][None, :], w_in[0].astype(BF16),
           two(q_norm_g[0]), two(k_norm_g[0]), subln_g[0][None, :],
           w_out[0].astype(BF16), w_gate[0].astype(BF16), w_up[0].astype(BF16),
           conv_w[0], conv_b[0][None, :], w_down[0].astype(BF16))
    lamv = jnp.stack([lam_q1[0], lam_k1[0], lam_q2[0], lam_k2[0]])

    y_p, k_p, v_p = _layer(
        x_prompt, mod, lambda i: 0, lambda i: 0, wts, lamv, None, None,
        tm=sp, tq=sp, heads=N_HEADS, kv_dtype=F32)

    p = cache_k.shape[3]
    ctx = (cache_k[:, 0].reshape(bs, N_HEADS, p, V_DIM), cache_v[:, 0])
    y_s, _, _ = _layer(
        x_sample, mod, lambda i: i + 1, lambda i: i + 1, wts, lamv,
        _rope_tables(ss), ctx, tm=512, tq=256, heads=1, kv_dtype=BF16)

    new_k = k_p.reshape(bp, 1, N_HEADS, sp, 2, HEAD_DIM)
    new_v = v_p.reshape(bp, 1, N_HEADS, sp, V_DIM)
    return (y_p, y_s, new_k, new_v)
```

```python
import functools
import math

import jax
import jax.numpy as jnp
import numpy as np
from jax import lax
from jax.experimental import pallas as pl
from jax.experimental.pallas import tpu as pltpu

F32 = jnp.float32
BF16 = jnp.bfloat16

D_MODEL = 1024
GRID_W = 64
N_HEADS = 6
HEAD_DIM = 64
V_DIM = 2 * HEAD_DIM
D_FOURIER = 256
FOURIER_HEAD_DIM = 64
D_QK = N_HEADS * V_DIM
D_ATTN = N_HEADS * V_DIM
D_IN = D_FOURIER + 2 * D_QK + D_ATTN
D_FF = 2816
ROPE_THETA = 10000.0
EPS = 1e-6
LAM_INIT = 0.8 - 0.6 * math.exp(-0.3 * 0)
Q_SCALE = HEAD_DIM ** -0.5 * math.log2(math.e)
N_MOD = 6
MOD_ROWS = 16

SUBLANES = 8
MXU_COLS = 256
FF_BLOCK = D_FF // 2
FF_SUBCHUNKS = tuple((c, min(MXU_COLS, FF_BLOCK - c))
                     for c in range(0, FF_BLOCK, MXU_COLS))
FFN_ROWS = 1024
VMEM_LIMIT = 58 * 1024 * 1024


def _params(n_axes):
    return pltpu.CompilerParams(
        dimension_semantics=("arbitrary",) * n_axes,
        vmem_limit_bytes=VMEM_LIMIT,
    )


def _ada_kernel(cond_ref, w_ref, b_ref, o_ref):
    c = cond_ref[...]
    s = c * jax.nn.sigmoid(c)
    o_ref[...] = jnp.dot(s.astype(BF16), w_ref[...].astype(BF16),
                         preferred_element_type=F32) + b_ref[...]


def _ada(cond, w_ada, b_ada):
    n = w_ada.shape[1]
    tn = 1536
    return pl.pallas_call(
        _ada_kernel,
        out_shape=jax.ShapeDtypeStruct((MOD_ROWS, n), F32),
        grid=(n // tn,),
        in_specs=[
            pl.BlockSpec((MOD_ROWS, D_MODEL), lambda j: (0, 0)),
            pl.BlockSpec((D_MODEL, tn), lambda j: (0, j)),
            pl.BlockSpec((1, tn), lambda j: (0, j)),
        ],
        out_specs=pl.BlockSpec((MOD_ROWS, tn), lambda j: (0, j)),
        compiler_params=_params(1),
        name="ada_mod",
    )(cond, w_ada, b_ada.reshape(1, n))


def _rms_rows(x, gain):
    ms = jnp.mean(x * x, axis=-1, keepdims=True)
    return x * lax.rsqrt(ms + EPS) * gain


def _comp_rmsnorm(x, gain, lo_mask):
    sq = x * x
    tot = jnp.sum(sq, axis=-1, keepdims=True)
    lo = jnp.sum(jnp.where(lo_mask, sq, 0.0), axis=-1, keepdims=True)
    inv_lo = lax.rsqrt(lo * (1.0 / HEAD_DIM) + EPS)
    inv_hi = lax.rsqrt((tot - lo) * (1.0 / HEAD_DIM) + EPS)
    return x * jnp.where(lo_mask, inv_lo, inv_hi) * gain


def _rope(x, cos, sin_dn, sin_up):
    return (x * cos + pltpu.roll(x, V_DIM - 16, 1) * sin_dn
            + pltpu.roll(x, 16, 1) * sin_up)


def _inproj_kernel(x_ref, mod_ref, n1_ref, w_ref, qg_ref, kg_ref, *rest,
                   use_rope, kv_dtype):
    if use_rope:
        cos_ref, sdn_ref, sup_ref, xf_ref, q_ref, k_ref, v_ref = rest
    else:
        xf_ref, q_ref, k_ref, v_ref = rest
    x = x_ref[0]
    mod = mod_ref[0]
    h = _rms_rows(x, n1_ref[...]) * (1.0 + mod[1:2]) + mod[0:1]
    proj = jnp.dot(h.astype(BF16), w_ref[...], preferred_element_type=F32)
    xf_ref[0] = proj[:, :D_FOURIER].astype(xf_ref.dtype)

    lane = lax.broadcasted_iota(jnp.int32, (1, V_DIM), 1)
    lo_mask = lane < HEAD_DIM
    qg = qg_ref[...]
    kg = kg_ref[...]
    for hd in range(N_HEADS):
        c0 = D_FOURIER + hd * V_DIM
        q = _comp_rmsnorm(proj[:, c0:c0 + V_DIM], qg, lo_mask)
        k = _comp_rmsnorm(proj[:, c0 + D_QK:c0 + D_QK + V_DIM], kg, lo_mask)
        v = proj[:, c0 + 2 * D_QK:c0 + 2 * D_QK + V_DIM]
        if use_rope:
            k_out = _rope(k, cos_ref[...], sdn_ref[...], sup_ref[...])
            q = _rope(q, cos_ref[...], sdn_ref[...], sup_ref[...])
        else:
            k_out = k
        q_ref[0, hd] = (q * Q_SCALE).astype(q_ref.dtype)
        k_ref[0, hd] = k_out.astype(kv_dtype)
        v_ref[0, hd] = v.astype(kv_dtype)


def _inproj(x, mod, mod_row, n1, w_in, qg2, kg2, rope, tm, kv_dtype):
    b, s, d = x.shape
    use_rope = rope is not None
    in_specs = [
        pl.BlockSpec((1, tm, d), lambda i, t: (i, t, 0)),
        pl.BlockSpec((1, N_MOD, d), lambda i, t: (mod_row(i), 0, 0)),
        pl.BlockSpec((1, d), lambda i, t: (0, 0)),
        pl.BlockSpec((d, D_IN), lambda i, t: (0, 0)),
        pl.BlockSpec((1, V_DIM), lambda i, t: (0, 0)),
        pl.BlockSpec((1, V_DIM), lambda i, t: (0, 0)),
    ]
    args = [x, mod, n1, w_in, qg2, kg2]
    if use_rope:
        in_specs += [pl.BlockSpec((tm, V_DIM), lambda i, t: (t, 0))] * 3
        args += list(rope)
    head_spec = pl.BlockSpec((1, N_HEADS, tm, V_DIM), lambda i, t: (i, 0, t, 0))
    return pl.pallas_call(
        functools.partial(_inproj_kernel, use_rope=use_rope, kv_dtype=kv_dtype),
        out_shape=(
            jax.ShapeDtypeStruct((b, s, D_FOURIER), BF16),
            jax.ShapeDtypeStruct((b, N_HEADS, s, V_DIM), BF16),
            jax.ShapeDtypeStruct((b, N_HEADS, s, V_DIM), kv_dtype),
            jax.ShapeDtypeStruct((b, N_HEADS, s, V_DIM), kv_dtype),
        ),
        grid=(b, s // tm),
        in_specs=in_specs,
        out_specs=(
            pl.BlockSpec((1, tm, D_FOURIER), lambda i, t: (i, t, 0)),
            head_spec, head_spec, head_spec,
        ),
        compiler_params=_params(2),
        name="inproj_rope" if use_rope else "inproj",
    )(*args)


def _fourier_kernel(x_ref, cc_ref, sc_ref, cs_ref, ss_ref, o_ref):
    x = x_ref[0]
    xc = jnp.dot(x, cc_ref[...], preferred_element_type=F32).astype(BF16)
    xs = jnp.dot(x, sc_ref[...], preferred_element_type=F32).astype(BF16)
    y = (jnp.dot(cs_ref[...], xc, preferred_element_type=F32)
         - jnp.dot(ss_ref[...], xs, preferred_element_type=F32))
    o_ref[0] = y.astype(o_ref.dtype)


def _dft_tables(n, scale):
    idx = np.arange(n, dtype=np.int64)
    ang = ((idx[:, None] * idx[None, :]) % n).astype(np.float64) * (2.0 * math.pi / n)
    return ((np.cos(ang) * scale).astype(np.float32),
            (np.sin(ang) * scale).astype(np.float32))


def _fourier(xf):
    b, s, c = xf.shape
    c64, s64 = _dft_tables(FOURIER_HEAD_DIM, (s * FOURIER_HEAD_DIM) ** -0.5)
    eye = np.eye(c // FOURIER_HEAD_DIM, dtype=np.float32)
    cc = jnp.asarray(np.kron(eye, c64)).astype(BF16)
    sc = jnp.asarray(np.kron(eye, s64)).astype(BF16)
    cs, ss = (jnp.asarray(t) for t in _dft_tables(s, 1.0))
    const = lambda shape: pl.BlockSpec(shape, lambda i: (0, 0))
    return pl.pallas_call(
        _fourier_kernel,
        out_shape=jax.ShapeDtypeStruct((b, s, c), BF16),
        grid=(b,),
        in_specs=[
            pl.BlockSpec((1, s, c), lambda i: (i, 0, 0)),
            const((c, c)), const((c, c)), const((s, s)), const((s, s)),
        ],
        out_specs=pl.BlockSpec((1, s, c), lambda i: (i, 0, 0)),
        compiler_params=_params(1),
        name="fourier_mix",
    )(xf, cc, sc, cs.astype(BF16), ss.astype(BF16))


def _scores(q, k):
    return lax.dot_general(q, k, (((1,), (1,)), ((), ())),
                           preferred_element_type=F32)


def _attn_kernel(lam_ref, subg_ref, q_ref, k_ref, v_ref, *rest, heads, has_ctx):
    if has_ctx:
        ck_ref, cv_ref, o_ref, kk_ref, vv_ref = rest
    else:
        o_ref, kk_ref, vv_ref = rest
    s_self = k_ref.shape[2]

    @pl.when(pl.program_id(2) == 0)
    def _():
        for hd in range(heads):
            kk_ref[hd, :s_self, :] = k_ref[0, hd].astype(BF16)
            vv_ref[hd, :s_self, :V_DIM] = v_ref[0, hd].astype(BF16)
            if has_ctx:
                kk_ref[hd, s_self:, :] = ck_ref[0, hd].astype(BF16)
                vv_ref[hd, s_self:, :V_DIM] = cv_ref[0, hd].astype(BF16)
            vv_ref[hd, :, V_DIM:] = jnp.ones((vv_ref.shape[1], V_DIM), BF16)

    lv = lam_ref[...]
    e1 = jnp.exp(jnp.sum(lv[0:1] * lv[1:2], axis=-1, keepdims=True))
    e2 = jnp.exp(jnp.sum(lv[2:3] * lv[3:4], axis=-1, keepdims=True))
    lam = e1 - e2 + LAM_INIT
    lane = lax.broadcasted_iota(jnp.int32, (1, V_DIM), 1)
    lo_mask = lane < HEAD_DIM
    subg = subg_ref[...]
    for hd in range(heads):
        q = q_ref[0, hd]
        zero = jnp.zeros_like(q)
        kk = kk_ref[hd]
        vv = vv_ref[hd]
        res = []
        for qc in (jnp.where(lo_mask, q, zero), jnp.where(lo_mask, zero, q)):
            s = _scores(qc, kk)
            p = jnp.exp2(s - s.max(axis=-1, keepdims=True)).astype(BF16)
            res.append(jnp.dot(p, vv, preferred_element_type=F32))
        r0, r1 = res
        o = r0[:, :V_DIM] / r0[:, V_DIM:] - lam * (r1[:, :V_DIM] / r1[:, V_DIM:])
        ms = jnp.mean(o * o, axis=-1, keepdims=True)
        o = o * lax.rsqrt(ms + EPS) * subg * (1.0 - LAM_INIT)
        o_ref[0, :, hd * V_DIM:(hd + 1) * V_DIM] = o.astype(o_ref.dtype)


def _attention(lamv, subg2, q, k, v, ctx, tq, heads):
    b, nh, s, _ = q.shape
    has_ctx = ctx is not None
    hg = nh // heads
    sk = s + (ctx[0].shape[2] if has_ctx else 0)
    qspec = pl.BlockSpec((1, heads, tq, V_DIM), lambda i, g, t: (i, g, t, 0))
    kvspec = pl.BlockSpec((1, heads, s, V_DIM), lambda i, g, t: (i, g, 0, 0))
    in_specs = [
        pl.BlockSpec((4, HEAD_DIM), lambda i, g, t: (0, 0)),
        pl.BlockSpec((1, V_DIM), lambda i, g, t: (0, 0)),
        qspec, kvspec, kvspec,
    ]
    args = [lamv, subg2, q, k, v]
    if has_ctx:
        p = ctx[0].shape[2]
        cspec = pl.BlockSpec((1, heads, p, V_DIM), lambda i, g, t: (i, g, 0, 0))
        in_specs += [cspec, cspec]
        args += list(ctx)
    return pl.pallas_call(
        functools.partial(_attn_kernel, heads=heads, has_ctx=has_ctx),
        out_shape=jax.ShapeDtypeStruct((b, s, nh * V_DIM), BF16),
        grid=(b, hg, s // tq),
        in_specs=in_specs,
        out_specs=pl.BlockSpec((1, tq, heads * V_DIM), lambda i, g, t: (i, t, g)),
        scratch_shapes=[
            pltpu.VMEM((heads, sk, V_DIM), BF16),
            pltpu.VMEM((heads, sk, 2 * V_DIM), BF16),
        ],
        compiler_params=_params(3),
        name="diff_attn_ctx" if has_ctx else "diff_attn",
    )(*args)


def _shifted_rows(gs_ref, slot, g, seq):
    rows, w = g.shape
    pitch = seq + SUBLANES
    for s in range(rows // seq):
        base = SUBLANES + s * pitch
        gs_ref[slot, base:base + seq, :w] = g[s * seq:(s + 1) * seq]
    shifted = []
    for off in (-1, 1):
        parts = [gs_ref[slot, SUBLANES + s * pitch + off:
                        SUBLANES + s * pitch + off + seq, :w]
                 for s in range(rows // seq)]
        shifted.append(parts[0] if len(parts) == 1 else jnp.concatenate(parts, axis=0))
    return shifted


def _mixffn_kernel(x_ref, yf_ref, o_ref, mod_ref, n2_ref, wo_ref, wg_ref, wu_ref,
                   cw_ref, cb_ref, wd_ref, y_ref, h2_ref, act_ref, gs_ref, *, seq):
    j = pl.program_id(1)
    mod = mod_ref[0]
    rows = x_ref.shape[0]

    @pl.when(j == 0)
    def _():
        mix = (jnp.dot(yf_ref[...], wo_ref[:D_FOURIER, :], preferred_element_type=F32)
               + jnp.dot(o_ref[...], wo_ref[D_FOURIER:, :], preferred_element_type=F32))
        x1 = x_ref[...] + mod[2:3] * mix
        y_ref[...] = x1
        h2 = _rms_rows(x1, n2_ref[...]) * (1.0 + mod[4:5]) + mod[3:4]
        h2_ref[...] = h2.astype(BF16)
        for slot in range(gs_ref.shape[0]):
            for s in range(rows // seq + 1):
                r0 = s * (seq + SUBLANES)
                gs_ref[slot, r0:r0 + SUBLANES, :] = jnp.zeros(
                    (SUBLANES, gs_ref.shape[2]), F32)

    h2 = h2_ref[...]
    cw = cw_ref[...]
    cb = cb_ref[...]
    for n, (c0, w) in enumerate(FF_SUBCHUNKS):
        if w == MXU_COLS:
            g = jnp.dot(h2, wg_ref[:, c0:c0 + w], preferred_element_type=F32)
            u = jnp.dot(h2, wu_ref[:, c0:c0 + w], preferred_element_type=F32)
        else:
            wgu = jnp.concatenate([wg_ref[:, c0:c0 + w], wu_ref[:, c0:c0 + w]], axis=1)
            gu = jnp.dot(h2, wgu, preferred_element_type=F32)
            g, u = gu[:, :w], gu[:, w:]
        g_prev, g_next = _shifted_rows(gs_ref, n % gs_ref.shape[0], g, seq)
        gc = (g * cw[1:2, c0:c0 + w] + g_prev * cw[0:1, c0:c0 + w]
              + g_next * cw[2:3, c0:c0 + w] + cb[:, c0:c0 + w])
        hg = 0.5 * gc
        act_ref[:, c0:c0 + w] = ((hg * u) * (1.0 + jnp.tanh(hg))).astype(BF16)
    y_ref[...] += mod[5:6] * jnp.dot(act_ref[...], wd_ref[...],
                                     preferred_element_type=F32)


def _mixffn(x2, yf2, o2, mod, mod_row, n2, w_out, w_gate, w_up, conv_w, conv_b2,
            w_down, seq):
    m, d = x2.shape
    tm = FFN_ROWS
    tf = FF_BLOCK
    row = lambda i, j: (i, 0)
    stage_rows = SUBLANES + (tm // seq) * (seq + SUBLANES)
    return pl.pallas_call(
        functools.partial(_mixffn_kernel, seq=seq),
        out_shape=jax.ShapeDtypeStruct((m, d), F32),
        grid=(m // tm, D_FF // tf),
        in_specs=[
            pl.BlockSpec((tm, d), row),
            pl.BlockSpec((tm, D_FOURIER), row),
            pl.BlockSpec((tm, D_ATTN), row),
            pl.BlockSpec((1, N_MOD, d), lambda i, j: (mod_row(i), 0, 0)),
            pl.BlockSpec((1, d), lambda i, j: (0, 0)),
            pl.BlockSpec((d, d), lambda i, j: (0, 0)),
            pl.BlockSpec((d, tf), lambda i, j: (0, j)),
            pl.BlockSpec((d, tf), lambda i, j: (0, j)),
            pl.BlockSpec((3, tf), lambda i, j: (0, j)),
            pl.BlockSpec((1, tf), lambda i, j: (0, j)),
            pl.BlockSpec((tf, d), lambda i, j: (j, 0)),
        ],
        out_specs=pl.BlockSpec((tm, d), row),
        scratch_shapes=[
            pltpu.VMEM((tm, d), BF16),
            pltpu.VMEM((tm, tf), BF16),
            pltpu.VMEM((2, stage_rows, MXU_COLS), F32),
        ],
        compiler_params=_params(2),
        name="mix_ffn",
    )(x2, yf2, o2, mod, n2, w_out, w_gate, w_up, conv_w, conv_b2, w_down)


def _rope_tables(n_tokens):
    rows = n_tokens // GRID_W
    row = np.repeat(np.arange(rows, dtype=np.float64), GRID_W)
    col = np.tile(np.arange(GRID_W, dtype=np.float64), rows)
    n_freq = HEAD_DIM // 4
    inv = ROPE_THETA ** (-np.arange(n_freq, dtype=np.float64) / n_freq)
    ang_r = row[:, None] * inv
    ang_c = col[:, None] * inv
    zero = np.zeros_like(ang_r)
    cos64 = np.concatenate([np.cos(ang_r)] * 2 + [np.cos(ang_c)] * 2, axis=-1)
    sdn64 = np.concatenate([-np.sin(ang_r), zero, -np.sin(ang_c), zero], axis=-1)
    sup64 = np.concatenate([zero, np.sin(ang_r), zero, np.sin(ang_c)], axis=-1)
    two = lambda t: jnp.asarray(np.concatenate([t, t], axis=-1).astype(np.float32))
    return two(cos64), two(sdn64), two(sup64)


def _layer(x, mod, mod_row_b, mod_row_tile, wts, lamv, rope, ctx, tm, tq, heads,
           kv_dtype):
    (n1, n2, w_in, qg2, kg2, subg2, w_out, w_gate, w_up, conv_w, conv_b2, w_down) = wts
    b, s, d = x.shape
    xf, q, k, v = _inproj(x, mod, mod_row_b, n1, w_in, qg2, kg2, rope, tm, kv_dtype)
    yf = _fourier(xf)
    o = _attention(lamv, subg2, q, k, v, ctx, tq, heads)
    y = _mixffn(x.reshape(b * s, d), yf.reshape(b * s, D_FOURIER),
                o.reshape(b * s, D_ATTN), mod, mod_row_tile, n2, w_out, w_gate,
                w_up, conv_w, conv_b2, w_down, s)
    return y.reshape(b, s, d), k, v


def kernel(x_prompt, x_sample, cache_k, cache_v, c, c_ctx, norm1_g, norm2_g, w_ada,
           b_ada, w_in, q_norm_g, k_norm_g, lam_q1, lam_k1, lam_q2, lam_k2, subln_g,
           w_out, w_gate, w_up, conv_w, conv_b, w_down):
    bp, sp, d = x_prompt.shape
    bs, ss, _ = x_sample.shape
    assert FFN_ROWS % sp == 0 and FFN_ROWS == ss

    cond = jnp.zeros((MOD_ROWS, d), F32).at[0].set(c_ctx).at[1:1 + bs].set(c)
    mod = _ada(cond, w_ada[0], b_ada[0]).reshape(MOD_ROWS, N_MOD, d)

    two = lambda t: jnp.concatenate([t, t], axis=-1)[None, :]
    wts = (norm1_g[0][None, :], norm2_g[0][None, :], w_in[0].astype(BF16),
           two(q_norm_g[0]), two(k_norm_g[0]), subln_g[0][None, :],
           w_out[0].astype(BF16), w_gate[0].astype(BF16), w_up[0].astype(BF16),
           conv_w[0], conv_b[0][None, :], w_down[0].astype(BF16))
    lamv = jnp.stack([lam_q1[0], lam_k1[0], lam_q2[0], lam_k2[0]])

    y_p, k_p, v_p = _layer(
        x_prompt, mod, lambda i: 0, lambda i: 0, wts, lamv, None, None,
        tm=sp, tq=sp, heads=N_HEADS, kv_dtype=F32)

    p = cache_k.shape[3]
    ctx = (cache_k[:, 0].reshape(bs, N_HEADS, p, V_DIM), cache_v[:, 0])
    y_s, _, _ = _layer(
        x_sample, mod, lambda i: i + 1, lambda i: i + 1, wts, lamv,
        _rope_tables(ss), ctx, tm=512, tq=512, heads=6, kv_dtype=BF16)

    new_k = k_p.reshape(bp, 1, N_HEADS, sp, 2, HEAD_DIM)
    new_v = v_p.reshape(bp, 1, N_HEADS, sp, V_DIM)
    return (y_p, y_s, new_k, new_v)
```

```python
import functools
import math

import jax
import jax.numpy as jnp
import numpy as np
from jax import lax
from jax.experimental import pallas as pl
from jax.experimental.pallas import tpu as pltpu

F32 = jnp.float32
BF16 = jnp.bfloat16

D_MODEL = 1024
GRID_W = 64
N_HEADS = 6
HEAD_DIM = 64
V_DIM = 2 * HEAD_DIM
D_FOURIER = 256
FOURIER_HEAD_DIM = 64
D_QK = N_HEADS * V_DIM
D_ATTN = N_HEADS * V_DIM
D_IN = D_FOURIER + 2 * D_QK + D_ATTN
D_FF = 2816
ROPE_THETA = 10000.0
EPS = 1e-6
LAM_INIT = 0.8 - 0.6 * math.exp(-0.3 * 0)
Q_SCALE = HEAD_DIM ** -0.5 * math.log2(math.e)
N_MOD = 6
MOD_ROWS = 16

SUBLANES = 8
MXU_COLS = 256
FF_BLOCK = D_FF // 2
FF_SUBCHUNKS = tuple((c, min(MXU_COLS, FF_BLOCK - c))
                     for c in range(0, FF_BLOCK, MXU_COLS))
FFN_ROWS = 1024
VMEM_LIMIT = 58 * 1024 * 1024


def _params(n_axes):
    return pltpu.CompilerParams(
        dimension_semantics=("arbitrary",) * n_axes,
        vmem_limit_bytes=VMEM_LIMIT,
    )


def _ada_kernel(cond_ref, w_ref, b_ref, o_ref):
    c = cond_ref[...]
    s = c * jax.nn.sigmoid(c)
    o_ref[...] = jnp.dot(s.astype(BF16), w_ref[...].astype(BF16),
                         preferred_element_type=F32) + b_ref[...]


def _ada(cond, w_ada, b_ada):
    n = w_ada.shape[1]
    tn = 1536
    return pl.pallas_call(
        _ada_kernel,
        out_shape=jax.ShapeDtypeStruct((MOD_ROWS, n), F32),
        grid=(n // tn,),
        in_specs=[
            pl.BlockSpec((MOD_ROWS, D_MODEL), lambda j: (0, 0)),
            pl.BlockSpec((D_MODEL, tn), lambda j: (0, j)),
            pl.BlockSpec((1, tn), lambda j: (0, j)),
        ],
        out_specs=pl.BlockSpec((MOD_ROWS, tn), lambda j: (0, j)),
        compiler_params=_params(1),
        name="ada_mod",
    )(cond, w_ada, b_ada.reshape(1, n))


def _rms_rows(x, gain):
    ms = jnp.mean(x * x, axis=-1, keepdims=True)
    return x * lax.rsqrt(ms + EPS) * gain


def _group_rmsnorm(x, gain, ones_blk):
    sq = x * x
    hi = sq.astype(BF16)
    lo = (sq - hi.astype(F32)).astype(BF16)
    ss = (jnp.dot(hi, ones_blk, preferred_element_type=F32)
          + jnp.dot(lo, ones_blk, preferred_element_type=F32))
    return x * lax.rsqrt(ss * (1.0 / HEAD_DIM) + EPS) * gain


def _rope(x, cos, sin_dn, sin_up):
    return (x * cos + pltpu.roll(x, V_DIM - 16, 1) * sin_dn
            + pltpu.roll(x, 16, 1) * sin_up)


def _inproj_kernel(x_ref, mod_ref, n1_ref, w_ref, qg_ref, kg_ref, ones_ref, cc_ref,
                   sc_ref, cs_ref, ss_ref, *rest, use_rope, kv_dtype):
    if use_rope:
        cos_ref, sdn_ref, sup_ref, yf_ref, q_ref, k_ref, v_ref = rest
    else:
        yf_ref, q_ref, k_ref, v_ref = rest
    nb, seq, d = x_ref.shape
    x = x_ref[...].reshape(nb * seq, d)
    mod = mod_ref[0]
    h = (_rms_rows(x, n1_ref[...]) * (1.0 + mod[1:2]) + mod[0:1]).astype(BF16)

    xf = jnp.dot(h, w_ref[:, :D_FOURIER], preferred_element_type=F32).astype(BF16)
    xc = jnp.dot(xf, cc_ref[...], preferred_element_type=F32).astype(BF16)
    xs = jnp.dot(xf, sc_ref[...], preferred_element_type=F32).astype(BF16)
    for b in range(nb):
        r = slice(b * seq, (b + 1) * seq)
        yf = (jnp.dot(cs_ref[...], xc[r], preferred_element_type=F32)
              - jnp.dot(ss_ref[...], xs[r], preferred_element_type=F32))
        yf_ref[b] = yf.astype(yf_ref.dtype)

    def store_heads(out_ref, head0, y, dtype):
        for hh in range(y.shape[1] // V_DIM):
            for b in range(nb):
                out_ref[b, head0 + hh] = y[b * seq:(b + 1) * seq,
                                           hh * V_DIM:(hh + 1) * V_DIM].astype(dtype)

    ones_blk = ones_ref[...]
    for pair in range(N_HEADS // 2):
        c0 = D_FOURIER + pair * MXU_COLS
        zq = jnp.dot(h, w_ref[:, c0:c0 + MXU_COLS], preferred_element_type=F32)
        zk = jnp.dot(h, w_ref[:, c0 + D_QK:c0 + D_QK + MXU_COLS],
                     preferred_element_type=F32)
        zv = jnp.dot(h, w_ref[:, c0 + 2 * D_QK:c0 + 2 * D_QK + MXU_COLS],
                     preferred_element_type=F32)
        q = _group_rmsnorm(zq, qg_ref[...], ones_blk)
        k = _group_rmsnorm(zk, kg_ref[...], ones_blk)
        for hh in range(2):
            ls = slice(hh * V_DIM, (hh + 1) * V_DIM)
            qh, kh = q[:, ls], k[:, ls]
            if use_rope:
                qh = _rope(qh, cos_ref[...], sdn_ref[...], sup_ref[...])
                kh = _rope(kh, cos_ref[...], sdn_ref[...], sup_ref[...])
            store_heads(q_ref, 2 * pair + hh, qh * Q_SCALE, BF16)
            store_heads(k_ref, 2 * pair + hh, kh, kv_dtype)
        store_heads(v_ref, 2 * pair, zv, kv_dtype)


def _dft_tables(n, scale):
    idx = np.arange(n, dtype=np.int64)
    ang = ((idx[:, None] * idx[None, :]) % n).astype(np.float64) * (2.0 * math.pi / n)
    return ((np.cos(ang) * scale).astype(np.float32),
            (np.sin(ang) * scale).astype(np.float32))


def _fourier_tables(s):
    c64, s64 = _dft_tables(FOURIER_HEAD_DIM, (s * FOURIER_HEAD_DIM) ** -0.5)
    eye = np.eye(D_FOURIER // FOURIER_HEAD_DIM, dtype=np.float32)
    tabs = (np.kron(eye, c64), np.kron(eye, s64)) + _dft_tables(s, 1.0)
    return tuple(jnp.asarray(t).astype(BF16) for t in tabs)


def _inproj(x, mod, mod_row, n1, w_in, qg4, kg4, rope, nb, kv_dtype):
    b, s, d = x.shape
    use_rope = rope is not None
    ones_blk = jnp.asarray(np.kron(np.eye(MXU_COLS // HEAD_DIM, dtype=np.float32),
                                   np.ones((HEAD_DIM, HEAD_DIM), np.float32))).astype(BF16)
    const = lambda shape: pl.BlockSpec(shape, lambda i: (0, 0))
    in_specs = [
        pl.BlockSpec((nb, s, d), lambda i: (i, 0, 0)),
        pl.BlockSpec((1, N_MOD, d), lambda i: (mod_row(i), 0, 0)),
        const((1, d)), const((d, D_IN)), const((1, MXU_COLS)), const((1, MXU_COLS)),
        const((MXU_COLS, MXU_COLS)), const((D_FOURIER, D_FOURIER)),
        const((D_FOURIER, D_FOURIER)), const((s, s)), const((s, s)),
    ]
    args = [x, mod, n1, w_in, qg4, kg4, ones_blk, *_fourier_tables(s)]
    if use_rope:
        assert nb == 1
        in_specs += [const((s, V_DIM))] * 3
        args += list(rope)
    head_spec = pl.BlockSpec((nb, N_HEADS, s, V_DIM), lambda i: (i, 0, 0, 0))
    return pl.pallas_call(
        functools.partial(_inproj_kernel, use_rope=use_rope, kv_dtype=kv_dtype),
        out_shape=(
            jax.ShapeDtypeStruct((b, s, D_FOURIER), BF16),
            jax.ShapeDtypeStruct((b, N_HEADS, s, V_DIM), BF16),
            jax.ShapeDtypeStruct((b, N_HEADS, s, V_DIM), kv_dtype),
            jax.ShapeDtypeStruct((b, N_HEADS, s, V_DIM), kv_dtype),
        ),
        grid=(b // nb,),
        in_specs=in_specs,
        out_specs=(
            pl.BlockSpec((nb, s, D_FOURIER), lambda i: (i, 0, 0)),
            head_spec, head_spec, head_spec,
        ),
        compiler_params=_params(1),
        name="inproj_rope" if use_rope else "inproj",
    )(*args)


def _scores(q, k):
    return lax.dot_general(q, k, (((1,), (1,)), ((), ())),
                           preferred_element_type=F32)


def _attn_kernel(lam_ref, subg_ref, q_ref, k_ref, v_ref, *rest, heads, has_ctx):
    if has_ctx:
        ck_ref, cv_ref, o_ref, kk_ref, vv_ref = rest
    else:
        o_ref, kk_ref, vv_ref = rest
    s_self = k_ref.shape[2]

    @pl.when(pl.program_id(2) == 0)
    def _():
        for hd in range(heads):
            kk_ref[hd, :s_self, :] = k_ref[0, hd].astype(BF16)
            vv_ref[hd, :s_self, :V_DIM] = v_ref[0, hd].astype(BF16)
            if has_ctx:
                kk_ref[hd, s_self:, :] = ck_ref[0, hd].astype(BF16)
                vv_ref[hd, s_self:, :V_DIM] = cv_ref[0, hd].astype(BF16)
            vv_ref[hd, :, V_DIM:] = jnp.ones((vv_ref.shape[1], V_DIM), BF16)

    lv = lam_ref[...]
    e1 = jnp.exp(jnp.sum(lv[0:1] * lv[1:2], axis=-1, keepdims=True))
    e2 = jnp.exp(jnp.sum(lv[2:3] * lv[3:4], axis=-1, keepdims=True))
    lam = e1 - e2 + LAM_INIT
    lane = lax.broadcasted_iota(jnp.int32, (1, V_DIM), 1)
    lo_mask = lane < HEAD_DIM
    subg = subg_ref[...]
    for hd in range(heads):
        q = q_ref[0, hd]
        zero = jnp.zeros_like(q)
        kk = kk_ref[hd]
        vv = vv_ref[hd]
        res = []
        for qc in (jnp.where(lo_mask, q, zero), jnp.where(lo_mask, zero, q)):
            s = _scores(qc, kk)
            p = jnp.exp2(s - s.max(axis=-1, keepdims=True)).astype(BF16)
            res.append(jnp.dot(p, vv, preferred_element_type=F32))
        r0, r1 = res
        o = r0[:, :V_DIM] / r0[:, V_DIM:] - lam * (r1[:, :V_DIM] / r1[:, V_DIM:])
        ms = jnp.mean(o * o, axis=-1, keepdims=True)
        o = o * lax.rsqrt(ms + EPS) * subg * (1.0 - LAM_INIT)
        o_ref[0, :, hd * V_DIM:(hd + 1) * V_DIM] = o.astype(o_ref.dtype)


def _attention(lamv, subg2, q, k, v, ctx, tq, heads):
    b, nh, s, _ = q.shape
    has_ctx = ctx is not None
    hg = nh // heads
    sk = s + (ctx[0].shape[2] if has_ctx else 0)
    qspec = pl.BlockSpec((1, heads, tq, V_DIM), lambda i, g, t: (i, g, t, 0))
    kvspec = pl.BlockSpec((1, heads, s, V_DIM), lambda i, g, t: (i, g, 0, 0))
    in_specs = [
        pl.BlockSpec((4, HEAD_DIM), lambda i, g, t: (0, 0)),
        pl.BlockSpec((1, V_DIM), lambda i, g, t: (0, 0)),
        qspec, kvspec, kvspec,
    ]
    args = [lamv, subg2, q, k, v]
    if has_ctx:
        p = ctx[0].shape[2]
        cspec = pl.BlockSpec((1, heads, p, V_DIM), lambda i, g, t: (i, g, 0, 0))
        in_specs += [cspec, cspec]
        args += list(ctx)
    return pl.pallas_call(
        functools.partial(_attn_kernel, heads=heads, has_ctx=has_ctx),
        out_shape=jax.ShapeDtypeStruct((b, s, nh * V_DIM), BF16),
        grid=(b, hg, s // tq),
        in_specs=in_specs,
        out_specs=pl.BlockSpec((1, tq, heads * V_DIM), lambda i, g, t: (i, t, g)),
        scratch_shapes=[
            pltpu.VMEM((heads, sk, V_DIM), BF16),
            pltpu.VMEM((heads, sk, 2 * V_DIM), BF16),
        ],
        compiler_params=_params(3),
        name="diff_attn_ctx" if has_ctx else "diff_attn",
    )(*args)


def _shifted_rows(gs_ref, slot, g, seq):
    rows, w = g.shape
    pitch = seq + SUBLANES
    for s in range(rows // seq):
        base = SUBLANES + s * pitch
        gs_ref[slot, base:base + seq, :w] = g[s * seq:(s + 1) * seq]
    shifted = []
    for off in (-1, 1):
        parts = [gs_ref[slot, SUBLANES + s * pitch + off:
                        SUBLANES + s * pitch + off + seq, :w]
                 for s in range(rows // seq)]
        shifted.append(parts[0] if len(parts) == 1 else jnp.concatenate(parts, axis=0))
    return shifted


def _mixffn_kernel(x_ref, yf_ref, o_ref, mod_ref, n2_ref, wo_ref, wg_ref, wu_ref,
                   cw_ref, cb_ref, wd_ref, y_ref, h2_ref, act_ref, gs_ref, *, seq):
    j = pl.program_id(1)
    mod = mod_ref[0]
    rows = x_ref.shape[0]

    @pl.when(j == 0)
    def _():
        mix = (jnp.dot(yf_ref[...], wo_ref[:D_FOURIER, :], preferred_element_type=F32)
               + jnp.dot(o_ref[...], wo_ref[D_FOURIER:, :], preferred_element_type=F32))
        x1 = x_ref[...] + mod[2:3] * mix
        y_ref[...] = x1
        h2 = _rms_rows(x1, n2_ref[...]) * (1.0 + mod[4:5]) + mod[3:4]
        h2_ref[...] = h2.astype(BF16)
        for slot in range(gs_ref.shape[0]):
            for s in range(rows // seq + 1):
                r0 = s * (seq + SUBLANES)
                gs_ref[slot, r0:r0 + SUBLANES, :] = jnp.zeros(
                    (SUBLANES, gs_ref.shape[2]), F32)

    h2 = h2_ref[...]
    cw = cw_ref[...]
    cb = cb_ref[...]
    for n, (c0, w) in enumerate(FF_SUBCHUNKS):
        if w == MXU_COLS:
            g = jnp.dot(h2, wg_ref[:, c0:c0 + w], preferred_element_type=F32)
            u = jnp.dot(h2, wu_ref[:, c0:c0 + w], preferred_element_type=F32)
        else:
            wgu = jnp.concatenate([wg_ref[:, c0:c0 + w], wu_ref[:, c0:c0 + w]], axis=1)
            gu = jnp.dot(h2, wgu, preferred_element_type=F32)
            g, u = gu[:, :w], gu[:, w:]
        g_prev, g_next = _shifted_rows(gs_ref, n % gs_ref.shape[0], g, seq)
        gc = (g * cw[1:2, c0:c0 + w] + g_prev * cw[0:1, c0:c0 + w]
              + g_next * cw[2:3, c0:c0 + w] + cb[:, c0:c0 + w])
        hg = 0.5 * gc
        act_ref[:, c0:c0 + w] = ((hg * u) * (1.0 + jnp.tanh(hg))).astype(BF16)
    y_ref[...] += mod[5:6] * jnp.dot(act_ref[...], wd_ref[...],
                                     preferred_element_type=F32)


def _mixffn(x2, yf2, o2, mod, mod_row, n2, w_out, w_gate, w_up, conv_w, conv_b2,
            w_down, seq):
    m, d = x2.shape
    tm = FFN_ROWS
    tf = FF_BLOCK
    row = lambda i, j: (i, 0)
    stage_rows = SUBLANES + (tm // seq) * (seq + SUBLANES)
    return pl.pallas_call(
        functools.partial(_mixffn_kernel, seq=seq),
        out_shape=jax.ShapeDtypeStruct((m, d), F32),
        grid=(m // tm, D_FF // tf),
        in_specs=[
            pl.BlockSpec((tm, d), row),
            pl.BlockSpec((tm, D_FOURIER), row),
            pl.BlockSpec((tm, D_ATTN), row),
            pl.BlockSpec((1, N_MOD, d), lambda i, j: (mod_row(i), 0, 0)),
            pl.BlockSpec((1, d), lambda i, j: (0, 0)),
            pl.BlockSpec((d, d), lambda i, j: (0, 0)),
            pl.BlockSpec((d, tf), lambda i, j: (0, j)),
            pl.BlockSpec((d, tf), lambda i, j: (0, j)),
            pl.BlockSpec((3, tf), lambda i, j: (0, j)),
            pl.BlockSpec((1, tf), lambda i, j: (0, j)),
            pl.BlockSpec((tf, d), lambda i, j: (j, 0)),
        ],
        out_specs=pl.BlockSpec((tm, d), row),
        scratch_shapes=[
            pltpu.VMEM((tm, d), BF16),
            pltpu.VMEM((tm, tf), BF16),
            pltpu.VMEM((2, stage_rows, MXU_COLS), F32),
        ],
        compiler_params=_params(2),
        name="mix_ffn",
    )(x2, yf2, o2, mod, n2, w_out, w_gate, w_up, conv_w, conv_b2, w_down)


def _rope_tables(n_tokens):
    rows = n_tokens // GRID_W
    row = np.repeat(np.arange(rows, dtype=np.float64), GRID_W)
    col = np.tile(np.arange(GRID_W, dtype=np.float64), rows)
    n_freq = HEAD_DIM // 4
    inv = ROPE_THETA ** (-np.arange(n_freq, dtype=np.float64) / n_freq)
    ang_r = row[:, None] * inv
    ang_c = col[:, None] * inv
    zero = np.zeros_like(ang_r)
    cos64 = np.concatenate([np.cos(ang_r)] * 2 + [np.cos(ang_c)] * 2, axis=-1)
    sdn64 = np.concatenate([-np.sin(ang_r), zero, -np.sin(ang_c), zero], axis=-1)
    sup64 = np.concatenate([zero, np.sin(ang_r), zero, np.sin(ang_c)], axis=-1)
    two = lambda t: jnp.asarray(np.concatenate([t, t], axis=-1).astype(np.float32))
    return two(cos64), two(sdn64), two(sup64)


def _layer(x, mod, mod_row_tile, wts, lamv, rope, ctx, tq, heads, kv_dtype):
    (n1, n2, w_in, qg4, kg4, subg2, w_out, w_gate, w_up, conv_w, conv_b2, w_down) = wts
    b, s, d = x.shape
    yf, q, k, v = _inproj(x, mod, mod_row_tile, n1, w_in, qg4, kg4, rope,
                          FFN_ROWS // s, kv_dtype)
    o = _attention(lamv, subg2, q, k, v, ctx, tq, heads)
    y = _mixffn(x.reshape(b * s, d), yf.reshape(b * s, D_FOURIER),
                o.reshape(b * s, D_ATTN), mod, mod_row_tile, n2, w_out, w_gate,
                w_up, conv_w, conv_b2, w_down, s)
    return y.reshape(b, s, d), k, v


def kernel(x_prompt, x_sample, cache_k, cache_v, c, c_ctx, norm1_g, norm2_g, w_ada,
           b_ada, w_in, q_norm_g, k_norm_g, lam_q1, lam_k1, lam_q2, lam_k2, subln_g,
           w_out, w_gate, w_up, conv_w, conv_b, w_down):
    bp, sp, d = x_prompt.shape
    bs, ss, _ = x_sample.shape
    assert FFN_ROWS % sp == 0 and FFN_ROWS == ss

    cond = jnp.zeros((MOD_ROWS, d), F32).at[0].set(c_ctx).at[1:1 + bs].set(c)
    mod = _ada(cond, w_ada[0], b_ada[0]).reshape(MOD_ROWS, N_MOD, d)

    tile4 = lambda t: jnp.tile(t, MXU_COLS // HEAD_DIM)[None, :]
    wts = (norm1_g[0][None, :], norm2_g[0][None, :], w_in[0].astype(BF16),
           tile4(q_norm_g[0]), tile4(k_norm_g[0]), subln_g[0][None, :],
           w_out[0].astype(BF16), w_gate[0].astype(BF16), w_up[0].astype(BF16),
           conv_w[0], conv_b[0][None, :], w_down[0].astype(BF16))
    lamv = jnp.stack([lam_q1[0], lam_k1[0], lam_q2[0], lam_k2[0]])

    y_p, k_p, v_p = _layer(
        x_prompt, mod, lambda i: 0, wts, lamv, None, None,
        tq=sp, heads=N_HEADS, kv_dtype=F32)

    p = cache_k.shape[3]
    ctx = (cache_k[:, 0].reshape(bs, N_HEADS, p, V_DIM), cache_v[:, 0])
    y_s, _, _ = _layer(
        x_sample, mod, lambda i: i + 1, wts, lamv,
        _rope_tables(ss), ctx, tq=512, heads=6, kv_dtype=BF16)

    new_k = k_p.reshape(bp, 1, N_HEADS, sp, 2, HEAD_DIM)
    new_v = v_p.reshape(bp, 1, N_HEADS, sp, V_DIM)
    return (y_p, y_s, new_k, new_v)
```

```python
import functools
import math

import jax
import jax.numpy as jnp
import numpy as np
from jax import lax
from jax.experimental import pallas as pl
from jax.experimental.pallas import tpu as pltpu

F32 = jnp.float32
BF16 = jnp.bfloat16

D_MODEL = 1024
GRID_W = 64
N_HEADS = 6
HEAD_DIM = 64
V_DIM = 2 * HEAD_DIM
D_FOURIER = 256
FOURIER_HEAD_DIM = 64
D_QK = N_HEADS * V_DIM
D_ATTN = N_HEADS * V_DIM
D_IN = D_FOURIER + 2 * D_QK + D_ATTN
D_FF = 2816
ROPE_THETA = 10000.0
EPS = 1e-6
LAM_INIT = 0.8 - 0.6 * math.exp(-0.3 * 0)
Q_SCALE = HEAD_DIM ** -0.5 * math.log2(math.e)
N_MOD = 6
MOD_ROWS = 16

SUBLANES = 8
MXU_COLS = 256
FF_SUBCHUNKS = tuple((c, min(MXU_COLS, D_FF - c)) for c in range(0, D_FF, MXU_COLS))
FFN_ROWS = 1024
VMEM_LIMIT = 58 * 1024 * 1024


def _params(n_axes):
    return pltpu.CompilerParams(
        dimension_semantics=("arbitrary",) * n_axes,
        vmem_limit_bytes=VMEM_LIMIT,
    )


def _ada_kernel(cond_ref, w_ref, b_ref, o_ref):
    c = cond_ref[...]
    s = c * jax.nn.sigmoid(c)
    o_ref[...] = jnp.dot(s.astype(BF16), w_ref[...].astype(BF16),
                         preferred_element_type=F32) + b_ref[...]


def _ada(cond, w_ada, b_ada):
    n = w_ada.shape[1]
    tn = 1536
    return pl.pallas_call(
        _ada_kernel,
        out_shape=jax.ShapeDtypeStruct((MOD_ROWS, n), F32),
        grid=(n // tn,),
        in_specs=[
            pl.BlockSpec((MOD_ROWS, D_MODEL), lambda j: (0, 0)),
            pl.BlockSpec((D_MODEL, tn), lambda j: (0, j)),
            pl.BlockSpec((1, tn), lambda j: (0, j)),
        ],
        out_specs=pl.BlockSpec((MOD_ROWS, tn), lambda j: (0, j)),
        compiler_params=_params(1),
        name="ada_mod",
    )(cond, w_ada, b_ada.reshape(1, n))


def _rms_rows(x, gain):
    ms = jnp.mean(x * x, axis=-1, keepdims=True)
    return x * lax.rsqrt(ms + EPS) * gain


def _group_rmsnorm(x, gain, ones_blk):
    sq = x * x
    hi = sq.astype(BF16)
    lo = (sq - hi.astype(F32)).astype(BF16)
    ss = (jnp.dot(hi, ones_blk, preferred_element_type=F32)
          + jnp.dot(lo, ones_blk, preferred_element_type=F32))
    return x * lax.rsqrt(ss * (1.0 / HEAD_DIM) + EPS) * gain


def _rope(x, cos, sin_dn, sin_up):
    return (x * cos + pltpu.roll(x, V_DIM - 16, 1) * sin_dn
            + pltpu.roll(x, 16, 1) * sin_up)


def _inproj_kernel(x_ref, mod_ref, n1_ref, w_ref, qg_ref, kg_ref, ones_ref, cc_ref,
                   sc_ref, cs_ref, ss_ref, *rest, use_rope, kv_dtype):
    if use_rope:
        cos_ref, sdn_ref, sup_ref, yf_ref, q_ref, k_ref, v_ref = rest
    else:
        yf_ref, q_ref, k_ref, v_ref = rest
    nb, seq, d = x_ref.shape
    x = x_ref[...].reshape(nb * seq, d)
    mod = mod_ref[0]
    h = (_rms_rows(x, n1_ref[...]) * (1.0 + mod[1:2]) + mod[0:1]).astype(BF16)

    xf = jnp.dot(h, w_ref[:, :D_FOURIER], preferred_element_type=F32).astype(BF16)
    xc = jnp.dot(xf, cc_ref[...], preferred_element_type=F32).astype(BF16)
    xs = jnp.dot(xf, sc_ref[...], preferred_element_type=F32).astype(BF16)
    for b in range(nb):
        r = slice(b * seq, (b + 1) * seq)
        yf = (jnp.dot(cs_ref[...], xc[r], preferred_element_type=F32)
              - jnp.dot(ss_ref[...], xs[r], preferred_element_type=F32))
        yf_ref[b] = yf.astype(yf_ref.dtype)

    def store_heads(out_ref, head0, y, dtype):
        for hh in range(y.shape[1] // V_DIM):
            for b in range(nb):
                out_ref[b, head0 + hh] = y[b * seq:(b + 1) * seq,
                                           hh * V_DIM:(hh + 1) * V_DIM].astype(dtype)

    ones_blk = ones_ref[...]
    for pair in range(N_HEADS // 2):
        c0 = D_FOURIER + pair * MXU_COLS
        zq = jnp.dot(h, w_ref[:, c0:c0 + MXU_COLS], preferred_element_type=F32)
        zk = jnp.dot(h, w_ref[:, c0 + D_QK:c0 + D_QK + MXU_COLS],
                     preferred_element_type=F32)
        zv = jnp.dot(h, w_ref[:, c0 + 2 * D_QK:c0 + 2 * D_QK + MXU_COLS],
                     preferred_element_type=F32)
        q = _group_rmsnorm(zq, qg_ref[...], ones_blk)
        k = _group_rmsnorm(zk, kg_ref[...], ones_blk)
        for hh in range(2):
            ls = slice(hh * V_DIM, (hh + 1) * V_DIM)
            qh, kh = q[:, ls], k[:, ls]
            if use_rope:
                qh = _rope(qh, cos_ref[...], sdn_ref[...], sup_ref[...])
                kh = _rope(kh, cos_ref[...], sdn_ref[...], sup_ref[...])
            store_heads(q_ref, 2 * pair + hh, qh * Q_SCALE, BF16)
            store_heads(k_ref, 2 * pair + hh, kh, kv_dtype)
        store_heads(v_ref, 2 * pair, zv, kv_dtype)


def _dft_tables(n, scale):
    idx = np.arange(n, dtype=np.int64)
    ang = ((idx[:, None] * idx[None, :]) % n).astype(np.float64) * (2.0 * math.pi / n)
    return ((np.cos(ang) * scale).astype(np.float32),
            (np.sin(ang) * scale).astype(np.float32))


def _fourier_tables(s):
    c64, s64 = _dft_tables(FOURIER_HEAD_DIM, (s * FOURIER_HEAD_DIM) ** -0.5)
    eye = np.eye(D_FOURIER // FOURIER_HEAD_DIM, dtype=np.float32)
    tabs = (np.kron(eye, c64), np.kron(eye, s64)) + _dft_tables(s, 1.0)
    return tuple(jnp.asarray(t).astype(BF16) for t in tabs)


def _inproj(x, mod, mod_row, n1, w_in, qg4, kg4, rope, nb, kv_dtype):
    b, s, d = x.shape
    use_rope = rope is not None
    ones_blk = jnp.asarray(np.kron(np.eye(MXU_COLS // HEAD_DIM, dtype=np.float32),
                                   np.ones((HEAD_DIM, HEAD_DIM), np.float32))).astype(BF16)
    const = lambda shape: pl.BlockSpec(shape, lambda i: (0, 0))
    in_specs = [
        pl.BlockSpec((nb, s, d), lambda i: (i, 0, 0)),
        pl.BlockSpec((1, N_MOD, d), lambda i: (mod_row(i), 0, 0)),
        const((1, d)), const((d, D_IN)), const((1, MXU_COLS)), const((1, MXU_COLS)),
        const((MXU_COLS, MXU_COLS)), const((D_FOURIER, D_FOURIER)),
        const((D_FOURIER, D_FOURIER)), const((s, s)), const((s, s)),
    ]
    args = [x, mod, n1, w_in, qg4, kg4, ones_blk, *_fourier_tables(s)]
    if use_rope:
        assert nb == 1
        in_specs += [const((s, V_DIM))] * 3
        args += list(rope)
    head_spec = pl.BlockSpec((nb, N_HEADS, s, V_DIM), lambda i: (i, 0, 0, 0))
    return pl.pallas_call(
        functools.partial(_inproj_kernel, use_rope=use_rope, kv_dtype=kv_dtype),
        out_shape=(
            jax.ShapeDtypeStruct((b, s, D_FOURIER), BF16),
            jax.ShapeDtypeStruct((b, N_HEADS, s, V_DIM), BF16),
            jax.ShapeDtypeStruct((b, N_HEADS, s, V_DIM), kv_dtype),
            jax.ShapeDtypeStruct((b, N_HEADS, s, V_DIM), kv_dtype),
        ),
        grid=(b // nb,),
        in_specs=in_specs,
        out_specs=(
            pl.BlockSpec((nb, s, D_FOURIER), lambda i: (i, 0, 0)),
            head_spec, head_spec, head_spec,
        ),
        compiler_params=_params(1),
        name="inproj_rope" if use_rope else "inproj",
    )(*args)


def _scores(q, k):
    return lax.dot_general(q, k, (((1,), (1,)), ((), ())),
                           preferred_element_type=F32)


def _attn_kernel(lam_ref, subg_ref, q_ref, k_ref, v_ref, *rest, heads, has_ctx):
    if has_ctx:
        ck_ref, cv_ref, o_ref, kk_ref, vv_ref = rest
    else:
        o_ref, kk_ref, vv_ref = rest
    s_self = k_ref.shape[2]

    @pl.when(pl.program_id(2) == 0)
    def _():
        for hd in range(heads):
            kk_ref[hd, :s_self, :] = k_ref[0, hd].astype(BF16)
            vv_ref[hd, :s_self, :V_DIM] = v_ref[0, hd].astype(BF16)
            if has_ctx:
                kk_ref[hd, s_self:, :] = ck_ref[0, hd].astype(BF16)
                vv_ref[hd, s_self:, :V_DIM] = cv_ref[0, hd].astype(BF16)
            vv_ref[hd, :, V_DIM:] = jnp.ones((vv_ref.shape[1], V_DIM), BF16)

    lv = lam_ref[...]
    e1 = jnp.exp(jnp.sum(lv[0:1] * lv[1:2], axis=-1, keepdims=True))
    e2 = jnp.exp(jnp.sum(lv[2:3] * lv[3:4], axis=-1, keepdims=True))
    lam = e1 - e2 + LAM_INIT
    lane = lax.broadcasted_iota(jnp.int32, (1, V_DIM), 1)
    lo_mask = lane < HEAD_DIM
    subg = subg_ref[...]
    for hd in range(heads):
        q = q_ref[0, hd]
        zero = jnp.zeros_like(q)
        kk = kk_ref[hd]
        vv = vv_ref[hd]
        res = []
        for qc in (jnp.where(lo_mask, q, zero), jnp.where(lo_mask, zero, q)):
            s = _scores(qc, kk)
            p = jnp.exp2(s - s.max(axis=-1, keepdims=True)).astype(BF16)
            res.append(jnp.dot(p, vv, preferred_element_type=F32))
        r0, r1 = res
        o = r0[:, :V_DIM] / r0[:, V_DIM:] - lam * (r1[:, :V_DIM] / r1[:, V_DIM:])
        ms = jnp.mean(o * o, axis=-1, keepdims=True)
        o = o * lax.rsqrt(ms + EPS) * subg * (1.0 - LAM_INIT)
        o_ref[0, :, hd * V_DIM:(hd + 1) * V_DIM] = o.astype(o_ref.dtype)


def _attention(lamv, subg2, q, k, v, ctx, tq, heads):
    b, nh, s, _ = q.shape
    has_ctx = ctx is not None
    hg = nh // heads
    sk = s + (ctx[0].shape[2] if has_ctx else 0)
    qspec = pl.BlockSpec((1, heads, tq, V_DIM), lambda i, g, t: (i, g, t, 0))
    kvspec = pl.BlockSpec((1, heads, s, V_DIM), lambda i, g, t: (i, g, 0, 0))
    in_specs = [
        pl.BlockSpec((4, HEAD_DIM), lambda i, g, t: (0, 0)),
        pl.BlockSpec((1, V_DIM), lambda i, g, t: (0, 0)),
        qspec, kvspec, kvspec,
    ]
    args = [lamv, subg2, q, k, v]
    if has_ctx:
        p = ctx[0].shape[2]
        cspec = pl.BlockSpec((1, heads, p, V_DIM), lambda i, g, t: (i, g, 0, 0))
        in_specs += [cspec, cspec]
        args += list(ctx)
    return pl.pallas_call(
        functools.partial(_attn_kernel, heads=heads, has_ctx=has_ctx),
        out_shape=jax.ShapeDtypeStruct((b, s, nh * V_DIM), BF16),
        grid=(b, hg, s // tq),
        in_specs=in_specs,
        out_specs=pl.BlockSpec((1, tq, heads * V_DIM), lambda i, g, t: (i, t, g)),
        scratch_shapes=[
            pltpu.VMEM((heads, sk, V_DIM), BF16),
            pltpu.VMEM((heads, sk, 2 * V_DIM), BF16),
        ],
        compiler_params=_params(3),
        name="diff_attn_ctx" if has_ctx else "diff_attn",
    )(*args)


def _shifted_rows(gs_ref, slot, g, seq):
    rows, w = g.shape
    pitch = seq + SUBLANES
    for s in range(rows // seq):
        base = SUBLANES + s * pitch
        gs_ref[slot, base:base + seq, :w] = g[s * seq:(s + 1) * seq]
    shifted = []
    for off in (-1, 1):
        parts = [gs_ref[slot, SUBLANES + s * pitch + off:
                        SUBLANES + s * pitch + off + seq, :w]
                 for s in range(rows // seq)]
        shifted.append(parts[0] if len(parts) == 1 else jnp.concatenate(parts, axis=0))
    return shifted


def _mixffn_kernel(x_ref, yf_ref, o_ref, mod_ref, n2_ref, wo_ref, wg_ref, wu_ref,
                   cw_ref, cb_ref, wd_ref, y_ref, h2_ref, act_ref, gs_ref, *, seq):
    mod = mod_ref[0]
    rows = x_ref.shape[0]

    mix = (jnp.dot(yf_ref[...], wo_ref[:D_FOURIER, :], preferred_element_type=F32)
           + jnp.dot(o_ref[...], wo_ref[D_FOURIER:, :], preferred_element_type=F32))
    x1 = x_ref[...] + mod[2:3] * mix
    y_ref[...] = x1
    h2 = _rms_rows(x1, n2_ref[...]) * (1.0 + mod[4:5]) + mod[3:4]
    h2_ref[...] = h2.astype(BF16)
    for slot in range(gs_ref.shape[0]):
        for s in range(rows // seq + 1):
            r0 = s * (seq + SUBLANES)
            gs_ref[slot, r0:r0 + SUBLANES, :] = jnp.zeros(
                (SUBLANES, gs_ref.shape[2]), F32)

    h2 = h2_ref[...]
    cw = cw_ref[...]
    cb = cb_ref[...]
    for n, (c0, w) in enumerate(FF_SUBCHUNKS):
        if w == MXU_COLS:
            g = jnp.dot(h2, wg_ref[:, c0:c0 + w], preferred_element_type=F32)
            u = jnp.dot(h2, wu_ref[:, c0:c0 + w], preferred_element_type=F32)
        else:
            wgu = jnp.concatenate([wg_ref[:, c0:c0 + w], wu_ref[:, c0:c0 + w]], axis=1)
            gu = jnp.dot(h2, wgu, preferred_element_type=F32)
            g, u = gu[:, :w], gu[:, w:]
        g_prev, g_next = _shifted_rows(gs_ref, n % gs_ref.shape[0], g, seq)
        gc = (g * cw[1:2, c0:c0 + w] + g_prev * cw[0:1, c0:c0 + w]
              + g_next * cw[2:3, c0:c0 + w] + cb[:, c0:c0 + w])
        hg = 0.5 * gc
        act_ref[:, c0:c0 + w] = ((hg * u) * (1.0 + jnp.tanh(hg))).astype(BF16)
    y_ref[...] += mod[5:6] * jnp.dot(act_ref[...], wd_ref[...],
                                     preferred_element_type=F32)


def _mixffn(x2, yf2, o2, mod, mod_row, n2, w_out, w_gate, w_up, conv_w, conv_b2,
            w_down, seq):
    m, d = x2.shape
    tm = FFN_ROWS
    row = lambda i: (i, 0)
    const = lambda shape: pl.BlockSpec(shape, lambda i: (0, 0))
    stage_rows = SUBLANES + (tm // seq) * (seq + SUBLANES)
    return pl.pallas_call(
        functools.partial(_mixffn_kernel, seq=seq),
        out_shape=jax.ShapeDtypeStruct((m, d), F32),
        grid=(m // tm,),
        in_specs=[
            pl.BlockSpec((tm, d), row),
            pl.BlockSpec((tm, D_FOURIER), row),
            pl.BlockSpec((tm, D_ATTN), row),
            pl.BlockSpec((1, N_MOD, d), lambda i: (mod_row(i), 0, 0)),
            const((1, d)), const((d, d)), const((d, D_FF)), const((d, D_FF)),
            const((3, D_FF)), const((1, D_FF)), const((D_FF, d)),
        ],
        out_specs=pl.BlockSpec((tm, d), row),
        scratch_shapes=[
            pltpu.VMEM((tm, d), BF16),
            pltpu.VMEM((tm, D_FF), BF16),
            pltpu.VMEM((2, stage_rows, MXU_COLS), F32),
        ],
        compiler_params=_params(1),
        name="mix_ffn",
    )(x2, yf2, o2, mod, n2, w_out, w_gate, w_up, conv_w, conv_b2, w_down)


def _rope_tables(n_tokens):
    rows = n_tokens // GRID_W
    row = np.repeat(np.arange(rows, dtype=np.float64), GRID_W)
    col = np.tile(np.arange(GRID_W, dtype=np.float64), rows)
    n_freq = HEAD_DIM // 4
    inv = ROPE_THETA ** (-np.arange(n_freq, dtype=np.float64) / n_freq)
    ang_r = row[:, None] * inv
    ang_c = col[:, None] * inv
    zero = np.zeros_like(ang_r)
    cos64 = np.concatenate([np.cos(ang_r)] * 2 + [np.cos(ang_c)] * 2, axis=-1)
    sdn64 = np.concatenate([-np.sin(ang_r), zero, -np.sin(ang_c), zero], axis=-1)
    sup64 = np.concatenate([zero, np.sin(ang_r), zero, np.sin(ang_c)], axis=-1)
    two = lambda t: jnp.asarray(np.concatenate([t, t], axis=-1).astype(np.float32))
    return two(cos64), two(sdn64), two(sup64)


def _layer(x, mod, mod_row_tile, wts, lamv, rope, ctx, tq, heads, kv_dtype):
    (n1, n2, w_in, qg4, kg4, subg2, w_out, w_gate, w_up, conv_w, conv_b2, w_down) = wts
    b, s, d = x.shape
    yf, q, k, v = _inproj(x, mod, mod_row_tile, n1, w_in, qg4, kg4, rope,
                          FFN_ROWS // s, kv_dtype)
    o = _attention(lamv, subg2, q, k, v, ctx, tq, heads)
    y = _mixffn(x.reshape(b * s, d), yf.reshape(b * s, D_FOURIER),
                o.reshape(b * s, D_ATTN), mod, mod_row_tile, n2, w_out, w_gate,
                w_up, conv_w, conv_b2, w_down, s)
    return y.reshape(b, s, d), k, v


def kernel(x_prompt, x_sample, cache_k, cache_v, c, c_ctx, norm1_g, norm2_g, w_ada,
           b_ada, w_in, q_norm_g, k_norm_g, lam_q1, lam_k1, lam_q2, lam_k2, subln_g,
           w_out, w_gate, w_up, conv_w, conv_b, w_down):
    bp, sp, d = x_prompt.shape
    bs, ss, _ = x_sample.shape
    assert FFN_ROWS % sp == 0 and FFN_ROWS == ss

    cond = jnp.zeros((MOD_ROWS, d), F32).at[0].set(c_ctx).at[1:1 + bs].set(c)
    mod = _ada(cond, w_ada[0], b_ada[0]).reshape(MOD_ROWS, N_MOD, d)

    tile4 = lambda t: jnp.tile(t, MXU_COLS // HEAD_DIM)[None, :]
    wts = (norm1_g[0][None, :], norm2_g[0][None, :], w_in[0].astype(BF16),
           tile4(q_norm_g[0]), tile4(k_norm_g[0]), subln_g[0][None, :],
           w_out[0].astype(BF16), w_gate[0].astype(BF16), w_up[0].astype(BF16),
           conv_w[0], conv_b[0][None, :], w_down[0].astype(BF16))
    lamv = jnp.stack([lam_q1[0], lam_k1[0], lam_q2[0], lam_k2[0]])

    y_p, k_p, v_p = _layer(
        x_prompt, mod, lambda i: 0, wts, lamv, None, None,
        tq=sp, heads=N_HEADS, kv_dtype=F32)

    p = cache_k.shape[3]
    ctx = (cache_k[:, 0].reshape(bs, N_HEADS, p, V_DIM), cache_v[:, 0])
    y_s, _, _ = _layer(
        x_sample, mod, lambda i: i + 1, wts, lamv,
        _rope_tables(ss), ctx, tq=512, heads=6, kv_dtype=BF16)

    new_k = k_p.reshape(bp, 1, N_HEADS, sp, 2, HEAD_DIM)
    new_v = v_p.reshape(bp, 1, N_HEADS, sp, V_DIM)
    return (y_p, y_s, new_k, new_v)
```

```python
import functools
import math

import jax
import jax.numpy as jnp
import numpy as np
from jax import lax
from jax.experimental import pallas as pl
from jax.experimental.pallas import tpu as pltpu

F32 = jnp.float32
BF16 = jnp.bfloat16

D_MODEL = 1024
GRID_W = 64
N_HEADS = 6
HEAD_DIM = 64
V_DIM = 2 * HEAD_DIM
D_FOURIER = 256
FOURIER_HEAD_DIM = 64
D_QK = N_HEADS * V_DIM
D_ATTN = N_HEADS * V_DIM
D_IN = D_FOURIER + 2 * D_QK + D_ATTN
D_FF = 2816
ROPE_THETA = 10000.0
EPS = 1e-6
LAM_INIT = 0.8 - 0.6 * math.exp(-0.3 * 0)
Q_SCALE = HEAD_DIM ** -0.5 * math.log2(math.e)
N_MOD = 6
MOD_ROWS = 16

SUBLANES = 8
MXU_COLS = 256
FF_SUBCHUNKS = tuple((c, min(MXU_COLS, D_FF - c)) for c in range(0, D_FF, MXU_COLS))
FFN_ROWS = 1024
VMEM_LIMIT = 58 * 1024 * 1024


def _params(n_axes):
    return pltpu.CompilerParams(
        dimension_semantics=("arbitrary",) * n_axes,
        vmem_limit_bytes=VMEM_LIMIT,
    )


def _ada_kernel(cond_ref, w_ref, b_ref, o_ref):
    c = cond_ref[...]
    s = c * jax.nn.sigmoid(c)
    o_ref[...] = jnp.dot(s.astype(BF16), w_ref[...].astype(BF16),
                         preferred_element_type=F32) + b_ref[...]


def _ada(cond, w_ada, b_ada):
    n = w_ada.shape[1]
    tn = 1536
    return pl.pallas_call(
        _ada_kernel,
        out_shape=jax.ShapeDtypeStruct((MOD_ROWS, n), F32),
        grid=(n // tn,),
        in_specs=[
            pl.BlockSpec((MOD_ROWS, D_MODEL), lambda j: (0, 0)),
            pl.BlockSpec((D_MODEL, tn), lambda j: (0, j)),
            pl.BlockSpec((1, tn), lambda j: (0, j)),
        ],
        out_specs=pl.BlockSpec((MOD_ROWS, tn), lambda j: (0, j)),
        compiler_params=_params(1),
        name="ada_mod",
    )(cond, w_ada, b_ada.reshape(1, n))


def _rms_rows(x, gain):
    ms = jnp.mean(x * x, axis=-1, keepdims=True)
    return x * lax.rsqrt(ms + EPS) * gain


def _group_rmsnorm(x, gain, ones_blk):
    ss = jnp.dot((x * x).astype(BF16), ones_blk, preferred_element_type=F32)
    return x * lax.rsqrt(ss * (1.0 / HEAD_DIM) + EPS) * gain


def _rope(x, cos, sin_dn, sin_up):
    return (x * cos + pltpu.roll(x, V_DIM - 16, 1) * sin_dn
            + pltpu.roll(x, 16, 1) * sin_up)


def _inproj_kernel(x_ref, mod_ref, n1_ref, w_ref, qg_ref, kg_ref, ones_ref, cc_ref,
                   sc_ref, cs_ref, ss_ref, *rest, use_rope, kv_dtype):
    if use_rope:
        cos_ref, sdn_ref, sup_ref, yf_ref, q_ref, k_ref, v_ref = rest
    else:
        yf_ref, q_ref, k_ref, v_ref = rest
    nb, seq, d = x_ref.shape
    x = x_ref[...].reshape(nb * seq, d)
    mod = mod_ref[0]
    h = (_rms_rows(x, n1_ref[...]) * (1.0 + mod[1:2]) + mod[0:1]).astype(BF16)

    def project(c0):
        return jnp.dot(h, w_ref[:, c0:c0 + MXU_COLS].astype(BF16),
                       preferred_element_type=F32)

    xf = project(0).astype(BF16)
    xc = jnp.dot(xf, cc_ref[...], preferred_element_type=F32).astype(BF16)
    xs = jnp.dot(xf, sc_ref[...], preferred_element_type=F32).astype(BF16)
    for b in range(nb):
        r = slice(b * seq, (b + 1) * seq)
        yf = (jnp.dot(cs_ref[...], xc[r], preferred_element_type=F32)
              - jnp.dot(ss_ref[...], xs[r], preferred_element_type=F32))
        yf_ref[b] = yf.astype(yf_ref.dtype)

    def store_heads(out_ref, head0, y, dtype):
        for hh in range(y.shape[1] // V_DIM):
            for b in range(nb):
                out_ref[b, head0 + hh] = y[b * seq:(b + 1) * seq,
                                           hh * V_DIM:(hh + 1) * V_DIM].astype(dtype)

    ones_blk = ones_ref[...]
    for pair in range(N_HEADS // 2):
        c0 = D_FOURIER + pair * MXU_COLS
        zq = project(c0)
        zk = project(c0 + D_QK)
        zv = project(c0 + 2 * D_QK)
        q = _group_rmsnorm(zq, qg_ref[...], ones_blk)
        k = _group_rmsnorm(zk, kg_ref[...], ones_blk)
        for hh in range(2):
            ls = slice(hh * V_DIM, (hh + 1) * V_DIM)
            qh, kh = q[:, ls], k[:, ls]
            if use_rope:
                qh = _rope(qh, cos_ref[...], sdn_ref[...], sup_ref[...])
                kh = _rope(kh, cos_ref[...], sdn_ref[...], sup_ref[...])
            store_heads(q_ref, 2 * pair + hh, qh * Q_SCALE, BF16)
            store_heads(k_ref, 2 * pair + hh, kh, kv_dtype)
        store_heads(v_ref, 2 * pair, zv, kv_dtype)


def _dft_tables(n, scale):
    idx = np.arange(n, dtype=np.int64)
    ang = ((idx[:, None] * idx[None, :]) % n).astype(np.float64) * (2.0 * math.pi / n)
    return ((np.cos(ang) * scale).astype(np.float32),
            (np.sin(ang) * scale).astype(np.float32))


def _fourier_tables(s):
    c64, s64 = _dft_tables(FOURIER_HEAD_DIM, (s * FOURIER_HEAD_DIM) ** -0.5)
    eye = np.eye(D_FOURIER // FOURIER_HEAD_DIM, dtype=np.float32)
    tabs = (np.kron(eye, c64), np.kron(eye, s64)) + _dft_tables(s, 1.0)
    return tuple(jnp.asarray(t).astype(BF16) for t in tabs)


def _inproj(x, mod, mod_row, n1, w_in, qg4, kg4, rope, nb, kv_dtype):
    b, s, d = x.shape
    use_rope = rope is not None
    ones_blk = jnp.asarray(np.kron(np.eye(MXU_COLS // HEAD_DIM, dtype=np.float32),
                                   np.ones((HEAD_DIM, HEAD_DIM), np.float32))).astype(BF16)
    const = lambda shape: pl.BlockSpec(shape, lambda i: (0, 0))
    in_specs = [
        pl.BlockSpec((nb, s, d), lambda i: (i, 0, 0)),
        pl.BlockSpec((1, N_MOD, d), lambda i: (mod_row(i), 0, 0)),
        const((1, d)), const((d, D_IN)), const((1, MXU_COLS)), const((1, MXU_COLS)),
        const((MXU_COLS, MXU_COLS)), const((D_FOURIER, D_FOURIER)),
        const((D_FOURIER, D_FOURIER)), const((s, s)), const((s, s)),
    ]
    args = [x, mod, n1, w_in, qg4, kg4, ones_blk, *_fourier_tables(s)]
    if use_rope:
        assert nb == 1
        in_specs += [const((s, V_DIM))] * 3
        args += list(rope)
    head_spec = pl.BlockSpec((nb, N_HEADS, s, V_DIM), lambda i: (i, 0, 0, 0))
    return pl.pallas_call(
        functools.partial(_inproj_kernel, use_rope=use_rope, kv_dtype=kv_dtype),
        out_shape=(
            jax.ShapeDtypeStruct((b, s, D_FOURIER), BF16),
            jax.ShapeDtypeStruct((b, N_HEADS, s, V_DIM), BF16),
            jax.ShapeDtypeStruct((b, N_HEADS, s, V_DIM), kv_dtype),
            jax.ShapeDtypeStruct((b, N_HEADS, s, V_DIM), kv_dtype),
        ),
        grid=(b // nb,),
        in_specs=in_specs,
        out_specs=(
            pl.BlockSpec((nb, s, D_FOURIER), lambda i: (i, 0, 0)),
            head_spec, head_spec, head_spec,
        ),
        compiler_params=_params(1),
        name="inproj_rope" if use_rope else "inproj",
    )(*args)


def _scores(q, k):
    return lax.dot_general(q, k, (((1,), (1,)), ((), ())),
                           preferred_element_type=F32)


def _attn_kernel(lam_ref, subg_ref, q_ref, k_ref, v_ref, *rest, heads, has_ctx):
    if has_ctx:
        ck_ref, cv_ref, o_ref, kk_ref, vv_ref = rest
    else:
        o_ref, kk_ref, vv_ref = rest
    s_self = k_ref.shape[2]

    @pl.when(pl.program_id(2) == 0)
    def _():
        for hd in range(heads):
            kk_ref[hd, :s_self, :] = k_ref[0, hd].astype(BF16)
            vv_ref[hd, :s_self, :V_DIM] = v_ref[0, hd].astype(BF16)
            if has_ctx:
                kk_ref[hd, s_self:, :] = ck_ref[0, hd].astype(BF16)
                vv_ref[hd, s_self:, :V_DIM] = cv_ref[0, hd].astype(BF16)
            vv_ref[hd, :, V_DIM:] = jnp.ones((vv_ref.shape[1], V_DIM), BF16)

    lv = lam_ref[...]
    e1 = jnp.exp(jnp.sum(lv[0:1] * lv[1:2], axis=-1, keepdims=True))
    e2 = jnp.exp(jnp.sum(lv[2:3] * lv[3:4], axis=-1, keepdims=True))
    lam = e1 - e2 + LAM_INIT
    lane = lax.broadcasted_iota(jnp.int32, (1, V_DIM), 1)
    lo_mask = lane < HEAD_DIM
    subg = subg_ref[...]
    for hd in range(heads):
        q = q_ref[0, hd]
        zero = jnp.zeros_like(q)
        kk = kk_ref[hd]
        vv = vv_ref[hd]
        res = []
        for qc in (jnp.where(lo_mask, q, zero), jnp.where(lo_mask, zero, q)):
            s = _scores(qc, kk)
            p = jnp.exp2(s - s.max(axis=-1, keepdims=True)).astype(BF16)
            res.append(jnp.dot(p, vv, preferred_element_type=F32))
        r0, r1 = res
        o = r0[:, :V_DIM] / r0[:, V_DIM:] - lam * (r1[:, :V_DIM] / r1[:, V_DIM:])
        ms = jnp.mean(o * o, axis=-1, keepdims=True)
        o = o * lax.rsqrt(ms + EPS) * subg * (1.0 - LAM_INIT)
        o_ref[0, :, hd * V_DIM:(hd + 1) * V_DIM] = o.astype(o_ref.dtype)


def _attention(lamv, subg2, q, k, v, ctx, tq, heads):
    b, nh, s, _ = q.shape
    has_ctx = ctx is not None
    hg = nh // heads
    sk = s + (ctx[0].shape[2] if has_ctx else 0)
    qspec = pl.BlockSpec((1, heads, tq, V_DIM), lambda i, g, t: (i, g, t, 0))
    kvspec = pl.BlockSpec((1, heads, s, V_DIM), lambda i, g, t: (i, g, 0, 0))
    in_specs = [
        pl.BlockSpec((4, HEAD_DIM), lambda i, g, t: (0, 0)),
        pl.BlockSpec((1, V_DIM), lambda i, g, t: (0, 0)),
        qspec, kvspec, kvspec,
    ]
    args = [lamv, subg2, q, k, v]
    if has_ctx:
        p = ctx[0].shape[2]
        cspec = pl.BlockSpec((1, heads, p, V_DIM), lambda i, g, t: (i, g, 0, 0))
        in_specs += [cspec, cspec]
        args += list(ctx)
    return pl.pallas_call(
        functools.partial(_attn_kernel, heads=heads, has_ctx=has_ctx),
        out_shape=jax.ShapeDtypeStruct((b, s, nh * V_DIM), BF16),
        grid=(b, hg, s // tq),
        in_specs=in_specs,
        out_specs=pl.BlockSpec((1, tq, heads * V_DIM), lambda i, g, t: (i, t, g)),
        scratch_shapes=[
            pltpu.VMEM((heads, sk, V_DIM), BF16),
            pltpu.VMEM((heads, sk, 2 * V_DIM), BF16),
        ],
        compiler_params=_params(3),
        name="diff_attn_ctx" if has_ctx else "diff_attn",
    )(*args)


def _shifted_rows(gs_ref, slot, g, seq):
    rows, w = g.shape
    pitch = seq + SUBLANES
    for s in range(rows // seq):
        base = SUBLANES + s * pitch
        gs_ref[slot, base:base + seq, :w] = g[s * seq:(s + 1) * seq]
    shifted = []
    for off in (-1, 1):
        parts = [gs_ref[slot, SUBLANES + s * pitch + off:
                        SUBLANES + s * pitch + off + seq, :w]
                 for s in range(rows // seq)]
        shifted.append(parts[0] if len(parts) == 1 else jnp.concatenate(parts, axis=0))
    return shifted


def _mixffn_kernel(x_ref, yf_ref, o_ref, mod_ref, n2_ref, wo_ref, wg_ref, wu_ref,
                   cw_ref, cb_ref, wd_ref, y_ref, h2_ref, act_ref, gs_ref, *, seq):
    mod = mod_ref[0]
    rows = x_ref.shape[0]

    mix = (jnp.dot(yf_ref[...], wo_ref[:D_FOURIER, :].astype(BF16),
                   preferred_element_type=F32)
           + jnp.dot(o_ref[...], wo_ref[D_FOURIER:, :].astype(BF16),
                     preferred_element_type=F32))
    x1 = x_ref[...] + mod[2:3] * mix
    y_ref[...] = x1
    h2 = _rms_rows(x1, n2_ref[...]) * (1.0 + mod[4:5]) + mod[3:4]
    h2_ref[...] = h2.astype(BF16)
    for slot in range(gs_ref.shape[0]):
        for s in range(rows // seq + 1):
            r0 = s * (seq + SUBLANES)
            gs_ref[slot, r0:r0 + SUBLANES, :] = jnp.zeros(
                (SUBLANES, gs_ref.shape[2]), F32)

    h2 = h2_ref[...]
    cw = cw_ref[...]
    cb = cb_ref[...]
    for n, (c0, w) in enumerate(FF_SUBCHUNKS):
        if w == MXU_COLS:
            g = jnp.dot(h2, wg_ref[:, c0:c0 + w], preferred_element_type=F32)
            u = jnp.dot(h2, wu_ref[:, c0:c0 + w], preferred_element_type=F32)
        else:
            wgu = jnp.concatenate([wg_ref[:, c0:c0 + w], wu_ref[:, c0:c0 + w]], axis=1)
            gu = jnp.dot(h2, wgu, preferred_element_type=F32)
            g, u = gu[:, :w], gu[:, w:]
        g_prev, g_next = _shifted_rows(gs_ref, n % gs_ref.shape[0], g, seq)
        gc = (g * cw[1:2, c0:c0 + w] + g_prev * cw[0:1, c0:c0 + w]
              + g_next * cw[2:3, c0:c0 + w] + cb[:, c0:c0 + w])
        hg = 0.5 * gc
        act_ref[:, c0:c0 + w] = ((hg * u) * (1.0 + jnp.tanh(hg))).astype(BF16)
    y_ref[...] += mod[5:6] * jnp.dot(act_ref[...], wd_ref[...],
                                     preferred_element_type=F32)


def _mixffn(x2, yf2, o2, mod, mod_row, n2, w_out, w_gate, w_up, conv_w, conv_b2,
            w_down, seq):
    m, d = x2.shape
    tm = FFN_ROWS
    row = lambda i: (i, 0)
    const = lambda shape: pl.BlockSpec(shape, lambda i: (0, 0))
    stage_rows = SUBLANES + (tm // seq) * (seq + SUBLANES)
    return pl.pallas_call(
        functools.partial(_mixffn_kernel, seq=seq),
        out_shape=jax.ShapeDtypeStruct((m, d), F32),
        grid=(m // tm,),
        in_specs=[
            pl.BlockSpec((tm, d), row),
            pl.BlockSpec((tm, D_FOURIER), row),
            pl.BlockSpec((tm, D_ATTN), row),
            pl.BlockSpec((1, N_MOD, d), lambda i: (mod_row(i), 0, 0)),
            const((1, d)), const((d, d)), const((d, D_FF)), const((d, D_FF)),
            const((3, D_FF)), const((1, D_FF)), const((D_FF, d)),
        ],
        out_specs=pl.BlockSpec((tm, d), row),
        scratch_shapes=[
            pltpu.VMEM((tm, d), BF16),
            pltpu.VMEM((tm, D_FF), BF16),
            pltpu.VMEM((2, stage_rows, MXU_COLS), F32),
        ],
        compiler_params=_params(1),
        name="mix_ffn",
    )(x2, yf2, o2, mod, n2, w_out, w_gate, w_up, conv_w, conv_b2, w_down)


def _rope_tables(n_tokens):
    rows = n_tokens // GRID_W
    row = np.repeat(np.arange(rows, dtype=np.float64), GRID_W)
    col = np.tile(np.arange(GRID_W, dtype=np.float64), rows)
    n_freq = HEAD_DIM // 4
    inv = ROPE_THETA ** (-np.arange(n_freq, dtype=np.float64) / n_freq)
    ang_r = row[:, None] * inv
    ang_c = col[:, None] * inv
    zero = np.zeros_like(ang_r)
    cos64 = np.concatenate([np.cos(ang_r)] * 2 + [np.cos(ang_c)] * 2, axis=-1)
    sdn64 = np.concatenate([-np.sin(ang_r), zero, -np.sin(ang_c), zero], axis=-1)
    sup64 = np.concatenate([zero, np.sin(ang_r), zero, np.sin(ang_c)], axis=-1)
    two = lambda t: jnp.asarray(np.concatenate([t, t], axis=-1).astype(np.float32))
    return two(cos64), two(sdn64), two(sup64)


def _layer(x, mod, mod_row_tile, wts, lamv, rope, ctx, tq, heads, kv_dtype):
    (n1, n2, w_in, qg4, kg4, subg2, w_out, w_gate, w_up, conv_w, conv_b2, w_down) = wts
    b, s, d = x.shape
    yf, q, k, v = _inproj(x, mod, mod_row_tile, n1, w_in, qg4, kg4, rope,
                          FFN_ROWS // s, kv_dtype)
    o = _attention(lamv, subg2, q, k, v, ctx, tq, heads)
    y = _mixffn(x.reshape(b * s, d), yf.reshape(b * s, D_FOURIER),
                o.reshape(b * s, D_ATTN), mod, mod_row_tile, n2, w_out, w_gate,
                w_up, conv_w, conv_b2, w_down, s)
    return y.reshape(b, s, d), k, v


def kernel(x_prompt, x_sample, cache_k, cache_v, c, c_ctx, norm1_g, norm2_g, w_ada,
           b_ada, w_in, q_norm_g, k_norm_g, lam_q1, lam_k1, lam_q2, lam_k2, subln_g,
           w_out, w_gate, w_up, conv_w, conv_b, w_down):
    bp, sp, d = x_prompt.shape
    bs, ss, _ = x_sample.shape
    assert FFN_ROWS % sp == 0 and FFN_ROWS == ss

    cond = jnp.zeros((MOD_ROWS, d), F32).at[0].set(c_ctx).at[1:1 + bs].set(c)
    mod = _ada(cond, w_ada[0], b_ada[0]).reshape(MOD_ROWS, N_MOD, d)

    tile4 = lambda t: jnp.tile(t, MXU_COLS // HEAD_DIM)[None, :]
    wts = (norm1_g[0][None, :], norm2_g[0][None, :], w_in[0],
           tile4(q_norm_g[0]), tile4(k_norm_g[0]), subln_g[0][None, :],
           w_out[0], w_gate[0].astype(BF16), w_up[0].astype(BF16),
           conv_w[0], conv_b[0][None, :], w_down[0].astype(BF16))
    lamv = jnp.stack([lam_q1[0], lam_k1[0], lam_q2[0], lam_k2[0]])

    y_p, k_p, v_p = _layer(
        x_prompt, mod, lambda i: 0, wts, lamv, None, None,
        tq=sp, heads=N_HEADS, kv_dtype=F32)

    p = cache_k.shape[3]
    ctx = (cache_k[:, 0].reshape(bs, N_HEADS, p, V_DIM), cache_v[:, 0])
    y_s, _, _ = _layer(
        x_sample, mod, lambda i: i + 1, wts, lamv,
        _rope_tables(ss), ctx, tq=512, heads=6, kv_dtype=BF16)

    new_k = k_p.reshape(bp, 1, N_HEADS, sp, 2, HEAD_DIM)
    new_v = v_p.reshape(bp, 1, N_HEADS, sp, V_DIM)
    return (y_p, y_s, new_k, new_v)
```

```python
import functools
import math

import jax
import jax.numpy as jnp
import numpy as np
from jax import lax
from jax.experimental import pallas as pl
from jax.experimental.pallas import tpu as pltpu

F32 = jnp.float32
BF16 = jnp.bfloat16

D_MODEL = 1024
GRID_W = 64
N_HEADS = 6
HEAD_DIM = 64
V_DIM = 2 * HEAD_DIM
D_FOURIER = 256
FOURIER_HEAD_DIM = 64
D_QK = N_HEADS * V_DIM
D_ATTN = N_HEADS * V_DIM
D_IN = D_FOURIER + 2 * D_QK + D_ATTN
D_FF = 2816
ROPE_THETA = 10000.0
EPS = 1e-6
LAM_INIT = 0.8 - 0.6 * math.exp(-0.3 * 0)
Q_SCALE = HEAD_DIM ** -0.5 * math.log2(math.e)
N_MOD = 6
MOD_ROWS = 16

SUBLANES = 8
MXU_COLS = 256
FF_SUBCHUNKS = tuple((c, min(MXU_COLS, D_FF - c)) for c in range(0, D_FF, MXU_COLS))
FFN_ROWS = 1024
ATTN_TQ = 512
ATTN_PROMPT_REQS = 4
VMEM_LIMIT = 58 * 1024 * 1024


def _params(n_axes):
    return pltpu.CompilerParams(
        dimension_semantics=("arbitrary",) * n_axes,
        vmem_limit_bytes=VMEM_LIMIT,
    )


def _ada_kernel(cond_ref, w_ref, b_ref, o_ref):
    c = cond_ref[...]
    s = c * jax.nn.sigmoid(c)
    o_ref[...] = jnp.dot(s.astype(BF16), w_ref[...].astype(BF16),
                         preferred_element_type=F32) + b_ref[...]


def _ada(cond, w_ada, b_ada):
    n = w_ada.shape[1]
    tn = 1536
    return pl.pallas_call(
        _ada_kernel,
        out_shape=jax.ShapeDtypeStruct((MOD_ROWS, n), F32),
        grid=(n // tn,),
        in_specs=[
            pl.BlockSpec((MOD_ROWS, D_MODEL), lambda j: (0, 0)),
            pl.BlockSpec((D_MODEL, tn), lambda j: (0, j)),
            pl.BlockSpec((1, tn), lambda j: (0, j)),
        ],
        out_specs=pl.BlockSpec((MOD_ROWS, tn), lambda j: (0, j)),
        compiler_params=_params(1),
        name="ada_mod",
    )(cond, w_ada, b_ada.reshape(1, n))


def _rms_rows(x, gain):
    ms = jnp.mean(x * x, axis=-1, keepdims=True)
    return x * lax.rsqrt(ms + EPS) * gain


def _group_rmsnorm(x, gain, ones_blk):
    ss = jnp.dot((x * x).astype(BF16), ones_blk, preferred_element_type=F32)
    return x * lax.rsqrt(ss * (1.0 / HEAD_DIM) + EPS) * gain


def _rope(x, cos, sin_dn, sin_up):
    return (x * cos + pltpu.roll(x, V_DIM - 16, 1) * sin_dn
            + pltpu.roll(x, 16, 1) * sin_up)


def _inproj_kernel(x_ref, mod_ref, n1_ref, w_ref, qg_ref, kg_ref, ones_ref, cc_ref,
                   sc_ref, cs_ref, ss_ref, *rest, use_rope, kv_dtype):
    if use_rope:
        cos_ref, sdn_ref, sup_ref, yf_ref, q_ref, k_ref, v_ref = rest
    else:
        yf_ref, q_ref, k_ref, v_ref = rest
    nb, seq, d = x_ref.shape
    x = x_ref[...].reshape(nb * seq, d)
    mod = mod_ref[0]
    h = (_rms_rows(x, n1_ref[...]) * (1.0 + mod[1:2]) + mod[0:1]).astype(BF16)

    def project(c0):
        return jnp.dot(h, w_ref[:, c0:c0 + MXU_COLS].astype(BF16),
                       preferred_element_type=F32)

    xf = project(0).astype(BF16)
    xc = jnp.dot(xf, cc_ref[...], preferred_element_type=F32).astype(BF16)
    xs = jnp.dot(xf, sc_ref[...], preferred_element_type=F32).astype(BF16)
    for b in range(nb):
        r = slice(b * seq, (b + 1) * seq)
        yf = (jnp.dot(cs_ref[...], xc[r], preferred_element_type=F32)
              - jnp.dot(ss_ref[...], xs[r], preferred_element_type=F32))
        yf_ref[b] = yf.astype(yf_ref.dtype)

    def store_heads(out_ref, head0, y, dtype):
        for hh in range(y.shape[1] // V_DIM):
            for b in range(nb):
                out_ref[b, head0 + hh] = y[b * seq:(b + 1) * seq,
                                           hh * V_DIM:(hh + 1) * V_DIM].astype(dtype)

    ones_blk = ones_ref[...]
    for pair in range(N_HEADS // 2):
        c0 = D_FOURIER + pair * MXU_COLS
        zq = project(c0)
        zk = project(c0 + D_QK)
        zv = project(c0 + 2 * D_QK)
        q = _group_rmsnorm(zq, qg_ref[...], ones_blk)
        k = _group_rmsnorm(zk, kg_ref[...], ones_blk)
        for hh in range(2):
            ls = slice(hh * V_DIM, (hh + 1) * V_DIM)
            qh, kh = q[:, ls], k[:, ls]
            if use_rope:
                qh = _rope(qh, cos_ref[...], sdn_ref[...], sup_ref[...])
                kh = _rope(kh, cos_ref[...], sdn_ref[...], sup_ref[...])
            store_heads(q_ref, 2 * pair + hh, qh * Q_SCALE, BF16)
            store_heads(k_ref, 2 * pair + hh, kh, kv_dtype)
        store_heads(v_ref, 2 * pair, zv, kv_dtype)


def _dft_tables(n, scale):
    idx = np.arange(n, dtype=np.int64)
    ang = ((idx[:, None] * idx[None, :]) % n).astype(np.float64) * (2.0 * math.pi / n)
    return ((np.cos(ang) * scale).astype(np.float32),
            (np.sin(ang) * scale).astype(np.float32))


def _fourier_tables(s):
    c64, s64 = _dft_tables(FOURIER_HEAD_DIM, (s * FOURIER_HEAD_DIM) ** -0.5)
    eye = np.eye(D_FOURIER // FOURIER_HEAD_DIM, dtype=np.float32)
    tabs = (np.kron(eye, c64), np.kron(eye, s64)) + _dft_tables(s, 1.0)
    return tuple(jnp.asarray(t).astype(BF16) for t in tabs)


def _inproj(x, mod, mod_row, n1, w_in, qg4, kg4, rope, nb, kv_dtype):
    b, s, d = x.shape
    use_rope = rope is not None
    ones_blk = jnp.asarray(np.kron(np.eye(MXU_COLS // HEAD_DIM, dtype=np.float32),
                                   np.ones((HEAD_DIM, HEAD_DIM), np.float32))).astype(BF16)
    const = lambda shape: pl.BlockSpec(shape, lambda i: (0, 0))
    in_specs = [
        pl.BlockSpec((nb, s, d), lambda i: (i, 0, 0)),
        pl.BlockSpec((1, N_MOD, d), lambda i: (mod_row(i), 0, 0)),
        const((1, d)), const((d, D_IN)), const((1, MXU_COLS)), const((1, MXU_COLS)),
        const((MXU_COLS, MXU_COLS)), const((D_FOURIER, D_FOURIER)),
        const((D_FOURIER, D_FOURIER)), const((s, s)), const((s, s)),
    ]
    args = [x, mod, n1, w_in, qg4, kg4, ones_blk, *_fourier_tables(s)]
    if use_rope:
        assert nb == 1
        in_specs += [const((s, V_DIM))] * 3
        args += list(rope)
    head_spec = pl.BlockSpec((nb, N_HEADS, s, V_DIM), lambda i: (i, 0, 0, 0))
    return pl.pallas_call(
        functools.partial(_inproj_kernel, use_rope=use_rope, kv_dtype=kv_dtype),
        out_shape=(
            jax.ShapeDtypeStruct((b, s, D_FOURIER), BF16),
            jax.ShapeDtypeStruct((b, N_HEADS, s, V_DIM), BF16),
            jax.ShapeDtypeStruct((b, N_HEADS, s, V_DIM), kv_dtype),
            jax.ShapeDtypeStruct((b, N_HEADS, s, V_DIM), kv_dtype),
        ),
        grid=(b // nb,),
        in_specs=in_specs,
        out_specs=(
            pl.BlockSpec((nb, s, D_FOURIER), lambda i: (i, 0, 0)),
            head_spec, head_spec, head_spec,
        ),
        compiler_params=_params(1),
        name="inproj_rope" if use_rope else "inproj",
    )(*args)


def _scores(q, k):
    return lax.dot_general(q, k, (((1,), (1,)), ((), ())),
                           preferred_element_type=F32)


def _attn_kernel(lam_ref, subg_ref, q_ref, k_ref, v_ref, *rest, has_ctx):
    n_cast = (len(rest) - (5 if has_ctx else 3)) // 2
    if has_ctx:
        ck_ref, cv_ref = rest[:2]
        rest = rest[2:]
    cast_in = rest[:n_cast]
    o_ref = rest[n_cast]
    cast_out = rest[n_cast + 1:2 * n_cast + 1]
    kk_ref, vv_ref = rest[2 * n_cast + 1:]
    nb, heads, s_self, _ = k_ref.shape

    for src, dst in zip(cast_in, cast_out):
        dst[...] = src[...].astype(BF16)

    @pl.when(pl.program_id(1) == 0)
    def _():
        for b in range(nb):
            for hd in range(heads):
                kk_ref[b, hd, :s_self, :] = k_ref[b, hd].astype(BF16)
                vv_ref[b, hd, :s_self, :V_DIM] = v_ref[b, hd].astype(BF16)
                if has_ctx:
                    kk_ref[b, hd, s_self:, :] = ck_ref[b, hd].astype(BF16)
                    vv_ref[b, hd, s_self:, :V_DIM] = cv_ref[b, hd].astype(BF16)
                vv_ref[b, hd, :, V_DIM:] = jnp.ones((vv_ref.shape[2], V_DIM), BF16)

    lv = lam_ref[...]
    e1 = jnp.exp(jnp.sum(lv[0:1] * lv[1:2], axis=-1, keepdims=True))
    e2 = jnp.exp(jnp.sum(lv[2:3] * lv[3:4], axis=-1, keepdims=True))
    lam = e1 - e2 + LAM_INIT
    lane = lax.broadcasted_iota(jnp.int32, (1, V_DIM), 1)
    lo_mask = lane < HEAD_DIM
    subg = subg_ref[...]
    for b in range(nb):
        for hd in range(heads):
            q = q_ref[b, hd]
            zero = jnp.zeros_like(q)
            kk = kk_ref[b, hd]
            vv = vv_ref[b, hd]
            ss = [_scores(qc, kk) for qc in (jnp.where(lo_mask, q, zero),
                                             jnp.where(lo_mask, zero, q))]
            ps = [jnp.exp2(s - s.max(axis=-1, keepdims=True)).astype(BF16) for s in ss]
            r0, r1 = [jnp.dot(p, vv, preferred_element_type=F32) for p in ps]
            o = r0[:, :V_DIM] / r0[:, V_DIM:] - lam * (r1[:, :V_DIM] / r1[:, V_DIM:])
            ms = jnp.mean(o * o, axis=-1, keepdims=True)
            o = o * lax.rsqrt(ms + EPS) * subg * (1.0 - LAM_INIT)
            o_ref[b, :, hd * V_DIM:(hd + 1) * V_DIM] = o.astype(o_ref.dtype)


def _attention(lamv, subg2, q, k, v, ctx, nb, tq, cast=()):
    b, nh, s, _ = q.shape
    has_ctx = ctx is not None
    sk = s + (ctx[0].shape[2] if has_ctx else 0)
    nt = s // tq
    steps = (b // nb) * nt
    qspec = pl.BlockSpec((nb, nh, tq, V_DIM), lambda i, t: (i, 0, t, 0))
    kvspec = pl.BlockSpec((nb, nh, s, V_DIM), lambda i, t: (i, 0, 0, 0))
    in_specs = [
        pl.BlockSpec((4, HEAD_DIM), lambda i, t: (0, 0)),
        pl.BlockSpec((1, V_DIM), lambda i, t: (0, 0)),
        qspec, kvspec, kvspec,
    ]
    args = [lamv, subg2, q, k, v]
    if has_ctx:
        p = ctx[0].shape[2]
        cspec = pl.BlockSpec((nb, nh, p, V_DIM), lambda i, t: (i, 0, 0, 0))
        in_specs += [cspec, cspec]
        args += list(ctx)
    cast_specs = [pl.BlockSpec((w.shape[0] // steps, w.shape[1]),
                               lambda i, t: (i * nt + t, 0)) for w in cast]
    outs = pl.pallas_call(
        functools.partial(_attn_kernel, has_ctx=has_ctx),
        out_shape=(jax.ShapeDtypeStruct((b, s, nh * V_DIM), BF16),
                   *(jax.ShapeDtypeStruct(w.shape, BF16) for w in cast)),
        grid=(b // nb, nt),
        in_specs=in_specs + cast_specs,
        out_specs=(pl.BlockSpec((nb, tq, nh * V_DIM), lambda i, t: (i, t, 0)),
                   *cast_specs),
        scratch_shapes=[
            pltpu.VMEM((nb, nh, sk, V_DIM), BF16),
            pltpu.VMEM((nb, nh, sk, 2 * V_DIM), BF16),
        ],
        compiler_params=_params(2),
        name="diff_attn_ctx" if has_ctx else "diff_attn",
    )(*args, *cast)
    return outs


def _shifted_rows(gs_ref, slot, g, seq):
    rows, w = g.shape
    pitch = seq + SUBLANES
    for s in range(rows // seq):
        base = SUBLANES + s * pitch
        gs_ref[slot, base:base + seq, :w] = g[s * seq:(s + 1) * seq]
    shifted = []
    for off in (-1, 1):
        parts = [gs_ref[slot, SUBLANES + s * pitch + off:
                        SUBLANES + s * pitch + off + seq, :w]
                 for s in range(rows // seq)]
        shifted.append(parts[0] if len(parts) == 1 else jnp.concatenate(parts, axis=0))
    return shifted


def _mixffn_kernel(x_ref, yf_ref, o_ref, mod_ref, n2_ref, wo_ref, wg_ref, wu_ref,
                   cw_ref, cb_ref, wd_ref, y_ref, h2_ref, act_ref, gs_ref, *, seq):
    mod = mod_ref[0]
    rows = x_ref.shape[0]

    mix = (jnp.dot(yf_ref[...], wo_ref[:D_FOURIER, :].astype(BF16),
                   preferred_element_type=F32)
           + jnp.dot(o_ref[...], wo_ref[D_FOURIER:, :].astype(BF16),
                     preferred_element_type=F32))
    x1 = x_ref[...] + mod[2:3] * mix
    y_ref[...] = x1
    h2 = _rms_rows(x1, n2_ref[...]) * (1.0 + mod[4:5]) + mod[3:4]
    h2_ref[...] = h2.astype(BF16)
    for slot in range(gs_ref.shape[0]):
        for s in range(rows // seq + 1):
            r0 = s * (seq + SUBLANES)
            gs_ref[slot, r0:r0 + SUBLANES, :] = jnp.zeros(
                (SUBLANES, gs_ref.shape[2]), F32)

    h2 = h2_ref[...]
    cw = cw_ref[...]
    cb = cb_ref[...]
    for n, (c0, w) in enumerate(FF_SUBCHUNKS):
        if w == MXU_COLS:
            g = jnp.dot(h2, wg_ref[:, c0:c0 + w], preferred_element_type=F32)
            u = jnp.dot(h2, wu_ref[:, c0:c0 + w], preferred_element_type=F32)
        else:
            wgu = jnp.concatenate([wg_ref[:, c0:c0 + w], wu_ref[:, c0:c0 + w]], axis=1)
            gu = jnp.dot(h2, wgu, preferred_element_type=F32)
            g, u = gu[:, :w], gu[:, w:]
        g_prev, g_next = _shifted_rows(gs_ref, n % gs_ref.shape[0], g, seq)
        gc = (g * cw[1:2, c0:c0 + w] + g_prev * cw[0:1, c0:c0 + w]
              + g_next * cw[2:3, c0:c0 + w] + cb[:, c0:c0 + w])
        hg = 0.5 * gc
        act_ref[:, c0:c0 + w] = ((hg * u) * (1.0 + jnp.tanh(hg))).astype(BF16)
    y_ref[...] += mod[5:6] * jnp.dot(act_ref[...], wd_ref[...],
                                     preferred_element_type=F32)


def _mixffn(x2, yf2, o2, mod, mod_row, n2, w_out, w_gate, w_up, conv_w, conv_b2,
            w_down, seq):
    m, d = x2.shape
    tm = FFN_ROWS
    row = lambda i: (i, 0)
    const = lambda shape: pl.BlockSpec(shape, lambda i: (0, 0))
    stage_rows = SUBLANES + (tm // seq) * (seq + SUBLANES)
    return pl.pallas_call(
        functools.partial(_mixffn_kernel, seq=seq),
        out_shape=jax.ShapeDtypeStruct((m, d), F32),
        grid=(m // tm,),
        in_specs=[
            pl.BlockSpec((tm, d), row),
            pl.BlockSpec((tm, D_FOURIER), row),
            pl.BlockSpec((tm, D_ATTN), row),
            pl.BlockSpec((1, N_MOD, d), lambda i: (mod_row(i), 0, 0)),
            const((1, d)), const((d, d)), const((d, D_FF)), const((d, D_FF)),
            const((3, D_FF)), const((1, D_FF)), const((D_FF, d)),
        ],
        out_specs=pl.BlockSpec((tm, d), row),
        scratch_shapes=[
            pltpu.VMEM((tm, d), BF16),
            pltpu.VMEM((tm, D_FF), BF16),
            pltpu.VMEM((2, stage_rows, MXU_COLS), F32),
        ],
        compiler_params=_params(1),
        name="mix_ffn",
    )(x2, yf2, o2, mod, n2, w_out, w_gate, w_up, conv_w, conv_b2, w_down)


def _rope_tables(n_tokens):
    rows = n_tokens // GRID_W
    row = np.repeat(np.arange(rows, dtype=np.float64), GRID_W)
    col = np.tile(np.arange(GRID_W, dtype=np.float64), rows)
    n_freq = HEAD_DIM // 4
    inv = ROPE_THETA ** (-np.arange(n_freq, dtype=np.float64) / n_freq)
    ang_r = row[:, None] * inv
    ang_c = col[:, None] * inv
    zero = np.zeros_like(ang_r)
    cos64 = np.concatenate([np.cos(ang_r)] * 2 + [np.cos(ang_c)] * 2, axis=-1)
    sdn64 = np.concatenate([-np.sin(ang_r), zero, -np.sin(ang_c), zero], axis=-1)
    sup64 = np.concatenate([zero, np.sin(ang_r), zero, np.sin(ang_c)], axis=-1)
    two = lambda t: jnp.asarray(np.concatenate([t, t], axis=-1).astype(np.float32))
    return two(cos64), two(sdn64), two(sup64)


def kernel(x_prompt, x_sample, cache_k, cache_v, c, c_ctx, norm1_g, norm2_g, w_ada,
           b_ada, w_in, q_norm_g, k_norm_g, lam_q1, lam_k1, lam_q2, lam_k2, subln_g,
           w_out, w_gate, w_up, conv_w, conv_b, w_down):
    bp, sp, d = x_prompt.shape
    bs, ss, _ = x_sample.shape
    assert FFN_ROWS % sp == 0 and FFN_ROWS == ss

    cond = jnp.zeros((MOD_ROWS, d), F32).at[0].set(c_ctx).at[1:1 + bs].set(c)
    mod = _ada(cond, w_ada[0], b_ada[0]).reshape(MOD_ROWS, N_MOD, d)

    tile4 = lambda t: jnp.tile(t, MXU_COLS // HEAD_DIM)[None, :]
    n1, n2 = norm1_g[0][None, :], norm2_g[0][None, :]
    qg4, kg4, subg2 = tile4(q_norm_g[0]), tile4(k_norm_g[0]), subln_g[0][None, :]
    lamv = jnp.stack([lam_q1[0], lam_k1[0], lam_q2[0], lam_k2[0]])
    row_p = lambda i: 0
    row_s = lambda i: i + 1
    p = cache_k.shape[3]
    ctx = (cache_k[:, 0].reshape(bs, N_HEADS, p, V_DIM), cache_v[:, 0])

    yf_p, q_p, k_p, v_p = _inproj(x_prompt, mod, row_p, n1, w_in[0], qg4, kg4, None,
                                  FFN_ROWS // sp, F32)
    yf_s, q_s, k_s, v_s = _inproj(x_sample, mod, row_s, n1, w_in[0], qg4, kg4,
                                  _rope_tables(ss), FFN_ROWS // ss, BF16)
    (o_p,) = _attention(lamv, subg2, q_p, k_p, v_p, None, ATTN_PROMPT_REQS, sp)
    o_s, wg_b, wu_b, wd_b = _attention(lamv, subg2, q_s, k_s, v_s, ctx, 1, ATTN_TQ,
                                       cast=(w_gate[0], w_up[0], w_down[0]))

    def mixer(x, yf, o, mod_row):
        b, s, _ = x.shape
        y = _mixffn(x.reshape(b * s, d), yf.reshape(b * s, D_FOURIER),
                    o.reshape(b * s, D_ATTN), mod, mod_row, n2, w_out[0], wg_b, wu_b,
                    conv_w[0], conv_b[0][None, :], wd_b, s)
        return y.reshape(b, s, d)

    y_p = mixer(x_prompt, yf_p, o_p, row_p)
    y_s = mixer(x_sample, yf_s, o_s, row_s)

    new_k = k_p.reshape(bp, 1, N_HEADS, sp, 2, HEAD_DIM)
    new_v = v_p.reshape(bp, 1, N_HEADS, sp, V_DIM)
    return (y_p, y_s, new_k, new_v)
```

```python
import functools
import math

import jax
import jax.numpy as jnp
import numpy as np
from jax import lax
from jax.experimental import pallas as pl
from jax.experimental.pallas import tpu as pltpu

F32 = jnp.float32
BF16 = jnp.bfloat16

D_MODEL = 1024
GRID_W = 64
N_HEADS = 6
HEAD_DIM = 64
V_DIM = 2 * HEAD_DIM
D_FOURIER = 256
FOURIER_HEAD_DIM = 64
D_QK = N_HEADS * V_DIM
D_ATTN = N_HEADS * V_DIM
D_IN = D_FOURIER + 2 * D_QK + D_ATTN
D_FF = 2816
ROPE_THETA = 10000.0
EPS = 1e-6
LAM_INIT = 0.8 - 0.6 * math.exp(-0.3 * 0)
Q_SCALE = HEAD_DIM ** -0.5 * math.log2(math.e)
N_MOD = 6
MOD_ROWS = 16

SUBLANES = 8
MXU_COLS = 256
FF_SUBCHUNKS = tuple((c, min(MXU_COLS, D_FF - c)) for c in range(0, D_FF, MXU_COLS))
FFN_ROWS = 1024
ATTN_TQ = 512
ATTN_PROMPT_REQS = 1
VMEM_LIMIT = 58 * 1024 * 1024


def _params(n_axes):
    return pltpu.CompilerParams(
        dimension_semantics=("arbitrary",) * n_axes,
        vmem_limit_bytes=VMEM_LIMIT,
    )


def _ada_kernel(cond_ref, w_ref, b_ref, o_ref):
    c = cond_ref[...]
    s = c * jax.nn.sigmoid(c)
    o_ref[...] = jnp.dot(s.astype(BF16), w_ref[...].astype(BF16),
                         preferred_element_type=F32) + b_ref[...]


def _ada(cond, w_ada, b_ada):
    n = w_ada.shape[1]
    tn = 1536
    return pl.pallas_call(
        _ada_kernel,
        out_shape=jax.ShapeDtypeStruct((MOD_ROWS, n), F32),
        grid=(n // tn,),
        in_specs=[
            pl.BlockSpec((MOD_ROWS, D_MODEL), lambda j: (0, 0)),
            pl.BlockSpec((D_MODEL, tn), lambda j: (0, j)),
            pl.BlockSpec((1, tn), lambda j: (0, j)),
        ],
        out_specs=pl.BlockSpec((MOD_ROWS, tn), lambda j: (0, j)),
        compiler_params=_params(1),
        name="ada_mod",
    )(cond, w_ada, b_ada.reshape(1, n))


def _rms_rows(x, gain):
    ms = jnp.mean(x * x, axis=-1, keepdims=True)
    return x * lax.rsqrt(ms + EPS) * gain


def _group_rmsnorm(x, gain, ones_blk):
    ss = jnp.dot((x * x).astype(BF16), ones_blk, preferred_element_type=F32)
    return x * lax.rsqrt(ss * (1.0 / HEAD_DIM) + EPS) * gain


def _rope(x, cos, sin_dn, sin_up):
    return (x * cos + pltpu.roll(x, V_DIM - 16, 1) * sin_dn
            + pltpu.roll(x, 16, 1) * sin_up)


def _inproj_kernel(x_ref, mod_ref, n1_ref, w_ref, qg_ref, kg_ref, ones_ref, cc_ref,
                   sc_ref, cs_ref, ss_ref, *rest, use_rope, kv_dtype):
    if use_rope:
        cos_ref, sdn_ref, sup_ref, yf_ref, q_ref, k_ref, v_ref = rest
    else:
        yf_ref, q_ref, k_ref, v_ref = rest
    nb, seq, d = x_ref.shape
    x = x_ref[...].reshape(nb * seq, d)
    mod = mod_ref[0]
    h = (_rms_rows(x, n1_ref[...]) * (1.0 + mod[1:2]) + mod[0:1]).astype(BF16)

    def project(c0):
        return jnp.dot(h, w_ref[:, c0:c0 + MXU_COLS].astype(BF16),
                       preferred_element_type=F32)

    xf = project(0).astype(BF16)
    xc = jnp.dot(xf, cc_ref[...], preferred_element_type=F32).astype(BF16)
    xs = jnp.dot(xf, sc_ref[...], preferred_element_type=F32).astype(BF16)
    for b in range(nb):
        r = slice(b * seq, (b + 1) * seq)
        yf = (jnp.dot(cs_ref[...], xc[r], preferred_element_type=F32)
              - jnp.dot(ss_ref[...], xs[r], preferred_element_type=F32))
        yf_ref[b] = yf.astype(yf_ref.dtype)

    def store_heads(out_ref, head0, y, dtype):
        for hh in range(y.shape[1] // V_DIM):
            for b in range(nb):
                out_ref[b, head0 + hh] = y[b * seq:(b + 1) * seq,
                                           hh * V_DIM:(hh + 1) * V_DIM].astype(dtype)

    ones_blk = ones_ref[...]
    for pair in range(N_HEADS // 2):
        c0 = D_FOURIER + pair * MXU_COLS
        zq = project(c0)
        zk = project(c0 + D_QK)
        zv = project(c0 + 2 * D_QK)
        q = _group_rmsnorm(zq, qg_ref[...], ones_blk)
        k = _group_rmsnorm(zk, kg_ref[...], ones_blk)
        for hh in range(2):
            ls = slice(hh * V_DIM, (hh + 1) * V_DIM)
            qh, kh = q[:, ls], k[:, ls]
            if use_rope:
                qh = _rope(qh, cos_ref[...], sdn_ref[...], sup_ref[...])
                kh = _rope(kh, cos_ref[...], sdn_ref[...], sup_ref[...])
            store_heads(q_ref, 2 * pair + hh, qh * Q_SCALE, BF16)
            store_heads(k_ref, 2 * pair + hh, kh, kv_dtype)
        store_heads(v_ref, 2 * pair, zv, kv_dtype)


def _dft_tables(n, scale):
    idx = np.arange(n, dtype=np.int64)
    ang = ((idx[:, None] * idx[None, :]) % n).astype(np.float64) * (2.0 * math.pi / n)
    return ((np.cos(ang) * scale).astype(np.float32),
            (np.sin(ang) * scale).astype(np.float32))


def _fourier_tables(s):
    c64, s64 = _dft_tables(FOURIER_HEAD_DIM, (s * FOURIER_HEAD_DIM) ** -0.5)
    eye = np.eye(D_FOURIER // FOURIER_HEAD_DIM, dtype=np.float32)
    tabs = (np.kron(eye, c64), np.kron(eye, s64)) + _dft_tables(s, 1.0)
    return tuple(jnp.asarray(t).astype(BF16) for t in tabs)


def _inproj(x, mod, mod_row, n1, w_in, qg4, kg4, rope, nb, kv_dtype):
    b, s, d = x.shape
    use_rope = rope is not None
    ones_blk = jnp.asarray(np.kron(np.eye(MXU_COLS // HEAD_DIM, dtype=np.float32),
                                   np.ones((HEAD_DIM, HEAD_DIM), np.float32))).astype(BF16)
    const = lambda shape: pl.BlockSpec(shape, lambda i: (0, 0))
    in_specs = [
        pl.BlockSpec((nb, s, d), lambda i: (i, 0, 0)),
        pl.BlockSpec((1, N_MOD, d), lambda i: (mod_row(i), 0, 0)),
        const((1, d)), const((d, D_IN)), const((1, MXU_COLS)), const((1, MXU_COLS)),
        const((MXU_COLS, MXU_COLS)), const((D_FOURIER, D_FOURIER)),
        const((D_FOURIER, D_FOURIER)), const((s, s)), const((s, s)),
    ]
    args = [x, mod, n1, w_in, qg4, kg4, ones_blk, *_fourier_tables(s)]
    if use_rope:
        assert nb == 1
        in_specs += [const((s, V_DIM))] * 3
        args += list(rope)
    head_spec = pl.BlockSpec((nb, N_HEADS, s, V_DIM), lambda i: (i, 0, 0, 0))
    return pl.pallas_call(
        functools.partial(_inproj_kernel, use_rope=use_rope, kv_dtype=kv_dtype),
        out_shape=(
            jax.ShapeDtypeStruct((b, s, D_FOURIER), BF16),
            jax.ShapeDtypeStruct((b, N_HEADS, s, V_DIM), BF16),
            jax.ShapeDtypeStruct((b, N_HEADS, s, V_DIM), kv_dtype),
            jax.ShapeDtypeStruct((b, N_HEADS, s, V_DIM), kv_dtype),
        ),
        grid=(b // nb,),
        in_specs=in_specs,
        out_specs=(
            pl.BlockSpec((nb, s, D_FOURIER), lambda i: (i, 0, 0)),
            head_spec, head_spec, head_spec,
        ),
        compiler_params=_params(1),
        name="inproj_rope" if use_rope else "inproj",
    )(*args)


def _scores(q, k):
    return lax.dot_general(q, k, (((1,), (1,)), ((), ())),
                           preferred_element_type=F32)


def _attn_kernel(lam_ref, subg_ref, q_ref, k_ref, v_ref, *rest, has_ctx):
    n_cast = (len(rest) - (5 if has_ctx else 3)) // 2
    if has_ctx:
        ck_ref, cv_ref = rest[:2]
        rest = rest[2:]
    cast_in = rest[:n_cast]
    o_ref = rest[n_cast]
    cast_out = rest[n_cast + 1:2 * n_cast + 1]
    kk_ref, vv_ref = rest[2 * n_cast + 1:]
    nb, heads, s_self, _ = k_ref.shape

    for src, dst in zip(cast_in, cast_out):
        dst[...] = src[...].astype(BF16)

    @pl.when(pl.program_id(1) == 0)
    def _():
        for b in range(nb):
            for hd in range(heads):
                kk_ref[b, hd, :s_self, :] = k_ref[b, hd].astype(BF16)
                vv_ref[b, hd, :s_self, :V_DIM] = v_ref[b, hd].astype(BF16)
                if has_ctx:
                    kk_ref[b, hd, s_self:, :] = ck_ref[b, hd].astype(BF16)
                    vv_ref[b, hd, s_self:, :V_DIM] = cv_ref[b, hd].astype(BF16)
                vv_ref[b, hd, :, V_DIM:] = jnp.ones((vv_ref.shape[2], V_DIM), BF16)

    lv = lam_ref[...]
    e1 = jnp.exp(jnp.sum(lv[0:1] * lv[1:2], axis=-1, keepdims=True))
    e2 = jnp.exp(jnp.sum(lv[2:3] * lv[3:4], axis=-1, keepdims=True))
    lam = e1 - e2 + LAM_INIT
    lane = lax.broadcasted_iota(jnp.int32, (1, V_DIM), 1)
    lo_mask = lane < HEAD_DIM
    subg = subg_ref[...]
    for b in range(nb):
        for hd in range(heads):
            q = q_ref[b, hd]
            zero = jnp.zeros_like(q)
            kk = kk_ref[b, hd]
            vv = vv_ref[b, hd]
            res = []
            for qc in (jnp.where(lo_mask, q, zero), jnp.where(lo_mask, zero, q)):
                s = _scores(qc, kk)
                p = jnp.exp2(s - s.max(axis=-1, keepdims=True)).astype(BF16)
                res.append(jnp.dot(p, vv, preferred_element_type=F32))
            r0, r1 = res
            o = r0[:, :V_DIM] / r0[:, V_DIM:] - lam * (r1[:, :V_DIM] / r1[:, V_DIM:])
            ms = jnp.mean(o * o, axis=-1, keepdims=True)
            o = o * lax.rsqrt(ms + EPS) * subg * (1.0 - LAM_INIT)
            o_ref[b, :, hd * V_DIM:(hd + 1) * V_DIM] = o.astype(o_ref.dtype)


def _attention(lamv, subg2, q, k, v, ctx, nb, tq, cast=()):
    b, nh, s, _ = q.shape
    has_ctx = ctx is not None
    sk = s + (ctx[0].shape[2] if has_ctx else 0)
    nt = s // tq
    steps = (b // nb) * nt
    qspec = pl.BlockSpec((nb, nh, tq, V_DIM), lambda i, t: (i, 0, t, 0))
    kvspec = pl.BlockSpec((nb, nh, s, V_DIM), lambda i, t: (i, 0, 0, 0))
    in_specs = [
        pl.BlockSpec((4, HEAD_DIM), lambda i, t: (0, 0)),
        pl.BlockSpec((1, V_DIM), lambda i, t: (0, 0)),
        qspec, kvspec, kvspec,
    ]
    args = [lamv, subg2, q, k, v]
    if has_ctx:
        p = ctx[0].shape[2]
        cspec = pl.BlockSpec((nb, nh, p, V_DIM), lambda i, t: (i, 0, 0, 0))
        in_specs += [cspec, cspec]
        args += list(ctx)
    cast_specs = [pl.BlockSpec((w.shape[0] // steps, w.shape[1]),
                               lambda i, t: (i * nt + t, 0)) for w in cast]
    outs = pl.pallas_call(
        functools.partial(_attn_kernel, has_ctx=has_ctx),
        out_shape=(jax.ShapeDtypeStruct((b, s, nh * V_DIM), BF16),
                   *(jax.ShapeDtypeStruct(w.shape, BF16) for w in cast)),
        grid=(b // nb, nt),
        in_specs=in_specs + cast_specs,
        out_specs=(pl.BlockSpec((nb, tq, nh * V_DIM), lambda i, t: (i, t, 0)),
                   *cast_specs),
        scratch_shapes=[
            pltpu.VMEM((nb, nh, sk, V_DIM), BF16),
            pltpu.VMEM((nb, nh, sk, 2 * V_DIM), BF16),
        ],
        compiler_params=_params(2),
        name="diff_attn_ctx" if has_ctx else "diff_attn",
    )(*args, *cast)
    return outs


def _shifted_rows(gs_ref, slot, g, seq):
    rows, w = g.shape
    pitch = seq + SUBLANES
    for s in range(rows // seq):
        base = SUBLANES + s * pitch
        gs_ref[slot, base:base + seq, :w] = g[s * seq:(s + 1) * seq]
    shifted = []
    for off in (-1, 1):
        parts = [gs_ref[slot, SUBLANES + s * pitch + off:
                        SUBLANES + s * pitch + off + seq, :w]
                 for s in range(rows // seq)]
        shifted.append(parts[0] if len(parts) == 1 else jnp.concatenate(parts, axis=0))
    return shifted


def _mixffn_kernel(x_ref, yf_ref, o_ref, mod_ref, n2_ref, wo_ref, wg_ref, wu_ref,
                   cw_ref, cb_ref, wd_ref, y_ref, h2_ref, act_ref, gs_ref, *, seq):
    mod = mod_ref[0]
    rows = x_ref.shape[0]

    mix = (jnp.dot(yf_ref[...], wo_ref[:D_FOURIER, :].astype(BF16),
                   preferred_element_type=F32)
           + jnp.dot(o_ref[...], wo_ref[D_FOURIER:, :].astype(BF16),
                     preferred_element_type=F32))
    x1 = x_ref[...] + mod[2:3] * mix
    y_ref[...] = x1
    h2 = _rms_rows(x1, n2_ref[...]) * (1.0 + mod[4:5]) + mod[3:4]
    h2_ref[...] = h2.astype(BF16)
    for slot in range(gs_ref.shape[0]):
        for s in range(rows // seq + 1):
            r0 = s * (seq + SUBLANES)
            gs_ref[slot, r0:r0 + SUBLANES, :] = jnp.zeros(
                (SUBLANES, gs_ref.shape[2]), F32)

    h2 = h2_ref[...]
    cw = cw_ref[...]
    cb = cb_ref[...]
    for n, (c0, w) in enumerate(FF_SUBCHUNKS):
        if w == MXU_COLS:
            g = jnp.dot(h2, wg_ref[:, c0:c0 + w], preferred_element_type=F32)
            u = jnp.dot(h2, wu_ref[:, c0:c0 + w], preferred_element_type=F32)
        else:
            wgu = jnp.concatenate([wg_ref[:, c0:c0 + w], wu_ref[:, c0:c0 + w]], axis=1)
            gu = jnp.dot(h2, wgu, preferred_element_type=F32)
            g, u = gu[:, :w], gu[:, w:]
        g_prev, g_next = _shifted_rows(gs_ref, n % gs_ref.shape[0], g, seq)
        gc = (g * cw[1:2, c0:c0 + w] + g_prev * cw[0:1, c0:c0 + w]
              + g_next * cw[2:3, c0:c0 + w] + cb[:, c0:c0 + w])
        hg = 0.5 * gc
        act_ref[:, c0:c0 + w] = ((hg * u) * (1.0 + jnp.tanh(hg))).astype(BF16)
    y_ref[...] += mod[5:6] * jnp.dot(act_ref[...], wd_ref[...],
                                     preferred_element_type=F32)


def _mixffn(x2, yf2, o2, mod, mod_row, n2, w_out, w_gate, w_up, conv_w, conv_b2,
            w_down, seq):
    m, d = x2.shape
    tm = FFN_ROWS
    row = lambda i: (i, 0)
    const = lambda shape: pl.BlockSpec(shape, lambda i: (0, 0))
    stage_rows = SUBLANES + (tm // seq) * (seq + SUBLANES)
    return pl.pallas_call(
        functools.partial(_mixffn_kernel, seq=seq),
        out_shape=jax.ShapeDtypeStruct((m, d), F32),
        grid=(m // tm,),
        in_specs=[
            pl.BlockSpec((tm, d), row),
            pl.BlockSpec((tm, D_FOURIER), row),
            pl.BlockSpec((tm, D_ATTN), row),
            pl.BlockSpec((1, N_MOD, d), lambda i: (mod_row(i), 0, 0)),
            const((1, d)), const((d, d)), const((d, D_FF)), const((d, D_FF)),
            const((3, D_FF)), const((1, D_FF)), const((D_FF, d)),
        ],
        out_specs=pl.BlockSpec((tm, d), row),
        scratch_shapes=[
            pltpu.VMEM((tm, d), BF16),
            pltpu.VMEM((tm, D_FF), BF16),
            pltpu.VMEM((2, stage_rows, MXU_COLS), F32),
        ],
        compiler_params=_params(1),
        name="mix_ffn",
    )(x2, yf2, o2, mod, n2, w_out, w_gate, w_up, conv_w, conv_b2, w_down)


def _rope_tables(n_tokens):
    rows = n_tokens // GRID_W
    row = np.repeat(np.arange(rows, dtype=np.float64), GRID_W)
    col = np.tile(np.arange(GRID_W, dtype=np.float64), rows)
    n_freq = HEAD_DIM // 4
    inv = ROPE_THETA ** (-np.arange(n_freq, dtype=np.float64) / n_freq)
    ang_r = row[:, None] * inv
    ang_c = col[:, None] * inv
    zero = np.zeros_like(ang_r)
    cos64 = np.concatenate([np.cos(ang_r)] * 2 + [np.cos(ang_c)] * 2, axis=-1)
    sdn64 = np.concatenate([-np.sin(ang_r), zero, -np.sin(ang_c), zero], axis=-1)
    sup64 = np.concatenate([zero, np.sin(ang_r), zero, np.sin(ang_c)], axis=-1)
    two = lambda t: jnp.asarray(np.concatenate([t, t], axis=-1).astype(np.float32))
    return two(cos64), two(sdn64), two(sup64)


def kernel(x_prompt, x_sample, cache_k, cache_v, c, c_ctx, norm1_g, norm2_g, w_ada,
           b_ada, w_in, q_norm_g, k_norm_g, lam_q1, lam_k1, lam_q2, lam_k2, subln_g,
           w_out, w_gate, w_up, conv_w, conv_b, w_down):
    bp, sp, d = x_prompt.shape
    bs, ss, _ = x_sample.shape
    assert FFN_ROWS % sp == 0 and FFN_ROWS == ss

    cond = jnp.zeros((MOD_ROWS, d), F32).at[0].set(c_ctx).at[1:1 + bs].set(c)
    mod = _ada(cond, w_ada[0], b_ada[0]).reshape(MOD_ROWS, N_MOD, d)

    tile4 = lambda t: jnp.tile(t, MXU_COLS // HEAD_DIM)[None, :]
    n1, n2 = norm1_g[0][None, :], norm2_g[0][None, :]
    qg4, kg4, subg2 = tile4(q_norm_g[0]), tile4(k_norm_g[0]), subln_g[0][None, :]
    lamv = jnp.stack([lam_q1[0], lam_k1[0], lam_q2[0], lam_k2[0]])
    row_p = lambda i: 0
    row_s = lambda i: i + 1
    p = cache_k.shape[3]
    ctx = (cache_k[:, 0].reshape(bs, N_HEADS, p, V_DIM), cache_v[:, 0])

    yf_p, q_p, k_p, v_p = _inproj(x_prompt, mod, row_p, n1, w_in[0], qg4, kg4, None,
                                  FFN_ROWS // sp, F32)
    yf_s, q_s, k_s, v_s = _inproj(x_sample, mod, row_s, n1, w_in[0], qg4, kg4,
                                  _rope_tables(ss), FFN_ROWS // ss, BF16)
    (o_p,) = _attention(lamv, subg2, q_p, k_p, v_p, None, ATTN_PROMPT_REQS, sp)
    o_s, wg_b, wu_b, wd_b = _attention(lamv, subg2, q_s, k_s, v_s, ctx, 1, ATTN_TQ,
                                       cast=(w_gate[0], w_up[0], w_down[0]))

    def mixer(x, yf, o, mod_row):
        b, s, _ = x.shape
        y = _mixffn(x.reshape(b * s, d), yf.reshape(b * s, D_FOURIER),
                    o.reshape(b * s, D_ATTN), mod, mod_row, n2, w_out[0], wg_b, wu_b,
                    conv_w[0], conv_b[0][None, :], wd_b, s)
        return y.reshape(b, s, d)

    y_p = mixer(x_prompt, yf_p, o_p, row_p)
    y_s = mixer(x_sample, yf_s, o_s, row_s)

    new_k = k_p.reshape(bp, 1, N_HEADS, sp, 2, HEAD_DIM)
    new_v = v_p.reshape(bp, 1, N_HEADS, sp, V_DIM)
    return (y_p, y_s, new_k, new_v)
```

```python
import functools
import math

import jax
import jax.numpy as jnp
import numpy as np
from jax import lax
from jax.experimental import pallas as pl
from jax.experimental.pallas import tpu as pltpu

F32 = jnp.float32
BF16 = jnp.bfloat16

D_MODEL = 1024
GRID_W = 64
N_HEADS = 6
HEAD_DIM = 64
V_DIM = 2 * HEAD_DIM
D_FOURIER = 256
FOURIER_HEAD_DIM = 64
D_QK = N_HEADS * V_DIM
D_ATTN = N_HEADS * V_DIM
D_IN = D_FOURIER + 2 * D_QK + D_ATTN
D_FF = 2816
ROPE_THETA = 10000.0
EPS = 1e-6
LAM_INIT = 0.8 - 0.6 * math.exp(-0.3 * 0)
Q_SCALE = HEAD_DIM ** -0.5 * math.log2(math.e)
N_MOD = 6
MOD_ROWS = 16

SUBLANES = 8
MXU_COLS = 256
FF_SUBCHUNKS = tuple((c, min(MXU_COLS, D_FF - c)) for c in range(0, D_FF, MXU_COLS))
FFN_ROWS = 1024
ATTN_TQ = 512
ATTN_PROMPT_REQS = 1
VMEM_LIMIT = 58 * 1024 * 1024


def _params(n_axes):
    return pltpu.CompilerParams(
        dimension_semantics=("arbitrary",) * n_axes,
        vmem_limit_bytes=VMEM_LIMIT,
    )


def _ada_kernel(cond_ref, w_ref, b_ref, o_ref):
    c = cond_ref[...]
    s = c * jax.nn.sigmoid(c)
    o_ref[...] = jnp.dot(s.astype(BF16), w_ref[...].astype(BF16),
                         preferred_element_type=F32) + b_ref[...]


def _ada(cond, w_ada, b_ada):
    n = w_ada.shape[1]
    tn = 1536
    return pl.pallas_call(
        _ada_kernel,
        out_shape=jax.ShapeDtypeStruct((MOD_ROWS, n), F32),
        grid=(n // tn,),
        in_specs=[
            pl.BlockSpec((MOD_ROWS, D_MODEL), lambda j: (0, 0)),
            pl.BlockSpec((D_MODEL, tn), lambda j: (0, j)),
            pl.BlockSpec((1, tn), lambda j: (0, j)),
        ],
        out_specs=pl.BlockSpec((MOD_ROWS, tn), lambda j: (0, j)),
        compiler_params=_params(1),
        name="ada_mod",
    )(cond, w_ada, b_ada.reshape(1, n))


def _rms_rows(x, gain):
    ms = jnp.mean(x * x, axis=-1, keepdims=True)
    return x * lax.rsqrt(ms + EPS) * gain


def _group_rmsnorm(x, gain, ones_blk):
    ss = jnp.dot((x * x).astype(BF16), ones_blk, preferred_element_type=F32)
    return x * lax.rsqrt(ss * (1.0 / HEAD_DIM) + EPS) * gain


def _rope(x, cos, sin_dn, sin_up):
    return (x * cos + pltpu.roll(x, V_DIM - 16, 1) * sin_dn
            + pltpu.roll(x, 16, 1) * sin_up)


def _inproj_kernel(x_ref, mod_ref, n1_ref, w_ref, qg_ref, kg_ref, ones_ref, cc_ref,
                   sc_ref, cs_ref, ss_ref, *rest, use_rope, kv_dtype):
    if use_rope:
        cos_ref, sdn_ref, sup_ref, yf_ref, q_ref, k_ref, v_ref = rest
    else:
        yf_ref, q_ref, k_ref, v_ref = rest
    nb, seq, d = x_ref.shape
    x = x_ref[...].reshape(nb * seq, d)
    mod = mod_ref[0]
    h = (_rms_rows(x, n1_ref[...]) * (1.0 + mod[1:2]) + mod[0:1]).astype(BF16)

    def project(c0):
        return jnp.dot(h, w_ref[:, c0:c0 + MXU_COLS].astype(BF16),
                       preferred_element_type=F32)

    xf = project(0).astype(BF16)
    xc = jnp.dot(xf, cc_ref[...], preferred_element_type=F32).astype(BF16)
    xs = jnp.dot(xf, sc_ref[...], preferred_element_type=F32).astype(BF16)
    for b in range(nb):
        r = slice(b * seq, (b + 1) * seq)
        yf = (jnp.dot(cs_ref[...], xc[r], preferred_element_type=F32)
              - jnp.dot(ss_ref[...], xs[r], preferred_element_type=F32))
        yf_ref[b] = yf.astype(yf_ref.dtype)

    def store_heads(out_ref, head0, y, dtype):
        for hh in range(y.shape[1] // V_DIM):
            for b in range(nb):
                out_ref[b, head0 + hh] = y[b * seq:(b + 1) * seq,
                                           hh * V_DIM:(hh + 1) * V_DIM].astype(dtype)

    ones_blk = ones_ref[...]
    for pair in range(N_HEADS // 2):
        c0 = D_FOURIER + pair * MXU_COLS
        zq = project(c0)
        zk = project(c0 + D_QK)
        zv = project(c0 + 2 * D_QK)
        q = _group_rmsnorm(zq, qg_ref[...], ones_blk)
        k = _group_rmsnorm(zk, kg_ref[...], ones_blk)
        for hh in range(2):
            ls = slice(hh * V_DIM, (hh + 1) * V_DIM)
            qh, kh = q[:, ls], k[:, ls]
            if use_rope:
                qh = _rope(qh, cos_ref[...], sdn_ref[...], sup_ref[...])
                kh = _rope(kh, cos_ref[...], sdn_ref[...], sup_ref[...])
            store_heads(q_ref, 2 * pair + hh, qh * Q_SCALE, BF16)
            store_heads(k_ref, 2 * pair + hh, kh, kv_dtype)
        store_heads(v_ref, 2 * pair, zv, kv_dtype)


def _dft_tables(n, scale):
    idx = np.arange(n, dtype=np.int64)
    ang = ((idx[:, None] * idx[None, :]) % n).astype(np.float64) * (2.0 * math.pi / n)
    return ((np.cos(ang) * scale).astype(np.float32),
            (np.sin(ang) * scale).astype(np.float32))


def _fourier_tables(s):
    c64, s64 = _dft_tables(FOURIER_HEAD_DIM, (s * FOURIER_HEAD_DIM) ** -0.5)
    eye = np.eye(D_FOURIER // FOURIER_HEAD_DIM, dtype=np.float32)
    tabs = (np.kron(eye, c64), np.kron(eye, s64)) + _dft_tables(s, 1.0)
    return tuple(jnp.asarray(t).astype(BF16) for t in tabs)


def _inproj(x, mod, mod_row, n1, w_in, qg4, kg4, rope, nb, kv_dtype):
    b, s, d = x.shape
    use_rope = rope is not None
    ones_blk = jnp.asarray(np.kron(np.eye(MXU_COLS // HEAD_DIM, dtype=np.float32),
                                   np.ones((HEAD_DIM, HEAD_DIM), np.float32))).astype(BF16)
    const = lambda shape: pl.BlockSpec(shape, lambda i: (0, 0))
    in_specs = [
        pl.BlockSpec((nb, s, d), lambda i: (i, 0, 0)),
        pl.BlockSpec((1, N_MOD, d), lambda i: (mod_row(i), 0, 0)),
        const((1, d)), const((d, D_IN)), const((1, MXU_COLS)), const((1, MXU_COLS)),
        const((MXU_COLS, MXU_COLS)), const((D_FOURIER, D_FOURIER)),
        const((D_FOURIER, D_FOURIER)), const((s, s)), const((s, s)),
    ]
    args = [x, mod, n1, w_in, qg4, kg4, ones_blk, *_fourier_tables(s)]
    if use_rope:
        assert nb == 1
        in_specs += [const((s, V_DIM))] * 3
        args += list(rope)
    head_spec = pl.BlockSpec((nb, N_HEADS, s, V_DIM), lambda i: (i, 0, 0, 0))
    return pl.pallas_call(
        functools.partial(_inproj_kernel, use_rope=use_rope, kv_dtype=kv_dtype),
        out_shape=(
            jax.ShapeDtypeStruct((b, s, D_FOURIER), BF16),
            jax.ShapeDtypeStruct((b, N_HEADS, s, V_DIM), BF16),
            jax.ShapeDtypeStruct((b, N_HEADS, s, V_DIM), kv_dtype),
            jax.ShapeDtypeStruct((b, N_HEADS, s, V_DIM), kv_dtype),
        ),
        grid=(b // nb,),
        in_specs=in_specs,
        out_specs=(
            pl.BlockSpec((nb, s, D_FOURIER), lambda i: (i, 0, 0)),
            head_spec, head_spec, head_spec,
        ),
        compiler_params=_params(1),
        name="inproj_rope" if use_rope else "inproj",
    )(*args)


def _scores(q, k):
    return lax.dot_general(q, k, (((1,), (1,)), ((), ())),
                           preferred_element_type=F32)


def _attn_kernel(lam_ref, subg_ref, q_ref, k_ref, v_ref, *rest, has_ctx):
    n_cast = (len(rest) - (4 if has_ctx else 2)) // 2
    if has_ctx:
        ckt_ref, cv_ref = rest[:2]
        rest = rest[2:]
    cast_in = rest[:n_cast]
    o_ref = rest[n_cast]
    cast_out = rest[n_cast + 1:2 * n_cast + 1]
    (vv_ref,) = rest[2 * n_cast + 1:]
    nb, heads, s_self, _ = k_ref.shape

    for src, dst in zip(cast_in, cast_out):
        dst[...] = src[...].astype(BF16)

    @pl.when(pl.program_id(1) == 0)
    def _():
        for b in range(nb):
            for hd in range(heads):
                vv_ref[b, hd, :s_self, :V_DIM] = v_ref[b, hd].astype(BF16)
                if has_ctx:
                    vv_ref[b, hd, s_self:, :V_DIM] = cv_ref[b, hd].astype(BF16)
                vv_ref[b, hd, :, V_DIM:] = jnp.ones((vv_ref.shape[2], V_DIM), BF16)

    lv = lam_ref[...]
    e1 = jnp.exp(jnp.sum(lv[0:1] * lv[1:2], axis=-1, keepdims=True))
    e2 = jnp.exp(jnp.sum(lv[2:3] * lv[3:4], axis=-1, keepdims=True))
    lam = e1 - e2 + LAM_INIT
    lane = lax.broadcasted_iota(jnp.int32, (1, V_DIM), 1)
    lo_mask = lane < HEAD_DIM
    subg = subg_ref[...]
    for b in range(nb):
        for hd in range(heads):
            q = q_ref[b, hd]
            zero = jnp.zeros_like(q)
            kk = k_ref[b, hd].astype(BF16)
            if has_ctx:
                ckt = ckt_ref[b, hd].astype(BF16)
            vv = vv_ref[b, hd]
            res = []
            for qc in (jnp.where(lo_mask, q, zero), jnp.where(lo_mask, zero, q)):
                s = _scores(qc, kk)
                if has_ctx:
                    s = jnp.concatenate(
                        [s, jnp.dot(qc, ckt, preferred_element_type=F32)], axis=1)
                p = jnp.exp2(s - s.max(axis=-1, keepdims=True)).astype(BF16)
                res.append(jnp.dot(p, vv, preferred_element_type=F32))
            r0, r1 = res
            o = r0[:, :V_DIM] / r0[:, V_DIM:] - lam * (r1[:, :V_DIM] / r1[:, V_DIM:])
            ms = jnp.mean(o * o, axis=-1, keepdims=True)
            o = o * lax.rsqrt(ms + EPS) * subg * (1.0 - LAM_INIT)
            o_ref[b, :, hd * V_DIM:(hd + 1) * V_DIM] = o.astype(o_ref.dtype)


def _attention(lamv, subg2, q, k, v, ctx, nb, tq, cast=()):
    b, nh, s, _ = q.shape
    has_ctx = ctx is not None
    sk = s + (ctx[1].shape[2] if has_ctx else 0)
    nt = s // tq
    steps = (b // nb) * nt
    qspec = pl.BlockSpec((nb, nh, tq, V_DIM), lambda i, t: (i, 0, t, 0))
    kvspec = pl.BlockSpec((nb, nh, s, V_DIM), lambda i, t: (i, 0, 0, 0))
    in_specs = [
        pl.BlockSpec((4, HEAD_DIM), lambda i, t: (0, 0)),
        pl.BlockSpec((1, V_DIM), lambda i, t: (0, 0)),
        qspec, kvspec, kvspec,
    ]
    args = [lamv, subg2, q, k, v]
    if has_ctx:
        p = ctx[1].shape[2]
        in_specs += [pl.BlockSpec((nb, nh, V_DIM, p), lambda i, t: (i, 0, 0, 0)),
                     pl.BlockSpec((nb, nh, p, V_DIM), lambda i, t: (i, 0, 0, 0))]
        args += list(ctx)
    cast_specs = [pl.BlockSpec((w.shape[0] // steps, w.shape[1]),
                               lambda i, t: (i * nt + t, 0)) for w in cast]
    outs = pl.pallas_call(
        functools.partial(_attn_kernel, has_ctx=has_ctx),
        out_shape=(jax.ShapeDtypeStruct((b, s, nh * V_DIM), BF16),
                   *(jax.ShapeDtypeStruct(w.shape, BF16) for w in cast)),
        grid=(b // nb, nt),
        in_specs=in_specs + cast_specs,
        out_specs=(pl.BlockSpec((nb, tq, nh * V_DIM), lambda i, t: (i, t, 0)),
                   *cast_specs),
        scratch_shapes=[pltpu.VMEM((nb, nh, sk, 2 * V_DIM), BF16)],
        compiler_params=_params(2),
        name="diff_attn_ctx" if has_ctx else "diff_attn",
    )(*args, *cast)
    return outs


def _shifted_rows(gs_ref, slot, g, seq):
    rows, w = g.shape
    pitch = seq + SUBLANES
    for s in range(rows // seq):
        base = SUBLANES + s * pitch
        gs_ref[slot, base:base + seq, :w] = g[s * seq:(s + 1) * seq]
    shifted = []
    for off in (-1, 1):
        parts = [gs_ref[slot, SUBLANES + s * pitch + off:
                        SUBLANES + s * pitch + off + seq, :w]
                 for s in range(rows // seq)]
        shifted.append(parts[0] if len(parts) == 1 else jnp.concatenate(parts, axis=0))
    return shifted


def _mixffn_kernel(x_ref, yf_ref, o_ref, mod_ref, n2_ref, wo_ref, wg_ref, wu_ref,
                   cw_ref, cb_ref, wd_ref, y_ref, h2_ref, act_ref, gs_ref, *, seq):
    mod = mod_ref[0]
    rows = x_ref.shape[0]

    mix = (jnp.dot(yf_ref[...], wo_ref[:D_FOURIER, :].astype(BF16),
                   preferred_element_type=F32)
           + jnp.dot(o_ref[...], wo_ref[D_FOURIER:, :].astype(BF16),
                     preferred_element_type=F32))
    x1 = x_ref[...] + mod[2:3] * mix
    y_ref[...] = x1
    h2 = _rms_rows(x1, n2_ref[...]) * (1.0 + mod[4:5]) + mod[3:4]
    h2_ref[...] = h2.astype(BF16)
    for slot in range(gs_ref.shape[0]):
        for s in range(rows // seq + 1):
            r0 = s * (seq + SUBLANES)
            gs_ref[slot, r0:r0 + SUBLANES, :] = jnp.zeros(
                (SUBLANES, gs_ref.shape[2]), F32)

    h2 = h2_ref[...]
    cw = cw_ref[...]
    cb = cb_ref[...]
    for n, (c0, w) in enumerate(FF_SUBCHUNKS):
        if w == MXU_COLS:
            g = jnp.dot(h2, wg_ref[:, c0:c0 + w], preferred_element_type=F32)
            u = jnp.dot(h2, wu_ref[:, c0:c0 + w], preferred_element_type=F32)
        else:
            wgu = jnp.concatenate([wg_ref[:, c0:c0 + w], wu_ref[:, c0:c0 + w]], axis=1)
            gu = jnp.dot(h2, wgu, preferred_element_type=F32)
            g, u = gu[:, :w], gu[:, w:]
        g_prev, g_next = _shifted_rows(gs_ref, n % gs_ref.shape[0], g, seq)
        gc = (g * cw[1:2, c0:c0 + w] + g_prev * cw[0:1, c0:c0 + w]
              + g_next * cw[2:3, c0:c0 + w] + cb[:, c0:c0 + w])
        hg = 0.5 * gc
        act_ref[:, c0:c0 + w] = ((hg * u) * (1.0 + jnp.tanh(hg))).astype(BF16)
    y_ref[...] += mod[5:6] * jnp.dot(act_ref[...], wd_ref[...],
                                     preferred_element_type=F32)


def _mixffn(x2, yf2, o2, mod, mod_row, n2, w_out, w_gate, w_up, conv_w, conv_b2,
            w_down, seq):
    m, d = x2.shape
    tm = FFN_ROWS
    row = lambda i: (i, 0)
    const = lambda shape: pl.BlockSpec(shape, lambda i: (0, 0))
    stage_rows = SUBLANES + (tm // seq) * (seq + SUBLANES)
    return pl.pallas_call(
        functools.partial(_mixffn_kernel, seq=seq),
        out_shape=jax.ShapeDtypeStruct((m, d), F32),
        grid=(m // tm,),
        in_specs=[
            pl.BlockSpec((tm, d), row),
            pl.BlockSpec((tm, D_FOURIER), row),
            pl.BlockSpec((tm, D_ATTN), row),
            pl.BlockSpec((1, N_MOD, d), lambda i: (mod_row(i), 0, 0)),
            const((1, d)), const((d, d)), const((d, D_FF)), const((d, D_FF)),
            const((3, D_FF)), const((1, D_FF)), const((D_FF, d)),
        ],
        out_specs=pl.BlockSpec((tm, d), row),
        scratch_shapes=[
            pltpu.VMEM((tm, d), BF16),
            pltpu.VMEM((tm, D_FF), BF16),
            pltpu.VMEM((2, stage_rows, MXU_COLS), F32),
        ],
        compiler_params=_params(1),
        name="mix_ffn",
    )(x2, yf2, o2, mod, n2, w_out, w_gate, w_up, conv_w, conv_b2, w_down)


def _rope_tables(n_tokens):
    rows = n_tokens // GRID_W
    row = np.repeat(np.arange(rows, dtype=np.float64), GRID_W)
    col = np.tile(np.arange(GRID_W, dtype=np.float64), rows)
    n_freq = HEAD_DIM // 4
    inv = ROPE_THETA ** (-np.arange(n_freq, dtype=np.float64) / n_freq)
    ang_r = row[:, None] * inv
    ang_c = col[:, None] * inv
    zero = np.zeros_like(ang_r)
    cos64 = np.concatenate([np.cos(ang_r)] * 2 + [np.cos(ang_c)] * 2, axis=-1)
    sdn64 = np.concatenate([-np.sin(ang_r), zero, -np.sin(ang_c), zero], axis=-1)
    sup64 = np.concatenate([zero, np.sin(ang_r), zero, np.sin(ang_c)], axis=-1)
    two = lambda t: jnp.asarray(np.concatenate([t, t], axis=-1).astype(np.float32))
    return two(cos64), two(sdn64), two(sup64)


def kernel(x_prompt, x_sample, cache_k, cache_v, c, c_ctx, norm1_g, norm2_g, w_ada,
           b_ada, w_in, q_norm_g, k_norm_g, lam_q1, lam_k1, lam_q2, lam_k2, subln_g,
           w_out, w_gate, w_up, conv_w, conv_b, w_down):
    bp, sp, d = x_prompt.shape
    bs, ss, _ = x_sample.shape
    assert FFN_ROWS % sp == 0 and FFN_ROWS == ss

    cond = jnp.zeros((MOD_ROWS, d), F32).at[0].set(c_ctx).at[1:1 + bs].set(c)
    mod = _ada(cond, w_ada[0], b_ada[0]).reshape(MOD_ROWS, N_MOD, d)

    tile4 = lambda t: jnp.tile(t, MXU_COLS // HEAD_DIM)[None, :]
    n1, n2 = norm1_g[0][None, :], norm2_g[0][None, :]
    qg4, kg4, subg2 = tile4(q_norm_g[0]), tile4(k_norm_g[0]), subln_g[0][None, :]
    lamv = jnp.stack([lam_q1[0], lam_k1[0], lam_q2[0], lam_k2[0]])
    row_p = lambda i: 0
    row_s = lambda i: i + 1
    p = cache_k.shape[3]
    ck_t = jnp.transpose(cache_k[:, 0], (0, 1, 3, 4, 2)).reshape(bs, N_HEADS, V_DIM, p)
    ctx = (ck_t, cache_v[:, 0])

    yf_p, q_p, k_p, v_p = _inproj(x_prompt, mod, row_p, n1, w_in[0], qg4, kg4, None,
                                  FFN_ROWS // sp, F32)
    yf_s, q_s, k_s, v_s = _inproj(x_sample, mod, row_s, n1, w_in[0], qg4, kg4,
                                  _rope_tables(ss), FFN_ROWS // ss, BF16)
    (o_p,) = _attention(lamv, subg2, q_p, k_p, v_p, None, ATTN_PROMPT_REQS, sp)
    o_s, wg_b, wu_b, wd_b = _attention(lamv, subg2, q_s, k_s, v_s, ctx, 1, ATTN_TQ,
                                       cast=(w_gate[0], w_up[0], w_down[0]))

    def mixer(x, yf, o, mod_row):
        b, s, _ = x.shape
        y = _mixffn(x.reshape(b * s, d), yf.reshape(b * s, D_FOURIER),
                    o.reshape(b * s, D_ATTN), mod, mod_row, n2, w_out[0], wg_b, wu_b,
                    conv_w[0], conv_b[0][None, :], wd_b, s)
        return y.reshape(b, s, d)

    y_p = mixer(x_prompt, yf_p, o_p, row_p)
    y_s = mixer(x_sample, yf_s, o_s, row_s)

    new_k = k_p.reshape(bp, 1, N_HEADS, sp, 2, HEAD_DIM)
    new_v = v_p.reshape(bp, 1, N_HEADS, sp, V_DIM)
    return (y_p, y_s, new_k, new_v)
```

```python
import functools
import math

import jax
import jax.numpy as jnp
import numpy as np
from jax import lax
from jax.experimental import pallas as pl
from jax.experimental.pallas import tpu as pltpu

F32 = jnp.float32
BF16 = jnp.bfloat16

D_MODEL = 1024
GRID_W = 64
N_HEADS = 6
HEAD_DIM = 64
V_DIM = 2 * HEAD_DIM
D_FOURIER = 256
FOURIER_HEAD_DIM = 64
D_QK = N_HEADS * V_DIM
D_ATTN = N_HEADS * V_DIM
D_IN = D_FOURIER + 2 * D_QK + D_ATTN
D_FF = 2816
ROPE_THETA = 10000.0
EPS = 1e-6
LAM_INIT = 0.8 - 0.6 * math.exp(-0.3 * 0)
Q_SCALE = HEAD_DIM ** -0.5 * math.log2(math.e)
N_MOD = 6
MOD_ROWS = 16

SUBLANES = 8
MXU_COLS = 256
FF_SUBCHUNKS = tuple((c, min(MXU_COLS, D_FF - c)) for c in range(0, D_FF, MXU_COLS))
FFN_ROWS = 1024
ATTN_TQ = 512
ATTN_PROMPT_REQS = 4
VMEM_LIMIT = 58 * 1024 * 1024


def _params(n_axes):
    return pltpu.CompilerParams(
        dimension_semantics=("arbitrary",) * n_axes,
        vmem_limit_bytes=VMEM_LIMIT,
    )


def _ada_kernel(cond_ref, w_ref, b_ref, o_ref):
    c = cond_ref[...]
    s = c * jax.nn.sigmoid(c)
    o_ref[...] = jnp.dot(s.astype(BF16), w_ref[...].astype(BF16),
                         preferred_element_type=F32) + b_ref[...]


def _ada(cond, w_ada, b_ada):
    n = w_ada.shape[1]
    tn = 1536
    return pl.pallas_call(
        _ada_kernel,
        out_shape=jax.ShapeDtypeStruct((MOD_ROWS, n), F32),
        grid=(n // tn,),
        in_specs=[
            pl.BlockSpec((MOD_ROWS, D_MODEL), lambda j: (0, 0)),
            pl.BlockSpec((D_MODEL, tn), lambda j: (0, j)),
            pl.BlockSpec((1, tn), lambda j: (0, j)),
        ],
        out_specs=pl.BlockSpec((MOD_ROWS, tn), lambda j: (0, j)),
        compiler_params=_params(1),
        name="ada_mod",
    )(cond, w_ada, b_ada.reshape(1, n))


def _rms_rows(x, gain):
    ms = jnp.mean(x * x, axis=-1, keepdims=True)
    return x * lax.rsqrt(ms + EPS) * gain


def _group_rmsnorm(x, gain, ones_blk):
    ss = jnp.dot((x * x).astype(BF16), ones_blk, preferred_element_type=F32)
    return x * lax.rsqrt(ss * (1.0 / HEAD_DIM) + EPS) * gain


def _rope(x, cos, sin_dn, sin_up):
    return (x * cos + pltpu.roll(x, V_DIM - 16, 1) * sin_dn
            + pltpu.roll(x, 16, 1) * sin_up)


def _inproj_kernel(x_ref, mod_ref, n1_ref, w_ref, qkg_ref, ones_ref, cc_ref,
                   sc_ref, cs_ref, ss_ref, *rest, use_rope, kv_dtype):
    if use_rope:
        cos_ref, sdn_ref, sup_ref, yf_ref, q_ref, k_ref, v_ref = rest
    else:
        yf_ref, q_ref, k_ref, v_ref = rest
    nb, seq, d = x_ref.shape
    x = x_ref[...].reshape(nb * seq, d)
    mod = mod_ref[0]
    h = (_rms_rows(x, n1_ref[...]) * (1.0 + mod[1:2]) + mod[0:1]).astype(BF16)

    def project(c0):
        return jnp.dot(h, w_ref[:, c0:c0 + MXU_COLS].astype(BF16),
                       preferred_element_type=F32)

    xf = project(0).astype(BF16)
    xc = jnp.dot(xf, cc_ref[...], preferred_element_type=F32).astype(BF16)
    xs = jnp.dot(xf, sc_ref[...], preferred_element_type=F32).astype(BF16)
    for b in range(nb):
        r = slice(b * seq, (b + 1) * seq)
        yf = (jnp.dot(cs_ref[...], xc[r], preferred_element_type=F32)
              - jnp.dot(ss_ref[...], xs[r], preferred_element_type=F32))
        yf_ref[b] = yf.astype(yf_ref.dtype)

    def store_heads(out_ref, head0, y, dtype):
        for hh in range(y.shape[1] // V_DIM):
            for b in range(nb):
                out_ref[b, head0 + hh] = y[b * seq:(b + 1) * seq,
                                           hh * V_DIM:(hh + 1) * V_DIM].astype(dtype)

    ones_blk = ones_ref[...]
    for pair in range(N_HEADS // 2):
        c0 = D_FOURIER + pair * MXU_COLS
        zq = project(c0)
        zk = project(c0 + D_QK)
        zv = project(c0 + 2 * D_QK)
        q = _group_rmsnorm(zq, qkg_ref[0:1, :], ones_blk)
        k = _group_rmsnorm(zk, qkg_ref[1:2, :], ones_blk)
        for hh in range(2):
            ls = slice(hh * V_DIM, (hh + 1) * V_DIM)
            qh, kh = q[:, ls], k[:, ls]
            if use_rope:
                qh = _rope(qh, cos_ref[...], sdn_ref[...], sup_ref[...])
                kh = _rope(kh, cos_ref[...], sdn_ref[...], sup_ref[...])
            store_heads(q_ref, 2 * pair + hh, qh * Q_SCALE, BF16)
            store_heads(k_ref, 2 * pair + hh, kh, kv_dtype)
        store_heads(v_ref, 2 * pair, zv, kv_dtype)


def _dft_tables(n, scale):
    idx = np.arange(n, dtype=np.int64)
    ang = ((idx[:, None] * idx[None, :]) % n).astype(np.float64) * (2.0 * math.pi / n)
    return ((np.cos(ang) * scale).astype(np.float32),
            (np.sin(ang) * scale).astype(np.float32))


def _fourier_tables(s):
    c64, s64 = _dft_tables(FOURIER_HEAD_DIM, (s * FOURIER_HEAD_DIM) ** -0.5)
    eye = np.eye(D_FOURIER // FOURIER_HEAD_DIM, dtype=np.float32)
    tabs = (np.kron(eye, c64), np.kron(eye, s64)) + _dft_tables(s, 1.0)
    return tuple(jnp.asarray(t).astype(BF16) for t in tabs)


def _inproj(x, mod, mod_row, n1, w_in, qkg, rope, nb, kv_dtype):
    b, s, d = x.shape
    use_rope = rope is not None
    ones_blk = jnp.asarray(np.kron(np.eye(MXU_COLS // HEAD_DIM, dtype=np.float32),
                                   np.ones((HEAD_DIM, HEAD_DIM), np.float32))).astype(BF16)
    const = lambda shape: pl.BlockSpec(shape, lambda i: (0, 0))
    in_specs = [
        pl.BlockSpec((nb, s, d), lambda i: (i, 0, 0)),
        pl.BlockSpec((1, N_MOD, d), lambda i: (mod_row(i), 0, 0)),
        const((1, d)), const((d, D_IN)), const((2, MXU_COLS)),
        const((MXU_COLS, MXU_COLS)), const((D_FOURIER, D_FOURIER)),
        const((D_FOURIER, D_FOURIER)), const((s, s)), const((s, s)),
    ]
    args = [x, mod, n1, w_in, qkg, ones_blk, *_fourier_tables(s)]
    if use_rope:
        assert nb == 1
        in_specs += [const((s, V_DIM))] * 3
        args += list(rope)
    head_spec = pl.BlockSpec((nb, N_HEADS, s, V_DIM), lambda i: (i, 0, 0, 0))
    return pl.pallas_call(
        functools.partial(_inproj_kernel, use_rope=use_rope, kv_dtype=kv_dtype),
        out_shape=(
            jax.ShapeDtypeStruct((b, s, D_FOURIER), BF16),
            jax.ShapeDtypeStruct((b, N_HEADS, s, V_DIM), BF16),
            jax.ShapeDtypeStruct((b, N_HEADS, s, V_DIM), kv_dtype),
            jax.ShapeDtypeStruct((b, N_HEADS, s, V_DIM), kv_dtype),
        ),
        grid=(b // nb,),
        in_specs=in_specs,
        out_specs=(
            pl.BlockSpec((nb, s, D_FOURIER), lambda i: (i, 0, 0)),
            head_spec, head_spec, head_spec,
        ),
        compiler_params=_params(1),
        name="inproj_rope" if use_rope else "inproj",
    )(*args)


def _scores(q, k):
    return lax.dot_general(q, k, (((1,), (1,)), ((), ())),
                           preferred_element_type=F32)


def _attn_kernel(lam_ref, subg_ref, q_ref, k_ref, v_ref, *rest, has_ctx):
    n_cast = (len(rest) - (4 if has_ctx else 2)) // 2
    if has_ctx:
        ckt_ref, cv_ref = rest[:2]
        rest = rest[2:]
    cast_in = rest[:n_cast]
    o_ref = rest[n_cast]
    cast_out = rest[n_cast + 1:2 * n_cast + 1]
    (vv_ref,) = rest[2 * n_cast + 1:]
    nb, heads, s_self, _ = k_ref.shape

    for src, dst in zip(cast_in, cast_out):
        dst[...] = src[...].astype(BF16)

    @pl.when(pl.program_id(1) == 0)
    def _():
        for b in range(nb):
            for hd in range(heads):
                vv_ref[b, hd, :s_self, :V_DIM] = v_ref[b, hd].astype(BF16)
                if has_ctx:
                    vv_ref[b, hd, s_self:, :V_DIM] = cv_ref[b, hd].astype(BF16)
                vv_ref[b, hd, :, V_DIM:] = jnp.ones((vv_ref.shape[2], V_DIM), BF16)

    lv = lam_ref[...]
    e1 = jnp.exp(jnp.sum(lv[0:1] * lv[1:2], axis=-1, keepdims=True))
    e2 = jnp.exp(jnp.sum(lv[2:3] * lv[3:4], axis=-1, keepdims=True))
    lam = e1 - e2 + LAM_INIT
    lane = lax.broadcasted_iota(jnp.int32, (1, V_DIM), 1)
    lo_mask = lane < HEAD_DIM
    subg = subg_ref[...]
    for b in range(nb):
        for hd in range(heads):
            q = q_ref[b, hd]
            zero = jnp.zeros_like(q)
            kk = k_ref[b, hd].astype(BF16)
            if has_ctx:
                ckt = ckt_ref[b, hd].astype(BF16)
            vv = vv_ref[b, hd]
            res = []
            for qc in (jnp.where(lo_mask, q, zero), jnp.where(lo_mask, zero, q)):
                s = _scores(qc, kk)
                if has_ctx:
                    s = jnp.concatenate(
                        [s, jnp.dot(qc, ckt, preferred_element_type=F32)], axis=1)
                p = jnp.exp2(s - s.max(axis=-1, keepdims=True)).astype(BF16)
                res.append(jnp.dot(p, vv, preferred_element_type=F32))
            r0, r1 = res
            o = r0[:, :V_DIM] / r0[:, V_DIM:] - lam * (r1[:, :V_DIM] / r1[:, V_DIM:])
            ms = jnp.mean(o * o, axis=-1, keepdims=True)
            o = o * lax.rsqrt(ms + EPS) * subg * (1.0 - LAM_INIT)
            o_ref[b, :, hd * V_DIM:(hd + 1) * V_DIM] = o.astype(o_ref.dtype)


def _attention(lamv, subg2, q, k, v, ctx, nb, tq, cast=()):
    b, nh, s, _ = q.shape
    has_ctx = ctx is not None
    sk = s + (ctx[1].shape[2] if has_ctx else 0)
    nt = s // tq
    steps = (b // nb) * nt
    qspec = pl.BlockSpec((nb, nh, tq, V_DIM), lambda i, t: (i, 0, t, 0))
    kvspec = pl.BlockSpec((nb, nh, s, V_DIM), lambda i, t: (i, 0, 0, 0))
    in_specs = [
        pl.BlockSpec((4, HEAD_DIM), lambda i, t: (0, 0)),
        pl.BlockSpec((1, V_DIM), lambda i, t: (0, 0)),
        qspec, kvspec, kvspec,
    ]
    args = [lamv, subg2, q, k, v]
    if has_ctx:
        p = ctx[1].shape[2]
        in_specs += [pl.BlockSpec((nb, nh, V_DIM, p), lambda i, t: (i, 0, 0, 0)),
                     pl.BlockSpec((nb, nh, p, V_DIM), lambda i, t: (i, 0, 0, 0))]
        args += list(ctx)
    cast_specs = [pl.BlockSpec((w.shape[0] // steps, w.shape[1]),
                               lambda i, t: (i * nt + t, 0)) for w in cast]
    outs = pl.pallas_call(
        functools.partial(_attn_kernel, has_ctx=has_ctx),
        out_shape=(jax.ShapeDtypeStruct((b, s, nh * V_DIM), BF16),
                   *(jax.ShapeDtypeStruct(w.shape, BF16) for w in cast)),
        grid=(b // nb, nt),
        in_specs=in_specs + cast_specs,
        out_specs=(pl.BlockSpec((nb, tq, nh * V_DIM), lambda i, t: (i, t, 0)),
                   *cast_specs),
        scratch_shapes=[pltpu.VMEM((nb, nh, sk, 2 * V_DIM), BF16)],
        compiler_params=_params(2),
        name="diff_attn_ctx" if has_ctx else "diff_attn",
    )(*args, *cast)
    return outs


def _shifted_rows(gs_ref, slot, g, seq):
    rows, w = g.shape
    pitch = seq + SUBLANES
    for s in range(rows // seq):
        base = SUBLANES + s * pitch
        gs_ref[slot, base:base + seq, :w] = g[s * seq:(s + 1) * seq]
    shifted = []
    for off in (-1, 1):
        parts = [gs_ref[slot, SUBLANES + s * pitch + off:
                        SUBLANES + s * pitch + off + seq, :w]
                 for s in range(rows // seq)]
        shifted.append(parts[0] if len(parts) == 1 else jnp.concatenate(parts, axis=0))
    return shifted


def _mixffn_kernel(x_ref, yf_ref, o_ref, mod_ref, n2_ref, wo_ref, wg_ref, wu_ref,
                   cw_ref, cb_ref, wd_ref, y_ref, h2_ref, act_ref, gs_ref, *, seq):
    mod = mod_ref[0]
    rows = x_ref.shape[0]

    mix = (jnp.dot(yf_ref[...], wo_ref[:D_FOURIER, :].astype(BF16),
                   preferred_element_type=F32)
           + jnp.dot(o_ref[...], wo_ref[D_FOURIER:, :].astype(BF16),
                     preferred_element_type=F32))
    x1 = x_ref[...] + mod[2:3] * mix
    y_ref[...] = x1
    h2 = _rms_rows(x1, n2_ref[...]) * (1.0 + mod[4:5]) + mod[3:4]
    h2_ref[...] = h2.astype(BF16)
    for slot in range(gs_ref.shape[0]):
        for s in range(rows // seq + 1):
            r0 = s * (seq + SUBLANES)
            gs_ref[slot, r0:r0 + SUBLANES, :] = jnp.zeros(
                (SUBLANES, gs_ref.shape[2]), F32)

    h2 = h2_ref[...]
    cw = cw_ref[...]
    cb = cb_ref[...]
    for n, (c0, w) in enumerate(FF_SUBCHUNKS):
        if w == MXU_COLS:
            g = jnp.dot(h2, wg_ref[:, c0:c0 + w], preferred_element_type=F32)
            u = jnp.dot(h2, wu_ref[:, c0:c0 + w], preferred_element_type=F32)
        else:
            wgu = jnp.concatenate([wg_ref[:, c0:c0 + w], wu_ref[:, c0:c0 + w]], axis=1)
            gu = jnp.dot(h2, wgu, preferred_element_type=F32)
            g, u = gu[:, :w], gu[:, w:]
        g_prev, g_next = _shifted_rows(gs_ref, n % gs_ref.shape[0], g, seq)
        gc = (g * cw[1:2, c0:c0 + w] + g_prev * cw[0:1, c0:c0 + w]
              + g_next * cw[2:3, c0:c0 + w] + cb[:, c0:c0 + w])
        hg = 0.5 * gc
        act_ref[:, c0:c0 + w] = ((hg * u) * (1.0 + jnp.tanh(hg))).astype(BF16)
    y_ref[...] += mod[5:6] * jnp.dot(act_ref[...], wd_ref[...],
                                     preferred_element_type=F32)


def _mixffn(x2, yf2, o2, mod, mod_row, n2, w_out, w_gate, w_up, conv_w, conv_b2,
            w_down, seq):
    m, d = x2.shape
    tm = FFN_ROWS
    row = lambda i: (i, 0)
    const = lambda shape: pl.BlockSpec(shape, lambda i: (0, 0))
    stage_rows = SUBLANES + (tm // seq) * (seq + SUBLANES)
    return pl.pallas_call(
        functools.partial(_mixffn_kernel, seq=seq),
        out_shape=jax.ShapeDtypeStruct((m, d), F32),
        grid=(m // tm,),
        in_specs=[
            pl.BlockSpec((tm, d), row),
            pl.BlockSpec((tm, D_FOURIER), row),
            pl.BlockSpec((tm, D_ATTN), row),
            pl.BlockSpec((1, N_MOD, d), lambda i: (mod_row(i), 0, 0)),
            const((1, d)), const((d, d)), const((d, D_FF)), const((d, D_FF)),
            const((3, D_FF)), const((1, D_FF)), const((D_FF, d)),
        ],
        out_specs=pl.BlockSpec((tm, d), row),
        scratch_shapes=[
            pltpu.VMEM((tm, d), BF16),
            pltpu.VMEM((tm, D_FF), BF16),
            pltpu.VMEM((2, stage_rows, MXU_COLS), F32),
        ],
        compiler_params=_params(1),
        name="mix_ffn",
    )(x2, yf2, o2, mod, n2, w_out, w_gate, w_up, conv_w, conv_b2, w_down)


def _rope_tables(n_tokens):
    rows = n_tokens // GRID_W
    row = np.repeat(np.arange(rows, dtype=np.float64), GRID_W)
    col = np.tile(np.arange(GRID_W, dtype=np.float64), rows)
    n_freq = HEAD_DIM // 4
    inv = ROPE_THETA ** (-np.arange(n_freq, dtype=np.float64) / n_freq)
    ang_r = row[:, None] * inv
    ang_c = col[:, None] * inv
    zero = np.zeros_like(ang_r)
    cos64 = np.concatenate([np.cos(ang_r)] * 2 + [np.cos(ang_c)] * 2, axis=-1)
    sdn64 = np.concatenate([-np.sin(ang_r), zero, -np.sin(ang_c), zero], axis=-1)
    sup64 = np.concatenate([zero, np.sin(ang_r), zero, np.sin(ang_c)], axis=-1)
    two = lambda t: jnp.asarray(np.concatenate([t, t], axis=-1).astype(np.float32))
    return two(cos64), two(sdn64), two(sup64)


def kernel(x_prompt, x_sample, cache_k, cache_v, c, c_ctx, norm1_g, norm2_g, w_ada,
           b_ada, w_in, q_norm_g, k_norm_g, lam_q1, lam_k1, lam_q2, lam_k2, subln_g,
           w_out, w_gate, w_up, conv_w, conv_b, w_down):
    bp, sp, d = x_prompt.shape
    bs, ss, _ = x_sample.shape
    assert FFN_ROWS % sp == 0 and FFN_ROWS == ss

    cond = jnp.concatenate(
        [c_ctx[None, :], c, jnp.zeros((MOD_ROWS - 1 - bs, d), F32)], axis=0)
    mod = _ada(cond, w_ada[0], b_ada[0]).reshape(MOD_ROWS, N_MOD, d)

    n1, n2, subg2 = norm1_g[0][None, :], norm2_g[0][None, :], subln_g[0][None, :]
    qkg = jnp.tile(jnp.concatenate([q_norm_g, k_norm_g], axis=0),
                   (1, MXU_COLS // HEAD_DIM))
    lamv = jnp.stack([lam_q1[0], lam_k1[0], lam_q2[0], lam_k2[0]])
    row_p = lambda i: 0
    row_s = lambda i: i + 1
    p = cache_k.shape[3]
    ck_t = jnp.transpose(cache_k[:, 0], (0, 1, 3, 4, 2)).reshape(bs, N_HEADS, V_DIM, p)
    ctx = (ck_t, cache_v[:, 0])

    yf_p, q_p, k_p, v_p = _inproj(x_prompt, mod, row_p, n1, w_in[0], qkg, None,
                                  FFN_ROWS // sp, F32)
    yf_s, q_s, k_s, v_s = _inproj(x_sample, mod, row_s, n1, w_in[0], qkg,
                                  _rope_tables(ss), FFN_ROWS // ss, BF16)
    (o_p,) = _attention(lamv, subg2, q_p, k_p, v_p, None, ATTN_PROMPT_REQS, sp)
    o_s, wg_b, wu_b, wd_b = _attention(lamv, subg2, q_s, k_s, v_s, ctx, 1, ATTN_TQ,
                                       cast=(w_gate[0], w_up[0], w_down[0]))

    def mixer(x, yf, o, mod_row):
        b, s, _ = x.shape
        y = _mixffn(x.reshape(b * s, d), yf.reshape(b * s, D_FOURIER),
                    o.reshape(b * s, D_ATTN), mod, mod_row, n2, w_out[0], wg_b, wu_b,
                    conv_w[0], conv_b[0][None, :], wd_b, s)
        return y.reshape(b, s, d)

    y_p = mixer(x_prompt, yf_p, o_p, row_p)
    y_s = mixer(x_sample, yf_s, o_s, row_s)

    new_k = k_p.reshape(bp, 1, N_HEADS, sp, 2, HEAD_DIM)
    new_v = v_p.reshape(bp, 1, N_HEADS, sp, V_DIM)
    return (y_p, y_s, new_k, new_v)
```

```python
import functools
import math

import jax
import jax.numpy as jnp
import numpy as np
from jax import lax
from jax.experimental import pallas as pl
from jax.experimental.pallas import tpu as pltpu

F32 = jnp.float32
BF16 = jnp.bfloat16

D_MODEL = 1024
GRID_W = 64
N_HEADS = 6
HEAD_DIM = 64
V_DIM = 2 * HEAD_DIM
D_FOURIER = 256
FOURIER_HEAD_DIM = 64
D_QK = N_HEADS * V_DIM
D_ATTN = N_HEADS * V_DIM
D_IN = D_FOURIER + 2 * D_QK + D_ATTN
D_FF = 2816
ROPE_THETA = 10000.0
EPS = 1e-6
LAM_INIT = 0.8 - 0.6 * math.exp(-0.3 * 0)
Q_SCALE = HEAD_DIM ** -0.5 * math.log2(math.e)
N_MOD = 6
MOD_ROWS = 16

SUBLANES = 8
MXU_COLS = 256
FF_SUBCHUNKS = tuple((c, min(MXU_COLS, D_FF - c)) for c in range(0, D_FF, MXU_COLS))
FFN_ROWS = 1024
ATTN_TQ = 512
VMEM_LIMIT = 58 * 1024 * 1024


def _params(n_axes):
    return pltpu.CompilerParams(
        dimension_semantics=("arbitrary",) * n_axes,
        vmem_limit_bytes=VMEM_LIMIT,
    )


def _ada_kernel(cond_ref, w_ref, b_ref, o_ref):
    c = cond_ref[...]
    s = c * jax.nn.sigmoid(c)
    o_ref[...] = jnp.dot(s.astype(BF16), w_ref[...].astype(BF16),
                         preferred_element_type=F32) + b_ref[...]


def _ada(cond, w_ada, b_ada):
    n = w_ada.shape[1]
    tn = 1536
    return pl.pallas_call(
        _ada_kernel,
        out_shape=jax.ShapeDtypeStruct((MOD_ROWS, n), F32),
        grid=(n // tn,),
        in_specs=[
            pl.BlockSpec((MOD_ROWS, D_MODEL), lambda j: (0, 0)),
            pl.BlockSpec((D_MODEL, tn), lambda j: (0, j)),
            pl.BlockSpec((1, tn), lambda j: (0, j)),
        ],
        out_specs=pl.BlockSpec((MOD_ROWS, tn), lambda j: (0, j)),
        compiler_params=_params(1),
        name="ada_mod",
    )(cond, w_ada, b_ada.reshape(1, n))


def _rms_rows(x, gain):
    ms = jnp.mean(x * x, axis=-1, keepdims=True)
    return x * lax.rsqrt(ms + EPS) * gain


def _group_rmsnorm(x, gain, ones_blk):
    ss = jnp.dot((x * x).astype(BF16), ones_blk, preferred_element_type=F32)
    return x * lax.rsqrt(ss * (1.0 / HEAD_DIM) + EPS) * gain


def _rope(x, cos, sin_dn, sin_up):
    return (x * cos + pltpu.roll(x, V_DIM - 16, 1) * sin_dn
            + pltpu.roll(x, 16, 1) * sin_up)


def _inproj_kernel(x_ref, mod_ref, n1_ref, w_ref, qkg_ref, ones_ref, cc_ref,
                   sc_ref, cs_ref, ss_ref, *rest, use_rope, attend, kv_dtype):
    if use_rope:
        cos_ref, sdn_ref, sup_ref = rest[:3]
        rest = rest[3:]
    if attend:
        lam_ref, subg_ref, yf_ref, o_ref, k_ref, v_ref = rest
        lam = _lam_value(lam_ref)
        ones_v = jnp.ones((x_ref.shape[1], V_DIM), BF16)
    else:
        yf_ref, q_ref, k_ref, v_ref = rest
    nb, seq, d = x_ref.shape
    x = x_ref[...].reshape(nb * seq, d)
    mod = mod_ref[0]
    h = (_rms_rows(x, n1_ref[...]) * (1.0 + mod[1:2]) + mod[0:1]).astype(BF16)

    def project(c0):
        return jnp.dot(h, w_ref[:, c0:c0 + MXU_COLS].astype(BF16),
                       preferred_element_type=F32)

    xf = project(0).astype(BF16)
    xc = jnp.dot(xf, cc_ref[...], preferred_element_type=F32).astype(BF16)
    xs = jnp.dot(xf, sc_ref[...], preferred_element_type=F32).astype(BF16)
    for b in range(nb):
        r = slice(b * seq, (b + 1) * seq)
        yf = (jnp.dot(cs_ref[...], xc[r], preferred_element_type=F32)
              - jnp.dot(ss_ref[...], xs[r], preferred_element_type=F32))
        yf_ref[b] = yf.astype(yf_ref.dtype)

    def store_heads(out_ref, head0, y, dtype):
        for hh in range(y.shape[1] // V_DIM):
            for b in range(nb):
                out_ref[b, head0 + hh] = y[b * seq:(b + 1) * seq,
                                           hh * V_DIM:(hh + 1) * V_DIM].astype(dtype)

    ones_blk = ones_ref[...]
    for pair in range(N_HEADS // 2):
        c0 = D_FOURIER + pair * MXU_COLS
        zq = project(c0)
        zk = project(c0 + D_QK)
        zv = project(c0 + 2 * D_QK)
        q = _group_rmsnorm(zq, qkg_ref[0:1, :], ones_blk)
        k = _group_rmsnorm(zk, qkg_ref[1:2, :], ones_blk)
        for hh in range(2):
            ls = slice(hh * V_DIM, (hh + 1) * V_DIM)
            qh, kh = q[:, ls], k[:, ls]
            if use_rope:
                qh = _rope(qh, cos_ref[...], sdn_ref[...], sup_ref[...])
                kh = _rope(kh, cos_ref[...], sdn_ref[...], sup_ref[...])
            hd = 2 * pair + hh
            qs = (qh * Q_SCALE).astype(BF16)
            store_heads(k_ref, hd, kh, kv_dtype)
            if not attend:
                store_heads(q_ref, hd, qs, BF16)
                continue
            kb, vb = kh.astype(BF16), zv[:, ls].astype(BF16)
            for b in range(nb):
                r = slice(b * seq, (b + 1) * seq)
                o = _attend_head(qs[r], kb[r], None,
                                 jnp.concatenate([vb[r], ones_v], axis=1), lam,
                                 subg_ref[...])
                o_ref[b, :, hd * V_DIM:(hd + 1) * V_DIM] = o.astype(o_ref.dtype)
        store_heads(v_ref, 2 * pair, zv, kv_dtype)


def _dft_tables(n, scale):
    idx = np.arange(n, dtype=np.int64)
    ang = ((idx[:, None] * idx[None, :]) % n).astype(np.float64) * (2.0 * math.pi / n)
    return ((np.cos(ang) * scale).astype(np.float32),
            (np.sin(ang) * scale).astype(np.float32))


def _fourier_tables(s):
    c64, s64 = _dft_tables(FOURIER_HEAD_DIM, (s * FOURIER_HEAD_DIM) ** -0.5)
    eye = np.eye(D_FOURIER // FOURIER_HEAD_DIM, dtype=np.float32)
    tabs = (np.kron(eye, c64), np.kron(eye, s64)) + _dft_tables(s, 1.0)
    return tuple(jnp.asarray(t).astype(BF16) for t in tabs)


def _inproj(x, mod, mod_row, n1, w_in, qkg, rope, nb, kv_dtype, attn=None):
    b, s, d = x.shape
    use_rope = rope is not None
    attend = attn is not None
    ones_blk = jnp.asarray(np.kron(np.eye(MXU_COLS // HEAD_DIM, dtype=np.float32),
                                   np.ones((HEAD_DIM, HEAD_DIM), np.float32))).astype(BF16)
    const = lambda shape: pl.BlockSpec(shape, lambda i: (0, 0))
    in_specs = [
        pl.BlockSpec((nb, s, d), lambda i: (i, 0, 0)),
        pl.BlockSpec((1, N_MOD, d), lambda i: (mod_row(i), 0, 0)),
        const((1, d)), const((d, D_IN)), const((2, MXU_COLS)),
        const((MXU_COLS, MXU_COLS)), const((D_FOURIER, D_FOURIER)),
        const((D_FOURIER, D_FOURIER)), const((s, s)), const((s, s)),
    ]
    args = [x, mod, n1, w_in, qkg, ones_blk, *_fourier_tables(s)]
    if use_rope:
        assert nb == 1
        in_specs += [const((s, V_DIM))] * 3
        args += list(rope)
    if attend:
        in_specs += [const((4, HEAD_DIM)), const((1, V_DIM))]
        args += list(attn)
    head_spec = pl.BlockSpec((nb, N_HEADS, s, V_DIM), lambda i: (i, 0, 0, 0))
    head_shape = (b, N_HEADS, s, V_DIM)
    if attend:
        second = (jax.ShapeDtypeStruct((b, s, D_ATTN), BF16),
                  pl.BlockSpec((nb, s, D_ATTN), lambda i: (i, 0, 0)))
    else:
        second = (jax.ShapeDtypeStruct(head_shape, BF16), head_spec)
    return pl.pallas_call(
        functools.partial(_inproj_kernel, use_rope=use_rope, attend=attend,
                          kv_dtype=kv_dtype),
        out_shape=(
            jax.ShapeDtypeStruct((b, s, D_FOURIER), BF16),
            second[0],
            jax.ShapeDtypeStruct(head_shape, kv_dtype),
            jax.ShapeDtypeStruct(head_shape, kv_dtype),
        ),
        grid=(b // nb,),
        in_specs=in_specs,
        out_specs=(
            pl.BlockSpec((nb, s, D_FOURIER), lambda i: (i, 0, 0)),
            second[1], head_spec, head_spec,
        ),
        compiler_params=_params(1),
        name="inproj_rope" if use_rope else "inproj_attn" if attend else "inproj",
    )(*args)


def _scores(q, k):
    return lax.dot_general(q, k, (((1,), (1,)), ((), ())),
                           preferred_element_type=F32)


def _lam_value(lam_ref):
    lv = lam_ref[...]
    e1 = jnp.exp(jnp.sum(lv[0:1] * lv[1:2], axis=-1, keepdims=True))
    e2 = jnp.exp(jnp.sum(lv[2:3] * lv[3:4], axis=-1, keepdims=True))
    return e1 - e2 + LAM_INIT


def _attend_head(q, kk, ckt, vv, lam, subg):
    lo_mask = lax.broadcasted_iota(jnp.int32, (1, V_DIM), 1) < HEAD_DIM
    zero = jnp.zeros_like(q)
    res = []
    for qc in (jnp.where(lo_mask, q, zero), jnp.where(lo_mask, zero, q)):
        s = _scores(qc, kk)
        if ckt is not None:
            s = jnp.concatenate([s, jnp.dot(qc, ckt, preferred_element_type=F32)],
                                axis=1)
        p = jnp.exp2(s - s.max(axis=-1, keepdims=True)).astype(BF16)
        res.append(jnp.dot(p, vv, preferred_element_type=F32))
    r0, r1 = res
    o = r0[:, :V_DIM] / r0[:, V_DIM:] - lam * (r1[:, :V_DIM] / r1[:, V_DIM:])
    ms = jnp.mean(o * o, axis=-1, keepdims=True)
    return o * lax.rsqrt(ms + EPS) * subg * (1.0 - LAM_INIT)


def _attn_kernel(lam_ref, subg_ref, q_ref, k_ref, v_ref, *rest, has_ctx):
    n_cast = (len(rest) - (4 if has_ctx else 2)) // 2
    if has_ctx:
        ckt_ref, cv_ref = rest[:2]
        rest = rest[2:]
    cast_in = rest[:n_cast]
    o_ref = rest[n_cast]
    cast_out = rest[n_cast + 1:2 * n_cast + 1]
    (vv_ref,) = rest[2 * n_cast + 1:]
    nb, heads, s_self, _ = k_ref.shape

    for src, dst in zip(cast_in, cast_out):
        dst[...] = src[...].astype(BF16)

    @pl.when(pl.program_id(1) == 0)
    def _():
        for b in range(nb):
            for hd in range(heads):
                vv_ref[b, hd, :s_self, :V_DIM] = v_ref[b, hd].astype(BF16)
                if has_ctx:
                    vv_ref[b, hd, s_self:, :V_DIM] = cv_ref[b, hd].astype(BF16)
                vv_ref[b, hd, :, V_DIM:] = jnp.ones((vv_ref.shape[2], V_DIM), BF16)

    lam = _lam_value(lam_ref)
    subg = subg_ref[...]
    for b in range(nb):
        for hd in range(heads):
            ckt = ckt_ref[b, hd].astype(BF16) if has_ctx else None
            o = _attend_head(q_ref[b, hd], k_ref[b, hd].astype(BF16), ckt,
                             vv_ref[b, hd], lam, subg)
            o_ref[b, :, hd * V_DIM:(hd + 1) * V_DIM] = o.astype(o_ref.dtype)


def _attention(lamv, subg2, q, k, v, ctx, nb, tq, cast=()):
    b, nh, s, _ = q.shape
    has_ctx = ctx is not None
    sk = s + (ctx[1].shape[2] if has_ctx else 0)
    nt = s // tq
    steps = (b // nb) * nt
    qspec = pl.BlockSpec((nb, nh, tq, V_DIM), lambda i, t: (i, 0, t, 0))
    kvspec = pl.BlockSpec((nb, nh, s, V_DIM), lambda i, t: (i, 0, 0, 0))
    in_specs = [
        pl.BlockSpec((4, HEAD_DIM), lambda i, t: (0, 0)),
        pl.BlockSpec((1, V_DIM), lambda i, t: (0, 0)),
        qspec, kvspec, kvspec,
    ]
    args = [lamv, subg2, q, k, v]
    if has_ctx:
        p = ctx[1].shape[2]
        in_specs += [pl.BlockSpec((nb, nh, V_DIM, p), lambda i, t: (i, 0, 0, 0)),
                     pl.BlockSpec((nb, nh, p, V_DIM), lambda i, t: (i, 0, 0, 0))]
        args += list(ctx)
    cast_specs = [pl.BlockSpec((w.shape[0] // steps, w.shape[1]),
                               lambda i, t: (i * nt + t, 0)) for w in cast]
    outs = pl.pallas_call(
        functools.partial(_attn_kernel, has_ctx=has_ctx),
        out_shape=(jax.ShapeDtypeStruct((b, s, nh * V_DIM), BF16),
                   *(jax.ShapeDtypeStruct(w.shape, BF16) for w in cast)),
        grid=(b // nb, nt),
        in_specs=in_specs + cast_specs,
        out_specs=(pl.BlockSpec((nb, tq, nh * V_DIM), lambda i, t: (i, t, 0)),
                   *cast_specs),
        scratch_shapes=[pltpu.VMEM((nb, nh, sk, 2 * V_DIM), BF16)],
        compiler_params=_params(2),
        name="diff_attn_ctx" if has_ctx else "diff_attn",
    )(*args, *cast)
    return outs


def _shifted_rows(gs_ref, slot, g, seq):
    rows, w = g.shape
    pitch = seq + SUBLANES
    for s in range(rows // seq):
        base = SUBLANES + s * pitch
        gs_ref[slot, base:base + seq, :w] = g[s * seq:(s + 1) * seq]
    shifted = []
    for off in (-1, 1):
        parts = [gs_ref[slot, SUBLANES + s * pitch + off:
                        SUBLANES + s * pitch + off + seq, :w]
                 for s in range(rows // seq)]
        shifted.append(parts[0] if len(parts) == 1 else jnp.concatenate(parts, axis=0))
    return shifted


def _mixffn_kernel(x_ref, yf_ref, o_ref, mod_ref, n2_ref, wo_ref, wg_ref, wu_ref,
                   cw_ref, cb_ref, wd_ref, y_ref, h2_ref, act_ref, gs_ref, *, seq):
    mod = mod_ref[0]
    rows = x_ref.shape[0]

    mix = (jnp.dot(yf_ref[...], wo_ref[:D_FOURIER, :].astype(BF16),
                   preferred_element_type=F32)
           + jnp.dot(o_ref[...], wo_ref[D_FOURIER:, :].astype(BF16),
                     preferred_element_type=F32))
    x1 = x_ref[...] + mod[2:3] * mix
    y_ref[...] = x1
    h2 = _rms_rows(x1, n2_ref[...]) * (1.0 + mod[4:5]) + mod[3:4]
    h2_ref[...] = h2.astype(BF16)
    for slot in range(gs_ref.shape[0]):
        for s in range(rows // seq + 1):
            r0 = s * (seq + SUBLANES)
            gs_ref[slot, r0:r0 + SUBLANES, :] = jnp.zeros(
                (SUBLANES, gs_ref.shape[2]), F32)

    h2 = h2_ref[...]
    cw = cw_ref[...]
    cb = cb_ref[...]
    for n, (c0, w) in enumerate(FF_SUBCHUNKS):
        if w == MXU_COLS:
            g = jnp.dot(h2, wg_ref[:, c0:c0 + w], preferred_element_type=F32)
            u = jnp.dot(h2, wu_ref[:, c0:c0 + w], preferred_element_type=F32)
        else:
            wgu = jnp.concatenate([wg_ref[:, c0:c0 + w], wu_ref[:, c0:c0 + w]], axis=1)
            gu = jnp.dot(h2, wgu, preferred_element_type=F32)
            g, u = gu[:, :w], gu[:, w:]
        g_prev, g_next = _shifted_rows(gs_ref, n % gs_ref.shape[0], g, seq)
        gc = (g * cw[1:2, c0:c0 + w] + g_prev * cw[0:1, c0:c0 + w]
              + g_next * cw[2:3, c0:c0 + w] + cb[:, c0:c0 + w])
        hg = 0.5 * gc
        act_ref[:, c0:c0 + w] = ((hg * u) * (1.0 + jnp.tanh(hg))).astype(BF16)
    y_ref[...] += mod[5:6] * jnp.dot(act_ref[...], wd_ref[...],
                                     preferred_element_type=F32)


def _mixffn(x2, yf2, o2, mod, mod_row, n2, w_out, w_gate, w_up, conv_w, conv_b2,
            w_down, seq):
    m, d = x2.shape
    tm = FFN_ROWS
    row = lambda i: (i, 0)
    const = lambda shape: pl.BlockSpec(shape, lambda i: (0, 0))
    stage_rows = SUBLANES + (tm // seq) * (seq + SUBLANES)
    return pl.pallas_call(
        functools.partial(_mixffn_kernel, seq=seq),
        out_shape=jax.ShapeDtypeStruct((m, d), F32),
        grid=(m // tm,),
        in_specs=[
            pl.BlockSpec((tm, d), row),
            pl.BlockSpec((tm, D_FOURIER), row),
            pl.BlockSpec((tm, D_ATTN), row),
            pl.BlockSpec((1, N_MOD, d), lambda i: (mod_row(i), 0, 0)),
            const((1, d)), const((d, d)), const((d, D_FF)), const((d, D_FF)),
            const((3, D_FF)), const((1, D_FF)), const((D_FF, d)),
        ],
        out_specs=pl.BlockSpec((tm, d), row),
        scratch_shapes=[
            pltpu.VMEM((tm, d), BF16),
            pltpu.VMEM((tm, D_FF), BF16),
            pltpu.VMEM((2, stage_rows, MXU_COLS), F32),
        ],
        compiler_params=_params(1),
        name="mix_ffn",
    )(x2, yf2, o2, mod, n2, w_out, w_gate, w_up, conv_w, conv_b2, w_down)


def _rope_tables(n_tokens):
    rows = n_tokens // GRID_W
    row = np.repeat(np.arange(rows, dtype=np.float64), GRID_W)
    col = np.tile(np.arange(GRID_W, dtype=np.float64), rows)
    n_freq = HEAD_DIM // 4
    inv = ROPE_THETA ** (-np.arange(n_freq, dtype=np.float64) / n_freq)
    ang_r = row[:, None] * inv
    ang_c = col[:, None] * inv
    zero = np.zeros_like(ang_r)
    cos64 = np.concatenate([np.cos(ang_r)] * 2 + [np.cos(ang_c)] * 2, axis=-1)
    sdn64 = np.concatenate([-np.sin(ang_r), zero, -np.sin(ang_c), zero], axis=-1)
    sup64 = np.concatenate([zero, np.sin(ang_r), zero, np.sin(ang_c)], axis=-1)
    two = lambda t: jnp.asarray(np.concatenate([t, t], axis=-1).astype(np.float32))
    return two(cos64), two(sdn64), two(sup64)


def kernel(x_prompt, x_sample, cache_k, cache_v, c, c_ctx, norm1_g, norm2_g, w_ada,
           b_ada, w_in, q_norm_g, k_norm_g, lam_q1, lam_k1, lam_q2, lam_k2, subln_g,
           w_out, w_gate, w_up, conv_w, conv_b, w_down):
    bp, sp, d = x_prompt.shape
    bs, ss, _ = x_sample.shape
    assert FFN_ROWS % sp == 0 and FFN_ROWS == ss

    cond = jnp.concatenate(
        [c_ctx[None, :], c, jnp.zeros((MOD_ROWS - 1 - bs, d), F32)], axis=0)
    mod = _ada(cond, w_ada[0], b_ada[0]).reshape(MOD_ROWS, N_MOD, d)

    n1, n2, subg2 = norm1_g[0][None, :], norm2_g[0][None, :], subln_g[0][None, :]
    qkg = jnp.tile(jnp.concatenate([q_norm_g, k_norm_g], axis=0),
                   (1, MXU_COLS // HEAD_DIM))
    lamv = jnp.stack([lam_q1[0], lam_k1[0], lam_q2[0], lam_k2[0]])
    row_p = lambda i: 0
    row_s = lambda i: i + 1
    p = cache_k.shape[3]
    ck_t = jnp.transpose(cache_k[:, 0], (0, 1, 3, 4, 2)).reshape(bs, N_HEADS, V_DIM, p)
    ctx = (ck_t, cache_v[:, 0])

    yf_p, o_p, k_p, v_p = _inproj(x_prompt, mod, row_p, n1, w_in[0], qkg, None,
                                  FFN_ROWS // sp, F32, attn=(lamv, subg2))
    yf_s, q_s, k_s, v_s = _inproj(x_sample, mod, row_s, n1, w_in[0], qkg,
                                  _rope_tables(ss), FFN_ROWS // ss, BF16)
    o_s, wg_b, wu_b, wd_b = _attention(lamv, subg2, q_s, k_s, v_s, ctx, 1, ATTN_TQ,
                                       cast=(w_gate[0], w_up[0], w_down[0]))

    def mixer(x, yf, o, mod_row):
        b, s, _ = x.shape
        y = _mixffn(x.reshape(b * s, d), yf.reshape(b * s, D_FOURIER),
                    o.reshape(b * s, D_ATTN), mod, mod_row, n2, w_out[0], wg_b, wu_b,
                    conv_w[0], conv_b[0][None, :], wd_b, s)
        return y.reshape(b, s, d)

    y_p = mixer(x_prompt, yf_p, o_p, row_p)
    y_s = mixer(x_sample, yf_s, o_s, row_s)

    new_k = k_p.reshape(bp, 1, N_HEADS, sp, 2, HEAD_DIM)
    new_v = v_p.reshape(bp, 1, N_HEADS, sp, V_DIM)
    return (y_p, y_s, new_k, new_v)
```

```python
import functools
import math

import jax
import jax.numpy as jnp
import numpy as np
from jax import lax
from jax.experimental import pallas as pl
from jax.experimental.pallas import tpu as pltpu

F32 = jnp.float32
BF16 = jnp.bfloat16

D_MODEL = 1024
GRID_W = 64
N_HEADS = 6
HEAD_DIM = 64
V_DIM = 2 * HEAD_DIM
D_FOURIER = 256
FOURIER_HEAD_DIM = 64
D_QK = N_HEADS * V_DIM
D_ATTN = N_HEADS * V_DIM
D_IN = D_FOURIER + 2 * D_QK + D_ATTN
D_FF = 2816
ROPE_THETA = 10000.0
EPS = 1e-6
LAM_INIT = 0.8 - 0.6 * math.exp(-0.3 * 0)
Q_SCALE = HEAD_DIM ** -0.5 * math.log2(math.e)
N_MOD = 6
MOD_ROWS = 16

SUBLANES = 8
MXU_COLS = 256
FF_SUBCHUNKS = tuple((c, min(MXU_COLS, D_FF - c)) for c in range(0, D_FF, MXU_COLS))
FFN_ROWS = 1024
ATTN_TQ = 512
VMEM_LIMIT = 58 * 1024 * 1024


def _params(n_axes):
    return pltpu.CompilerParams(
        dimension_semantics=("arbitrary",) * n_axes,
        vmem_limit_bytes=VMEM_LIMIT,
    )


def _ada_kernel(cond_ref, w_ref, b_ref, o_ref):
    c = cond_ref[...]
    s = c * jax.nn.sigmoid(c)
    o_ref[...] = jnp.dot(s.astype(BF16), w_ref[...].astype(BF16),
                         preferred_element_type=F32) + b_ref[...]


def _ada(cond, w_ada, b_ada):
    n = w_ada.shape[1]
    tn = 1536
    return pl.pallas_call(
        _ada_kernel,
        out_shape=jax.ShapeDtypeStruct((MOD_ROWS, n), F32),
        grid=(n // tn,),
        in_specs=[
            pl.BlockSpec((MOD_ROWS, D_MODEL), lambda j: (0, 0)),
            pl.BlockSpec((D_MODEL, tn), lambda j: (0, j)),
            pl.BlockSpec((1, tn), lambda j: (0, j)),
        ],
        out_specs=pl.BlockSpec((MOD_ROWS, tn), lambda j: (0, j)),
        compiler_params=_params(1),
        name="ada_mod",
    )(cond, w_ada, b_ada.reshape(1, n))


def _rms_rows(x, gain):
    ms = jnp.mean(x * x, axis=-1, keepdims=True)
    return x * lax.rsqrt(ms + EPS) * gain


def _group_rmsnorm(x, gain, ones_blk):
    ss = jnp.dot((x * x).astype(BF16), ones_blk, preferred_element_type=F32)
    return x * lax.rsqrt(ss * (1.0 / HEAD_DIM) + EPS) * gain


def _rope(x, cos, sin_dn, sin_up):
    return (x * cos + pltpu.roll(x, V_DIM - 16, 1) * sin_dn
            + pltpu.roll(x, 16, 1) * sin_up)


def _inproj_kernel(x_ref, mod_ref, n1_ref, w_ref, qkg_ref, ones_ref, cc_ref,
                   sc_ref, cs_ref, ss_ref, *rest, use_rope, attend, kv_dtype):
    if use_rope:
        cos_ref, sdn_ref, sup_ref = rest[:3]
        rest = rest[3:]
    if attend:
        lam_ref, subg_ref, yf_ref, o_ref, k_ref, v_ref = rest
        lam = _lam_value(lam_ref)
        ones_v = jnp.ones((x_ref.shape[1], V_DIM), BF16)
    else:
        yf_ref, q_ref, k_ref, v_ref = rest
    nb, seq, d = x_ref.shape
    x = x_ref[...].reshape(nb * seq, d)
    mod = mod_ref[0]
    h = (_rms_rows(x, n1_ref[...]) * (1.0 + mod[1:2]) + mod[0:1]).astype(BF16)

    def project(c0):
        return jnp.dot(h, w_ref[:, c0:c0 + MXU_COLS].astype(BF16),
                       preferred_element_type=F32)

    xf = project(0).astype(BF16)
    xc = jnp.dot(xf, cc_ref[...], preferred_element_type=F32).astype(BF16)
    xs = jnp.dot(xf, sc_ref[...], preferred_element_type=F32).astype(BF16)
    for b in range(nb):
        r = slice(b * seq, (b + 1) * seq)
        yf = (jnp.dot(cs_ref[...], xc[r], preferred_element_type=F32)
              - jnp.dot(ss_ref[...], xs[r], preferred_element_type=F32))
        yf_ref[b] = yf.astype(yf_ref.dtype)

    def store_heads(out_ref, head0, y, dtype):
        for hh in range(y.shape[1] // V_DIM):
            for b in range(nb):
                out_ref[b, head0 + hh] = y[b * seq:(b + 1) * seq,
                                           hh * V_DIM:(hh + 1) * V_DIM].astype(dtype)

    ones_blk = ones_ref[...]
    for pair in range(N_HEADS // 2):
        c0 = D_FOURIER + pair * MXU_COLS
        zq = project(c0)
        zk = project(c0 + D_QK)
        zv = project(c0 + 2 * D_QK)
        q = _group_rmsnorm(zq, qkg_ref[0:1, :], ones_blk)
        k = _group_rmsnorm(zk, qkg_ref[1:2, :], ones_blk)
        for hh in range(2):
            ls = slice(hh * V_DIM, (hh + 1) * V_DIM)
            qh, kh = q[:, ls], k[:, ls]
            if use_rope:
                qh = _rope(qh, cos_ref[...], sdn_ref[...], sup_ref[...])
                kh = _rope(kh, cos_ref[...], sdn_ref[...], sup_ref[...])
            hd = 2 * pair + hh
            qs = (qh * Q_SCALE).astype(BF16)
            if not attend:
                store_heads(k_ref, hd, kh, kv_dtype)
                store_heads(q_ref, hd, qs, BF16)
                continue
            kb, vb = kh.astype(BF16), zv[:, ls].astype(BF16)
            for b in range(nb):
                r = slice(b * seq, (b + 1) * seq)
                k_ref[b, hd] = kh[r].T.astype(kv_dtype)
                o = _attend_head(qs[r], kb[r], None,
                                 jnp.concatenate([vb[r], ones_v], axis=1), lam,
                                 subg_ref[...])
                o_ref[b, :, hd * V_DIM:(hd + 1) * V_DIM] = o.astype(o_ref.dtype)
        store_heads(v_ref, 2 * pair, zv, kv_dtype)


def _dft_tables(n, scale):
    idx = np.arange(n, dtype=np.int64)
    ang = ((idx[:, None] * idx[None, :]) % n).astype(np.float64) * (2.0 * math.pi / n)
    return ((np.cos(ang) * scale).astype(np.float32),
            (np.sin(ang) * scale).astype(np.float32))


def _fourier_tables(s):
    c64, s64 = _dft_tables(FOURIER_HEAD_DIM, (s * FOURIER_HEAD_DIM) ** -0.5)
    eye = np.eye(D_FOURIER // FOURIER_HEAD_DIM, dtype=np.float32)
    tabs = (np.kron(eye, c64), np.kron(eye, s64)) + _dft_tables(s, 1.0)
    return tuple(jnp.asarray(t).astype(BF16) for t in tabs)


def _inproj(x, mod, mod_row, n1, w_in, qkg, rope, nb, kv_dtype, attn=None):
    b, s, d = x.shape
    use_rope = rope is not None
    attend = attn is not None
    ones_blk = jnp.asarray(np.kron(np.eye(MXU_COLS // HEAD_DIM, dtype=np.float32),
                                   np.ones((HEAD_DIM, HEAD_DIM), np.float32))).astype(BF16)
    const = lambda shape: pl.BlockSpec(shape, lambda i: (0, 0))
    in_specs = [
        pl.BlockSpec((nb, s, d), lambda i: (i, 0, 0)),
        pl.BlockSpec((1, N_MOD, d), lambda i: (mod_row(i), 0, 0)),
        const((1, d)), const((d, D_IN)), const((2, MXU_COLS)),
        const((MXU_COLS, MXU_COLS)), const((D_FOURIER, D_FOURIER)),
        const((D_FOURIER, D_FOURIER)), const((s, s)), const((s, s)),
    ]
    args = [x, mod, n1, w_in, qkg, ones_blk, *_fourier_tables(s)]
    if use_rope:
        assert nb == 1
        in_specs += [const((s, V_DIM))] * 3
        args += list(rope)
    if attend:
        in_specs += [const((4, HEAD_DIM)), const((1, V_DIM))]
        args += list(attn)
    head_spec = pl.BlockSpec((nb, N_HEADS, s, V_DIM), lambda i: (i, 0, 0, 0))
    head_shape = (b, N_HEADS, s, V_DIM)
    if attend:
        second = (jax.ShapeDtypeStruct((b, s, D_ATTN), BF16),
                  pl.BlockSpec((nb, s, D_ATTN), lambda i: (i, 0, 0)))
        keys = (jax.ShapeDtypeStruct((b, N_HEADS, V_DIM, s), kv_dtype),
                pl.BlockSpec((nb, N_HEADS, V_DIM, s), lambda i: (i, 0, 0, 0)))
    else:
        second = (jax.ShapeDtypeStruct(head_shape, BF16), head_spec)
        keys = (jax.ShapeDtypeStruct(head_shape, kv_dtype), head_spec)
    return pl.pallas_call(
        functools.partial(_inproj_kernel, use_rope=use_rope, attend=attend,
                          kv_dtype=kv_dtype),
        out_shape=(
            jax.ShapeDtypeStruct((b, s, D_FOURIER), BF16),
            second[0], keys[0],
            jax.ShapeDtypeStruct(head_shape, kv_dtype),
        ),
        grid=(b // nb,),
        in_specs=in_specs,
        out_specs=(
            pl.BlockSpec((nb, s, D_FOURIER), lambda i: (i, 0, 0)),
            second[1], keys[1], head_spec,
        ),
        compiler_params=_params(1),
        name="inproj_rope" if use_rope else "inproj_attn" if attend else "inproj",
    )(*args)


def _scores(q, k):
    return lax.dot_general(q, k, (((1,), (1,)), ((), ())),
                           preferred_element_type=F32)


def _lam_value(lam_ref):
    lv = lam_ref[...]
    e1 = jnp.exp(jnp.sum(lv[0:1] * lv[1:2], axis=-1, keepdims=True))
    e2 = jnp.exp(jnp.sum(lv[2:3] * lv[3:4], axis=-1, keepdims=True))
    return e1 - e2 + LAM_INIT


def _attend_head(q, kk, ckt, vv, lam, subg):
    lo_mask = lax.broadcasted_iota(jnp.int32, (1, V_DIM), 1) < HEAD_DIM
    zero = jnp.zeros_like(q)
    res = []
    for qc in (jnp.where(lo_mask, q, zero), jnp.where(lo_mask, zero, q)):
        s = _scores(qc, kk)
        if ckt is not None:
            s = jnp.concatenate([s, jnp.dot(qc, ckt, preferred_element_type=F32)],
                                axis=1)
        p = jnp.exp2(s - s.max(axis=-1, keepdims=True)).astype(BF16)
        res.append(jnp.dot(p, vv, preferred_element_type=F32))
    r0, r1 = res
    o = r0[:, :V_DIM] / r0[:, V_DIM:] - lam * (r1[:, :V_DIM] / r1[:, V_DIM:])
    ms = jnp.mean(o * o, axis=-1, keepdims=True)
    return o * lax.rsqrt(ms + EPS) * subg * (1.0 - LAM_INIT)


def _attn_kernel(lam_ref, subg_ref, q_ref, k_ref, v_ref, *rest, has_ctx):
    n_cast = (len(rest) - (4 if has_ctx else 2)) // 2
    if has_ctx:
        ckt_ref, cv_ref = rest[:2]
        rest = rest[2:]
    cast_in = rest[:n_cast]
    o_ref = rest[n_cast]
    cast_out = rest[n_cast + 1:2 * n_cast + 1]
    (vv_ref,) = rest[2 * n_cast + 1:]
    nb, heads, s_self, _ = k_ref.shape

    for src, dst in zip(cast_in, cast_out):
        dst[...] = src[...].astype(BF16)

    @pl.when(pl.program_id(1) == 0)
    def _():
        for b in range(nb):
            for hd in range(heads):
                vv_ref[b, hd, :s_self, :V_DIM] = v_ref[b, hd].astype(BF16)
                if has_ctx:
                    vv_ref[b, hd, s_self:, :V_DIM] = cv_ref[b, hd].astype(BF16)
                vv_ref[b, hd, :, V_DIM:] = jnp.ones((vv_ref.shape[2], V_DIM), BF16)

    lam = _lam_value(lam_ref)
    subg = subg_ref[...]
    for b in range(nb):
        for hd in range(heads):
            ckt = ckt_ref[b, hd].astype(BF16) if has_ctx else None
            o = _attend_head(q_ref[b, hd], k_ref[b, hd].astype(BF16), ckt,
                             vv_ref[b, hd], lam, subg)
            o_ref[b, :, hd * V_DIM:(hd + 1) * V_DIM] = o.astype(o_ref.dtype)


def _attention(lamv, subg2, q, k, v, ctx, nb, tq, cast=()):
    b, nh, s, _ = q.shape
    has_ctx = ctx is not None
    sk = s + (ctx[1].shape[2] if has_ctx else 0)
    nt = s // tq
    steps = (b // nb) * nt
    qspec = pl.BlockSpec((nb, nh, tq, V_DIM), lambda i, t: (i, 0, t, 0))
    kvspec = pl.BlockSpec((nb, nh, s, V_DIM), lambda i, t: (i, 0, 0, 0))
    in_specs = [
        pl.BlockSpec((4, HEAD_DIM), lambda i, t: (0, 0)),
        pl.BlockSpec((1, V_DIM), lambda i, t: (0, 0)),
        qspec, kvspec, kvspec,
    ]
    args = [lamv, subg2, q, k, v]
    if has_ctx:
        p = ctx[1].shape[2]
        in_specs += [pl.BlockSpec((nb, nh, V_DIM, p), lambda i, t: (i, 0, 0, 0)),
                     pl.BlockSpec((nb, nh, p, V_DIM), lambda i, t: (i, 0, 0, 0))]
        args += list(ctx)
    cast_specs = [pl.BlockSpec((w.shape[0] // steps, w.shape[1]),
                               lambda i, t: (i * nt + t, 0)) for w in cast]
    outs = pl.pallas_call(
        functools.partial(_attn_kernel, has_ctx=has_ctx),
        out_shape=(jax.ShapeDtypeStruct((b, s, nh * V_DIM), BF16),
                   *(jax.ShapeDtypeStruct(w.shape, BF16) for w in cast)),
        grid=(b // nb, nt),
        in_specs=in_specs + cast_specs,
        out_specs=(pl.BlockSpec((nb, tq, nh * V_DIM), lambda i, t: (i, t, 0)),
                   *cast_specs),
        scratch_shapes=[pltpu.VMEM((nb, nh, sk, 2 * V_DIM), BF16)],
        compiler_params=_params(2),
        name="diff_attn_ctx" if has_ctx else "diff_attn",
    )(*args, *cast)
    return outs


def _shifted_rows(gs_ref, slot, g, seq):
    rows, w = g.shape
    pitch = seq + SUBLANES
    for s in range(rows // seq):
        base = SUBLANES + s * pitch
        gs_ref[slot, base:base + seq, :w] = g[s * seq:(s + 1) * seq]
    shifted = []
    for off in (-1, 1):
        parts = [gs_ref[slot, SUBLANES + s * pitch + off:
                        SUBLANES + s * pitch + off + seq, :w]
                 for s in range(rows // seq)]
        shifted.append(parts[0] if len(parts) == 1 else jnp.concatenate(parts, axis=0))
    return shifted


def _mixffn_kernel(x_ref, yf_ref, o_ref, mod_ref, n2_ref, wo_ref, wg_ref, wu_ref,
                   cw_ref, cb_ref, wd_ref, y_ref, h2_ref, act_ref, gs_ref, *, seq):
    mod = mod_ref[0]
    rows = x_ref.shape[0]

    mix = (jnp.dot(yf_ref[...], wo_ref[:D_FOURIER, :].astype(BF16),
                   preferred_element_type=F32)
           + jnp.dot(o_ref[...], wo_ref[D_FOURIER:, :].astype(BF16),
                     preferred_element_type=F32))
    x1 = x_ref[...] + mod[2:3] * mix
    y_ref[...] = x1
    h2 = _rms_rows(x1, n2_ref[...]) * (1.0 + mod[4:5]) + mod[3:4]
    h2_ref[...] = h2.astype(BF16)
    for slot in range(gs_ref.shape[0]):
        for s in range(rows // seq + 1):
            r0 = s * (seq + SUBLANES)
            gs_ref[slot, r0:r0 + SUBLANES, :] = jnp.zeros(
                (SUBLANES, gs_ref.shape[2]), F32)

    h2 = h2_ref[...]
    cw = cw_ref[...]
    cb = cb_ref[...]
    for n, (c0, w) in enumerate(FF_SUBCHUNKS):
        if w == MXU_COLS:
            g = jnp.dot(h2, wg_ref[:, c0:c0 + w], preferred_element_type=F32)
            u = jnp.dot(h2, wu_ref[:, c0:c0 + w], preferred_element_type=F32)
        else:
            wgu = jnp.concatenate([wg_ref[:, c0:c0 + w], wu_ref[:, c0:c0 + w]], axis=1)
            gu = jnp.dot(h2, wgu, preferred_element_type=F32)
            g, u = gu[:, :w], gu[:, w:]
        g_prev, g_next = _shifted_rows(gs_ref, n % gs_ref.shape[0], g, seq)
        gc = (g * cw[1:2, c0:c0 + w] + g_prev * cw[0:1, c0:c0 + w]
              + g_next * cw[2:3, c0:c0 + w] + cb[:, c0:c0 + w])
        hg = 0.5 * gc
        act_ref[:, c0:c0 + w] = ((hg * u) * (1.0 + jnp.tanh(hg))).astype(BF16)
    y_ref[...] += mod[5:6] * jnp.dot(act_ref[...], wd_ref[...],
                                     preferred_element_type=F32)


def _mixffn(x2, yf2, o2, mod, mod_row, n2, w_out, w_gate, w_up, conv_w, conv_b2,
            w_down, seq):
    m, d = x2.shape
    tm = FFN_ROWS
    row = lambda i: (i, 0)
    const = lambda shape: pl.BlockSpec(shape, lambda i: (0, 0))
    stage_rows = SUBLANES + (tm // seq) * (seq + SUBLANES)
    return pl.pallas_call(
        functools.partial(_mixffn_kernel, seq=seq),
        out_shape=jax.ShapeDtypeStruct((m, d), F32),
        grid=(m // tm,),
        in_specs=[
            pl.BlockSpec((tm, d), row),
            pl.BlockSpec((tm, D_FOURIER), row),
            pl.BlockSpec((tm, D_ATTN), row),
            pl.BlockSpec((1, N_MOD, d), lambda i: (mod_row(i), 0, 0)),
            const((1, d)), const((d, d)), const((d, D_FF)), const((d, D_FF)),
            const((3, D_FF)), const((1, D_FF)), const((D_FF, d)),
        ],
        out_specs=pl.BlockSpec((tm, d), row),
        scratch_shapes=[
            pltpu.VMEM((tm, d), BF16),
            pltpu.VMEM((tm, D_FF), BF16),
            pltpu.VMEM((2, stage_rows, MXU_COLS), F32),
        ],
        compiler_params=_params(1),
        name="mix_ffn",
    )(x2, yf2, o2, mod, n2, w_out, w_gate, w_up, conv_w, conv_b2, w_down)


def _rope_tables(n_tokens):
    rows = n_tokens // GRID_W
    row = np.repeat(np.arange(rows, dtype=np.float64), GRID_W)
    col = np.tile(np.arange(GRID_W, dtype=np.float64), rows)
    n_freq = HEAD_DIM // 4
    inv = ROPE_THETA ** (-np.arange(n_freq, dtype=np.float64) / n_freq)
    ang_r = row[:, None] * inv
    ang_c = col[:, None] * inv
    zero = np.zeros_like(ang_r)
    cos64 = np.concatenate([np.cos(ang_r)] * 2 + [np.cos(ang_c)] * 2, axis=-1)
    sdn64 = np.concatenate([-np.sin(ang_r), zero, -np.sin(ang_c), zero], axis=-1)
    sup64 = np.concatenate([zero, np.sin(ang_r), zero, np.sin(ang_c)], axis=-1)
    two = lambda t: jnp.asarray(np.concatenate([t, t], axis=-1).astype(np.float32))
    return two(cos64), two(sdn64), two(sup64)


def kernel(x_prompt, x_sample, cache_k, cache_v, c, c_ctx, norm1_g, norm2_g, w_ada,
           b_ada, w_in, q_norm_g, k_norm_g, lam_q1, lam_k1, lam_q2, lam_k2, subln_g,
           w_out, w_gate, w_up, conv_w, conv_b, w_down):
    bp, sp, d = x_prompt.shape
    bs, ss, _ = x_sample.shape
    assert FFN_ROWS % sp == 0 and FFN_ROWS == ss

    cond = jnp.concatenate(
        [c_ctx[None, :], c, jnp.zeros((MOD_ROWS - 1 - bs, d), F32)], axis=0)
    mod = _ada(cond, w_ada[0], b_ada[0]).reshape(MOD_ROWS, N_MOD, d)

    n1, n2, subg2 = norm1_g[0][None, :], norm2_g[0][None, :], subln_g[0][None, :]
    qkg = jnp.tile(jnp.concatenate([q_norm_g, k_norm_g], axis=0),
                   (1, MXU_COLS // HEAD_DIM))
    lamv = jnp.stack([lam_q1[0], lam_k1[0], lam_q2[0], lam_k2[0]])
    row_p = lambda i: 0
    row_s = lambda i: i + 1
    p = cache_k.shape[3]
    ck_t = jnp.transpose(cache_k[:, 0], (0, 1, 3, 4, 2)).reshape(bs, N_HEADS, V_DIM, p)
    ctx = (ck_t, cache_v[:, 0])

    yf_p, o_p, k_p, v_p = _inproj(x_prompt, mod, row_p, n1, w_in[0], qkg, None,
                                  FFN_ROWS // sp, F32, attn=(lamv, subg2))
    yf_s, q_s, k_s, v_s = _inproj(x_sample, mod, row_s, n1, w_in[0], qkg,
                                  _rope_tables(ss), FFN_ROWS // ss, BF16)
    o_s, wg_b, wu_b, wd_b = _attention(lamv, subg2, q_s, k_s, v_s, ctx, 1, ATTN_TQ,
                                       cast=(w_gate[0], w_up[0], w_down[0]))

    def mixer(x, yf, o, mod_row):
        b, s, _ = x.shape
        y = _mixffn(x.reshape(b * s, d), yf.reshape(b * s, D_FOURIER),
                    o.reshape(b * s, D_ATTN), mod, mod_row, n2, w_out[0], wg_b, wu_b,
                    conv_w[0], conv_b[0][None, :], wd_b, s)
        return y.reshape(b, s, d)

    y_p = mixer(x_prompt, yf_p, o_p, row_p)
    y_s = mixer(x_sample, yf_s, o_s, row_s)

    new_k = jnp.transpose(k_p.reshape(bp, 1, N_HEADS, 2, HEAD_DIM, sp),
                          (0, 1, 2, 5, 3, 4))
    new_v = v_p.reshape(bp, 1, N_HEADS, sp, V_DIM)
    return (y_p, y_s, new_k, new_v)
```

```python
import functools
import math

import jax
import jax.numpy as jnp
import numpy as np
from jax import lax
from jax.experimental import pallas as pl
from jax.experimental.pallas import tpu as pltpu

F32 = jnp.float32
BF16 = jnp.bfloat16

D_MODEL = 1024
GRID_W = 64
N_HEADS = 6
HEAD_DIM = 64
V_DIM = 2 * HEAD_DIM
D_FOURIER = 256
FOURIER_HEAD_DIM = 64
D_QK = N_HEADS * V_DIM
D_ATTN = N_HEADS * V_DIM
D_IN = D_FOURIER + 2 * D_QK + D_ATTN
D_FF = 2816
ROPE_THETA = 10000.0
EPS = 1e-6
LAM_INIT = 0.8 - 0.6 * math.exp(-0.3 * 0)
Q_SCALE = HEAD_DIM ** -0.5 * math.log2(math.e)
N_MOD = 6
MOD_ROWS = 16

SUBLANES = 8
MXU_COLS = 256
FF_SUBCHUNKS = tuple((c, min(MXU_COLS, D_FF - c)) for c in range(0, D_FF, MXU_COLS))
FFN_ROWS = 1024
ATTN_TQ = 1024
VMEM_LIMIT = 58 * 1024 * 1024


def _params(n_axes):
    return pltpu.CompilerParams(
        dimension_semantics=("arbitrary",) * n_axes,
        vmem_limit_bytes=VMEM_LIMIT,
    )


def _ada_kernel(cond_ref, w_ref, b_ref, o_ref):
    c = cond_ref[...]
    s = c * jax.nn.sigmoid(c)
    o_ref[0] = jnp.dot(s.astype(BF16), w_ref[...].astype(BF16),
                       preferred_element_type=F32) + b_ref[...]


def _ada(cond, w_ada, b_ada):
    n = w_ada.shape[1]
    return pl.pallas_call(
        _ada_kernel,
        out_shape=jax.ShapeDtypeStruct((N_MOD, MOD_ROWS, D_MODEL), F32),
        grid=(N_MOD,),
        in_specs=[
            pl.BlockSpec((MOD_ROWS, D_MODEL), lambda j: (0, 0)),
            pl.BlockSpec((D_MODEL, D_MODEL), lambda j: (0, j)),
            pl.BlockSpec((1, D_MODEL), lambda j: (0, j)),
        ],
        out_specs=pl.BlockSpec((1, MOD_ROWS, D_MODEL), lambda j: (j, 0, 0)),
        compiler_params=_params(1),
        name="ada_mod",
    )(cond, w_ada, b_ada.reshape(1, n))


def _mod_rows(mod_ref, row):
    return [mod_ref[k, pl.ds(row, 1), :] for k in range(N_MOD)]


def _rms_rows(x, gain):
    ms = jnp.mean(x * x, axis=-1, keepdims=True)
    return x * lax.rsqrt(ms + EPS) * gain


def _group_rmsnorm(x, gain, ones_blk):
    ss = jnp.dot((x * x).astype(BF16), ones_blk, preferred_element_type=F32)
    return x * lax.rsqrt(ss * (1.0 / HEAD_DIM) + EPS) * gain


def _rope(x, cos, sin_dn, sin_up):
    return (x * cos + pltpu.roll(x, V_DIM - 16, 1) * sin_dn
            + pltpu.roll(x, 16, 1) * sin_up)


def _inproj_kernel(x_ref, mod_ref, n1_ref, w_ref, qkg_ref, ones_ref, cc_ref,
                   sc_ref, cs_ref, ss_ref, *rest, use_rope, attend, kv_dtype, mod_row):
    if use_rope:
        cos_ref, sdn_ref, sup_ref = rest[:3]
        rest = rest[3:]
    if attend:
        lam_ref, subg_ref, yf_ref, o_ref, k_ref, v_ref = rest
        lam = _lam_value(lam_ref)
        ones_v = jnp.ones((x_ref.shape[1], V_DIM), BF16)
    else:
        yf_ref, q_ref, k_ref, v_ref = rest
    nb, seq, d = x_ref.shape
    x = x_ref[...].reshape(nb * seq, d)
    mod = _mod_rows(mod_ref, mod_row(pl.program_id(0)))
    h = (_rms_rows(x, n1_ref[...]) * (1.0 + mod[1]) + mod[0]).astype(BF16)

    def project(c0):
        return jnp.dot(h, w_ref[:, c0:c0 + MXU_COLS].astype(BF16),
                       preferred_element_type=F32)

    xf = project(0).astype(BF16)
    xc = jnp.dot(xf, cc_ref[...], preferred_element_type=F32).astype(BF16)
    xs = jnp.dot(xf, sc_ref[...], preferred_element_type=F32).astype(BF16)
    for b in range(nb):
        r = slice(b * seq, (b + 1) * seq)
        yf = (jnp.dot(cs_ref[...], xc[r], preferred_element_type=F32)
              - jnp.dot(ss_ref[...], xs[r], preferred_element_type=F32))
        yf_ref[b] = yf.astype(yf_ref.dtype)

    def store_heads(out_ref, head0, y, dtype):
        for hh in range(y.shape[1] // V_DIM):
            for b in range(nb):
                out_ref[b, head0 + hh] = y[b * seq:(b + 1) * seq,
                                           hh * V_DIM:(hh + 1) * V_DIM].astype(dtype)

    ones_blk = ones_ref[...]
    for pair in range(N_HEADS // 2):
        c0 = D_FOURIER + pair * MXU_COLS
        zq = project(c0)
        zk = project(c0 + D_QK)
        zv = project(c0 + 2 * D_QK)
        q = _group_rmsnorm(zq, qkg_ref[0:1, :], ones_blk)
        k = _group_rmsnorm(zk, qkg_ref[1:2, :], ones_blk)
        for hh in range(2):
            ls = slice(hh * V_DIM, (hh + 1) * V_DIM)
            qh, kh = q[:, ls], k[:, ls]
            if use_rope:
                qh = _rope(qh, cos_ref[...], sdn_ref[...], sup_ref[...])
                kh = _rope(kh, cos_ref[...], sdn_ref[...], sup_ref[...])
            hd = 2 * pair + hh
            qs = (qh * Q_SCALE).astype(BF16)
            if not attend:
                store_heads(k_ref, hd, kh, kv_dtype)
                store_heads(q_ref, hd, qs, BF16)
                continue
            kb, vb = kh.astype(BF16), zv[:, ls].astype(BF16)
            for b in range(nb):
                r = slice(b * seq, (b + 1) * seq)
                k_ref[b, hd] = kh[r].T.astype(kv_dtype)
                o = _attend_head(qs[r], kb[r], None,
                                 jnp.concatenate([vb[r], ones_v], axis=1), lam,
                                 subg_ref[...])
                o_ref[b, :, hd * V_DIM:(hd + 1) * V_DIM] = o.astype(o_ref.dtype)
        store_heads(v_ref, 2 * pair, zv, kv_dtype)


def _dft_tables(n, scale):
    idx = np.arange(n, dtype=np.int64)
    ang = ((idx[:, None] * idx[None, :]) % n).astype(np.float64) * (2.0 * math.pi / n)
    return ((np.cos(ang) * scale).astype(np.float32),
            (np.sin(ang) * scale).astype(np.float32))


def _fourier_tables(s):
    c64, s64 = _dft_tables(FOURIER_HEAD_DIM, (s * FOURIER_HEAD_DIM) ** -0.5)
    eye = np.eye(D_FOURIER // FOURIER_HEAD_DIM, dtype=np.float32)
    tabs = (np.kron(eye, c64), np.kron(eye, s64)) + _dft_tables(s, 1.0)
    return tuple(jnp.asarray(t).astype(BF16) for t in tabs)


def _inproj(x, mod, mod_row, n1, w_in, qkg, rope, nb, kv_dtype, attn=None):
    b, s, d = x.shape
    use_rope = rope is not None
    attend = attn is not None
    ones_blk = jnp.asarray(np.kron(np.eye(MXU_COLS // HEAD_DIM, dtype=np.float32),
                                   np.ones((HEAD_DIM, HEAD_DIM), np.float32))).astype(BF16)
    const = lambda shape: pl.BlockSpec(shape, lambda i: (0, 0))
    in_specs = [
        pl.BlockSpec((nb, s, d), lambda i: (i, 0, 0)),
        pl.BlockSpec((N_MOD, MOD_ROWS, d), lambda i: (0, 0, 0)),
        const((1, d)), const((d, D_IN)), const((2, MXU_COLS)),
        const((MXU_COLS, MXU_COLS)), const((D_FOURIER, D_FOURIER)),
        const((D_FOURIER, D_FOURIER)), const((s, s)), const((s, s)),
    ]
    args = [x, mod, n1, w_in, qkg, ones_blk, *_fourier_tables(s)]
    if use_rope:
        assert nb == 1
        in_specs += [const((s, V_DIM))] * 3
        args += list(rope)
    if attend:
        in_specs += [const((4, HEAD_DIM)), const((1, V_DIM))]
        args += list(attn)
    head_spec = pl.BlockSpec((nb, N_HEADS, s, V_DIM), lambda i: (i, 0, 0, 0))
    head_shape = (b, N_HEADS, s, V_DIM)
    if attend:
        second = (jax.ShapeDtypeStruct((b, s, D_ATTN), BF16),
                  pl.BlockSpec((nb, s, D_ATTN), lambda i: (i, 0, 0)))
        keys = (jax.ShapeDtypeStruct((b, N_HEADS, V_DIM, s), kv_dtype),
                pl.BlockSpec((nb, N_HEADS, V_DIM, s), lambda i: (i, 0, 0, 0)))
    else:
        second = (jax.ShapeDtypeStruct(head_shape, BF16), head_spec)
        keys = (jax.ShapeDtypeStruct(head_shape, kv_dtype), head_spec)
    return pl.pallas_call(
        functools.partial(_inproj_kernel, use_rope=use_rope, attend=attend,
                          kv_dtype=kv_dtype, mod_row=mod_row),
        out_shape=(
            jax.ShapeDtypeStruct((b, s, D_FOURIER), BF16),
            second[0], keys[0],
            jax.ShapeDtypeStruct(head_shape, kv_dtype),
        ),
        grid=(b // nb,),
        in_specs=in_specs,
        out_specs=(
            pl.BlockSpec((nb, s, D_FOURIER), lambda i: (i, 0, 0)),
            second[1], keys[1], head_spec,
        ),
        compiler_params=_params(1),
        name="inproj_rope" if use_rope else "inproj_attn" if attend else "inproj",
    )(*args)


def _scores(q, k):
    return lax.dot_general(q, k, (((1,), (1,)), ((), ())),
                           preferred_element_type=F32)


def _lam_value(lam_ref):
    lv = lam_ref[...]
    e1 = jnp.exp(jnp.sum(lv[0:1] * lv[1:2], axis=-1, keepdims=True))
    e2 = jnp.exp(jnp.sum(lv[2:3] * lv[3:4], axis=-1, keepdims=True))
    return e1 - e2 + LAM_INIT


def _attend_head(q, kk, ckt, vv, lam, subg):
    lo_mask = lax.broadcasted_iota(jnp.int32, (1, V_DIM), 1) < HEAD_DIM
    zero = jnp.zeros_like(q)
    res = []
    for qc in (jnp.where(lo_mask, q, zero), jnp.where(lo_mask, zero, q)):
        s = _scores(qc, kk)
        if ckt is not None:
            s = jnp.concatenate([s, jnp.dot(qc, ckt, preferred_element_type=F32)],
                                axis=1)
        p = jnp.exp2(s - s.max(axis=-1, keepdims=True)).astype(BF16)
        res.append(jnp.dot(p, vv, preferred_element_type=F32))
    r0, r1 = res
    o = r0[:, :V_DIM] / r0[:, V_DIM:] - lam * (r1[:, :V_DIM] / r1[:, V_DIM:])
    ms = jnp.mean(o * o, axis=-1, keepdims=True)
    return o * lax.rsqrt(ms + EPS) * subg * (1.0 - LAM_INIT)


def _attn_kernel(lam_ref, subg_ref, q_ref, k_ref, v_ref, *rest, has_ctx):
    n_cast = (len(rest) - (4 if has_ctx else 2)) // 2
    if has_ctx:
        ckt_ref, cv_ref = rest[:2]
        rest = rest[2:]
    cast_in = rest[:n_cast]
    o_ref = rest[n_cast]
    cast_out = rest[n_cast + 1:2 * n_cast + 1]
    (vv_ref,) = rest[2 * n_cast + 1:]
    nb, heads, s_self, _ = k_ref.shape

    for src, dst in zip(cast_in, cast_out):
        dst[...] = src[...].astype(BF16)

    @pl.when(pl.program_id(1) == 0)
    def _():
        for b in range(nb):
            for hd in range(heads):
                vv_ref[b, hd, :s_self, :V_DIM] = v_ref[b, hd].astype(BF16)
                if has_ctx:
                    vv_ref[b, hd, s_self:, :V_DIM] = cv_ref[b, hd].astype(BF16)
                vv_ref[b, hd, :, V_DIM:] = jnp.ones((vv_ref.shape[2], V_DIM), BF16)

    lam = _lam_value(lam_ref)
    subg = subg_ref[...]
    for b in range(nb):
        for hd in range(heads):
            ckt = ckt_ref[b, hd].astype(BF16) if has_ctx else None
            o = _attend_head(q_ref[b, hd], k_ref[b, hd].astype(BF16), ckt,
                             vv_ref[b, hd], lam, subg)
            o_ref[b, :, hd * V_DIM:(hd + 1) * V_DIM] = o.astype(o_ref.dtype)


def _attention(lamv, subg2, q, k, v, ctx, nb, tq, cast=()):
    b, nh, s, _ = q.shape
    has_ctx = ctx is not None
    sk = s + (ctx[1].shape[2] if has_ctx else 0)
    nt = s // tq
    steps = (b // nb) * nt
    qspec = pl.BlockSpec((nb, nh, tq, V_DIM), lambda i, t: (i, 0, t, 0))
    kvspec = pl.BlockSpec((nb, nh, s, V_DIM), lambda i, t: (i, 0, 0, 0))
    in_specs = [
        pl.BlockSpec((4, HEAD_DIM), lambda i, t: (0, 0)),
        pl.BlockSpec((1, V_DIM), lambda i, t: (0, 0)),
        qspec, kvspec, kvspec,
    ]
    args = [lamv, subg2, q, k, v]
    if has_ctx:
        p = ctx[1].shape[2]
        in_specs += [pl.BlockSpec((nb, nh, V_DIM, p), lambda i, t: (i, 0, 0, 0)),
                     pl.BlockSpec((nb, nh, p, V_DIM), lambda i, t: (i, 0, 0, 0))]
        args += list(ctx)
    cast_specs = [pl.BlockSpec((w.shape[0] // steps, w.shape[1]),
                               lambda i, t: (i * nt + t, 0)) for w in cast]
    outs = pl.pallas_call(
        functools.partial(_attn_kernel, has_ctx=has_ctx),
        out_shape=(jax.ShapeDtypeStruct((b, s, nh * V_DIM), BF16),
                   *(jax.ShapeDtypeStruct(w.shape, BF16) for w in cast)),
        grid=(b // nb, nt),
        in_specs=in_specs + cast_specs,
        out_specs=(pl.BlockSpec((nb, tq, nh * V_DIM), lambda i, t: (i, t, 0)),
                   *cast_specs),
        scratch_shapes=[pltpu.VMEM((nb, nh, sk, 2 * V_DIM), BF16)],
        compiler_params=_params(2),
        name="diff_attn_ctx" if has_ctx else "diff_attn",
    )(*args, *cast)
    return outs


def _shifted_rows(gs_ref, slot, g, seq):
    rows, w = g.shape
    pitch = seq + SUBLANES
    for s in range(rows // seq):
        base = SUBLANES + s * pitch
        gs_ref[slot, base:base + seq, :w] = g[s * seq:(s + 1) * seq]
    shifted = []
    for off in (-1, 1):
        parts = [gs_ref[slot, SUBLANES + s * pitch + off:
                        SUBLANES + s * pitch + off + seq, :w]
                 for s in range(rows // seq)]
        shifted.append(parts[0] if len(parts) == 1 else jnp.concatenate(parts, axis=0))
    return shifted


def _mixffn_kernel(x_ref, yf_ref, o_ref, mod_ref, n2_ref, wo_ref, wg_ref, wu_ref,
                   cw_ref, cb_ref, wd_ref, y_ref, h2_ref, act_ref, gs_ref, *, seq,
                   mod_row):
    mod = _mod_rows(mod_ref, mod_row(pl.program_id(0)))
    rows = x_ref.shape[0]

    mix = (jnp.dot(yf_ref[...], wo_ref[:D_FOURIER, :].astype(BF16),
                   preferred_element_type=F32)
           + jnp.dot(o_ref[...], wo_ref[D_FOURIER:, :].astype(BF16),
                     preferred_element_type=F32))
    x1 = x_ref[...] + mod[2] * mix
    y_ref[...] = x1
    h2 = _rms_rows(x1, n2_ref[...]) * (1.0 + mod[4]) + mod[3]
    h2_ref[...] = h2.astype(BF16)
    for slot in range(gs_ref.shape[0]):
        for s in range(rows // seq + 1):
            r0 = s * (seq + SUBLANES)
            gs_ref[slot, r0:r0 + SUBLANES, :] = jnp.zeros(
                (SUBLANES, gs_ref.shape[2]), F32)

    h2 = h2_ref[...]
    cw = cw_ref[...]
    cb = cb_ref[...]
    for n, (c0, w) in enumerate(FF_SUBCHUNKS):
        if w == MXU_COLS:
            g = jnp.dot(h2, wg_ref[:, c0:c0 + w], preferred_element_type=F32)
            u = jnp.dot(h2, wu_ref[:, c0:c0 + w], preferred_element_type=F32)
        else:
            wgu = jnp.concatenate([wg_ref[:, c0:c0 + w], wu_ref[:, c0:c0 + w]], axis=1)
            gu = jnp.dot(h2, wgu, preferred_element_type=F32)
            g, u = gu[:, :w], gu[:, w:]
        g_prev, g_next = _shifted_rows(gs_ref, n % gs_ref.shape[0], g, seq)
        gc = (g * cw[1:2, c0:c0 + w] + g_prev * cw[0:1, c0:c0 + w]
              + g_next * cw[2:3, c0:c0 + w] + cb[:, c0:c0 + w])
        hg = 0.5 * gc
        act_ref[:, c0:c0 + w] = ((hg * u) * (1.0 + jnp.tanh(hg))).astype(BF16)
    y_ref[...] += mod[5] * jnp.dot(act_ref[...], wd_ref[...],
                                     preferred_element_type=F32)


def _mixffn(x2, yf2, o2, mod, mod_row, n2, w_out, w_gate, w_up, conv_w, conv_b2,
            w_down, seq):
    m, d = x2.shape
    tm = FFN_ROWS
    row = lambda i: (i, 0)
    const = lambda shape: pl.BlockSpec(shape, lambda i: (0, 0))
    stage_rows = SUBLANES + (tm // seq) * (seq + SUBLANES)
    return pl.pallas_call(
        functools.partial(_mixffn_kernel, seq=seq, mod_row=mod_row),
        out_shape=jax.ShapeDtypeStruct((m, d), F32),
        grid=(m // tm,),
        in_specs=[
            pl.BlockSpec((tm, d), row),
            pl.BlockSpec((tm, D_FOURIER), row),
            pl.BlockSpec((tm, D_ATTN), row),
            pl.BlockSpec((N_MOD, MOD_ROWS, d), lambda i: (0, 0, 0)),
            const((1, d)), const((d, d)), const((d, D_FF)), const((d, D_FF)),
            const((3, D_FF)), const((1, D_FF)), const((D_FF, d)),
        ],
        out_specs=pl.BlockSpec((tm, d), row),
        scratch_shapes=[
            pltpu.VMEM((tm, d), BF16),
            pltpu.VMEM((tm, D_FF), BF16),
            pltpu.VMEM((2, stage_rows, MXU_COLS), F32),
        ],
        compiler_params=_params(1),
        name="mix_ffn",
    )(x2, yf2, o2, mod, n2, w_out, w_gate, w_up, conv_w, conv_b2, w_down)


def _rope_tables(n_tokens):
    rows = n_tokens // GRID_W
    row = np.repeat(np.arange(rows, dtype=np.float64), GRID_W)
    col = np.tile(np.arange(GRID_W, dtype=np.float64), rows)
    n_freq = HEAD_DIM // 4
    inv = ROPE_THETA ** (-np.arange(n_freq, dtype=np.float64) / n_freq)
    ang_r = row[:, None] * inv
    ang_c = col[:, None] * inv
    zero = np.zeros_like(ang_r)
    cos64 = np.concatenate([np.cos(ang_r)] * 2 + [np.cos(ang_c)] * 2, axis=-1)
    sdn64 = np.concatenate([-np.sin(ang_r), zero, -np.sin(ang_c), zero], axis=-1)
    sup64 = np.concatenate([zero, np.sin(ang_r), zero, np.sin(ang_c)], axis=-1)
    two = lambda t: jnp.asarray(np.concatenate([t, t], axis=-1).astype(np.float32))
    return two(cos64), two(sdn64), two(sup64)


def kernel(x_prompt, x_sample, cache_k, cache_v, c, c_ctx, norm1_g, norm2_g, w_ada,
           b_ada, w_in, q_norm_g, k_norm_g, lam_q1, lam_k1, lam_q2, lam_k2, subln_g,
           w_out, w_gate, w_up, conv_w, conv_b, w_down):
    bp, sp, d = x_prompt.shape
    bs, ss, _ = x_sample.shape
    assert FFN_ROWS % sp == 0 and FFN_ROWS == ss

    cond = jnp.concatenate(
        [c_ctx[None, :], c, jnp.zeros((MOD_ROWS - 1 - bs, d), F32)], axis=0)
    mod = _ada(cond, w_ada[0], b_ada[0])

    n1, n2, subg2 = norm1_g[0][None, :], norm2_g[0][None, :], subln_g[0][None, :]
    qkg = jnp.tile(jnp.concatenate([q_norm_g, k_norm_g], axis=0),
                   (1, MXU_COLS // HEAD_DIM))
    lamv = jnp.stack([lam_q1[0], lam_k1[0], lam_q2[0], lam_k2[0]])
    row_p = lambda i: 0
    row_s = lambda i: i + 1
    p = cache_k.shape[3]
    ck_t = jnp.transpose(cache_k[:, 0], (0, 1, 3, 4, 2)).reshape(bs, N_HEADS, V_DIM, p)
    ctx = (ck_t, cache_v[:, 0])

    yf_p, o_p, k_p, v_p = _inproj(x_prompt, mod, row_p, n1, w_in[0], qkg, None,
                                  FFN_ROWS // sp, F32, attn=(lamv, subg2))
    yf_s, q_s, k_s, v_s = _inproj(x_sample, mod, row_s, n1, w_in[0], qkg,
                                  _rope_tables(ss), FFN_ROWS // ss, BF16)
    o_s, wg_b, wu_b, wd_b = _attention(lamv, subg2, q_s, k_s, v_s, ctx, 1, ATTN_TQ,
                                       cast=(w_gate[0], w_up[0], w_down[0]))

    def mixer(x, yf, o, mod_row):
        b, s, _ = x.shape
        y = _mixffn(x.reshape(b * s, d), yf.reshape(b * s, D_FOURIER),
                    o.reshape(b * s, D_ATTN), mod, mod_row, n2, w_out[0], wg_b, wu_b,
                    conv_w[0], conv_b[0][None, :], wd_b, s)
        return y.reshape(b, s, d)

    y_p = mixer(x_prompt, yf_p, o_p, row_p)
    y_s = mixer(x_sample, yf_s, o_s, row_s)

    new_k = jnp.transpose(k_p.reshape(bp, 1, N_HEADS, 2, HEAD_DIM, sp),
                          (0, 1, 2, 5, 3, 4))
    new_v = v_p.reshape(bp, 1, N_HEADS, sp, V_DIM)
    return (y_p, y_s, new_k, new_v)
```

```python
import functools
import math

import jax
import jax.numpy as jnp
import numpy as np
from jax import lax
from jax.experimental import pallas as pl
from jax.experimental.pallas import tpu as pltpu

F32 = jnp.float32
BF16 = jnp.bfloat16

D_MODEL = 1024
GRID_W = 64
N_HEADS = 6
HEAD_DIM = 64
V_DIM = 2 * HEAD_DIM
D_FOURIER = 256
FOURIER_HEAD_DIM = 64
D_QK = N_HEADS * V_DIM
D_ATTN = N_HEADS * V_DIM
D_IN = D_FOURIER + 2 * D_QK + D_ATTN
D_FF = 2816
ROPE_THETA = 10000.0
EPS = 1e-6
LAM_INIT = 0.8 - 0.6 * math.exp(-0.3 * 0)
Q_SCALE = HEAD_DIM ** -0.5 * math.log2(math.e)
N_MOD = 6
MOD_ROWS = 16

SUBLANES = 8
MXU_COLS = 256
FF_SUBCHUNKS = tuple((c, min(MXU_COLS, D_FF - c)) for c in range(0, D_FF, MXU_COLS))
FFN_ROWS = 1024
ATTN_TQ = 512
VMEM_LIMIT = 58 * 1024 * 1024


def _params(n_axes):
    return pltpu.CompilerParams(
        dimension_semantics=("arbitrary",) * n_axes,
        vmem_limit_bytes=VMEM_LIMIT,
    )


def _ada_kernel(cond_ref, w_ref, b_ref, o_ref):
    c = cond_ref[...]
    s = c * jax.nn.sigmoid(c)
    o_ref[0] = jnp.dot(s.astype(BF16), w_ref[...].astype(BF16),
                       preferred_element_type=F32) + b_ref[...]


def _ada(cond, w_ada, b_ada):
    n = w_ada.shape[1]
    return pl.pallas_call(
        _ada_kernel,
        out_shape=jax.ShapeDtypeStruct((N_MOD, MOD_ROWS, D_MODEL), F32),
        grid=(N_MOD,),
        in_specs=[
            pl.BlockSpec((MOD_ROWS, D_MODEL), lambda j: (0, 0)),
            pl.BlockSpec((D_MODEL, D_MODEL), lambda j: (0, j)),
            pl.BlockSpec((1, D_MODEL), lambda j: (0, j)),
        ],
        out_specs=pl.BlockSpec((1, MOD_ROWS, D_MODEL), lambda j: (j, 0, 0)),
        compiler_params=_params(1),
        name="ada_mod",
    )(cond, w_ada, b_ada.reshape(1, n))


def _mod_rows(mod_ref, row):
    return [mod_ref[k, pl.ds(row, 1), :] for k in range(N_MOD)]


def _rms_rows(x, gain):
    ms = jnp.mean(x * x, axis=-1, keepdims=True)
    return x * lax.rsqrt(ms + EPS) * gain


def _group_rmsnorm(x, gain, ones_blk):
    ss = jnp.dot((x * x).astype(BF16), ones_blk, preferred_element_type=F32)
    return x * lax.rsqrt(ss * (1.0 / HEAD_DIM) + EPS) * gain


def _rope(x, cos, sin_dn, sin_up):
    return (x * cos + pltpu.roll(x, V_DIM - 16, 1) * sin_dn
            + pltpu.roll(x, 16, 1) * sin_up)


def _inproj_kernel(x_ref, mod_ref, n1_ref, w_ref, qkg_ref, ones_ref, cc_ref,
                   sc_ref, cs_ref, ss_ref, *rest, use_rope, attend, kv_dtype, mod_row):
    if use_rope:
        cos_ref, sdn_ref, sup_ref = rest[:3]
        rest = rest[3:]
    if attend:
        lam_ref, subg_ref, yf_ref, o_ref, k_ref, v_ref = rest
        lam = _lam_value(lam_ref)
        ones_v = jnp.ones((x_ref.shape[1], V_DIM), BF16)
    else:
        yf_ref, q_ref, k_ref, v_ref = rest
    nb, seq, d = x_ref.shape
    x = x_ref[...].reshape(nb * seq, d)
    mod = _mod_rows(mod_ref, mod_row(pl.program_id(0)))
    h = (_rms_rows(x, n1_ref[...]) * (1.0 + mod[1]) + mod[0]).astype(BF16)

    def project(c0):
        return jnp.dot(h, w_ref[:, c0:c0 + MXU_COLS].astype(BF16),
                       preferred_element_type=F32)

    xf = project(0).astype(BF16)
    xc = jnp.dot(xf, cc_ref[...], preferred_element_type=F32).astype(BF16)
    xs = jnp.dot(xf, sc_ref[...], preferred_element_type=F32).astype(BF16)
    for b in range(nb):
        r = slice(b * seq, (b + 1) * seq)
        yf = (jnp.dot(cs_ref[...], xc[r], preferred_element_type=F32)
              - jnp.dot(ss_ref[...], xs[r], preferred_element_type=F32))
        yf_ref[b] = yf.astype(yf_ref.dtype)

    def store_heads(out_ref, head0, y, dtype):
        for hh in range(y.shape[1] // V_DIM):
            for b in range(nb):
                out_ref[b, head0 + hh] = y[b * seq:(b + 1) * seq,
                                           hh * V_DIM:(hh + 1) * V_DIM].astype(dtype)

    ones_blk = ones_ref[...]
    for pair in range(N_HEADS // 2):
        c0 = D_FOURIER + pair * MXU_COLS
        zq = project(c0)
        zk = project(c0 + D_QK)
        zv = project(c0 + 2 * D_QK)
        q = _group_rmsnorm(zq, qkg_ref[0:1, :], ones_blk)
        k = _group_rmsnorm(zk, qkg_ref[1:2, :], ones_blk)
        for hh in range(2):
            ls = slice(hh * V_DIM, (hh + 1) * V_DIM)
            qh, kh = q[:, ls], k[:, ls]
            if use_rope:
                qh = _rope(qh, cos_ref[...], sdn_ref[...], sup_ref[...])
                kh = _rope(kh, cos_ref[...], sdn_ref[...], sup_ref[...])
            hd = 2 * pair + hh
            qs = (qh * Q_SCALE).astype(BF16)
            if not attend:
                store_heads(k_ref, hd, kh, kv_dtype)
                store_heads(q_ref, hd, qs, BF16)
                continue
            kb, vb = kh.astype(BF16), zv[:, ls].astype(BF16)
            for b in range(nb):
                r = slice(b * seq, (b + 1) * seq)
                k_ref[b, hd] = kh[r].T.astype(kv_dtype)
                o = _attend_head(qs[r], kb[r], None,
                                 jnp.concatenate([vb[r], ones_v], axis=1), lam,
                                 subg_ref[...])
                o_ref[b, :, hd * V_DIM:(hd + 1) * V_DIM] = o.astype(o_ref.dtype)
        store_heads(v_ref, 2 * pair, zv, kv_dtype)


def _dft_tables(n, scale):
    idx = np.arange(n, dtype=np.int64)
    ang = ((idx[:, None] * idx[None, :]) % n).astype(np.float64) * (2.0 * math.pi / n)
    return ((np.cos(ang) * scale).astype(np.float32),
            (np.sin(ang) * scale).astype(np.float32))


def _fourier_tables(s):
    c64, s64 = _dft_tables(FOURIER_HEAD_DIM, (s * FOURIER_HEAD_DIM) ** -0.5)
    eye = np.eye(D_FOURIER // FOURIER_HEAD_DIM, dtype=np.float32)
    tabs = (np.kron(eye, c64), np.kron(eye, s64)) + _dft_tables(s, 1.0)
    return tuple(jnp.asarray(t).astype(BF16) for t in tabs)


def _inproj(x, mod, mod_row, n1, w_in, qkg, rope, nb, kv_dtype, attn=None):
    b, s, d = x.shape
    use_rope = rope is not None
    attend = attn is not None
    ones_blk = jnp.asarray(np.kron(np.eye(MXU_COLS // HEAD_DIM, dtype=np.float32),
                                   np.ones((HEAD_DIM, HEAD_DIM), np.float32))).astype(BF16)
    const = lambda shape: pl.BlockSpec(shape, lambda i: (0, 0))
    in_specs = [
        pl.BlockSpec((nb, s, d), lambda i: (i, 0, 0)),
        pl.BlockSpec((N_MOD, MOD_ROWS, d), lambda i: (0, 0, 0)),
        const((1, d)), const((d, D_IN)), const((2, MXU_COLS)),
        const((MXU_COLS, MXU_COLS)), const((D_FOURIER, D_FOURIER)),
        const((D_FOURIER, D_FOURIER)), const((s, s)), const((s, s)),
    ]
    args = [x, mod, n1, w_in, qkg, ones_blk, *_fourier_tables(s)]
    if use_rope:
        assert nb == 1
        in_specs += [const((s, V_DIM))] * 3
        args += list(rope)
    if attend:
        in_specs += [const((4, HEAD_DIM)), const((1, V_DIM))]
        args += list(attn)
    head_spec = pl.BlockSpec((nb, N_HEADS, s, V_DIM), lambda i: (i, 0, 0, 0))
    head_shape = (b, N_HEADS, s, V_DIM)
    if attend:
        second = (jax.ShapeDtypeStruct((b, s, D_ATTN), BF16),
                  pl.BlockSpec((nb, s, D_ATTN), lambda i: (i, 0, 0)))
        keys = (jax.ShapeDtypeStruct((b, N_HEADS, V_DIM, s), kv_dtype),
                pl.BlockSpec((nb, N_HEADS, V_DIM, s), lambda i: (i, 0, 0, 0)))
    else:
        second = (jax.ShapeDtypeStruct(head_shape, BF16), head_spec)
        keys = (jax.ShapeDtypeStruct(head_shape, kv_dtype), head_spec)
    return pl.pallas_call(
        functools.partial(_inproj_kernel, use_rope=use_rope, attend=attend,
                          kv_dtype=kv_dtype, mod_row=mod_row),
        out_shape=(
            jax.ShapeDtypeStruct((b, s, D_FOURIER), BF16),
            second[0], keys[0],
            jax.ShapeDtypeStruct(head_shape, kv_dtype),
        ),
        grid=(b // nb,),
        in_specs=in_specs,
        out_specs=(
            pl.BlockSpec((nb, s, D_FOURIER), lambda i: (i, 0, 0)),
            second[1], keys[1], head_spec,
        ),
        compiler_params=_params(1),
        name="inproj_rope" if use_rope else "inproj_attn" if attend else "inproj",
    )(*args)


def _scores(q, k):
    return lax.dot_general(q, k, (((1,), (1,)), ((), ())),
                           preferred_element_type=F32)


def _lam_value(lam_ref):
    lv = lam_ref[...]
    e1 = jnp.exp(jnp.sum(lv[0:1] * lv[1:2], axis=-1, keepdims=True))
    e2 = jnp.exp(jnp.sum(lv[2:3] * lv[3:4], axis=-1, keepdims=True))
    return e1 - e2 + LAM_INIT


def _attend_head(q, kk, ckt, vv, lam, subg):
    lo_mask = lax.broadcasted_iota(jnp.int32, (1, V_DIM), 1) < HEAD_DIM
    zero = jnp.zeros_like(q)
    res = []
    for qc in (jnp.where(lo_mask, q, zero), jnp.where(lo_mask, zero, q)):
        s = _scores(qc, kk)
        if ckt is not None:
            s = jnp.concatenate([s, jnp.dot(qc, ckt, preferred_element_type=F32)],
                                axis=1)
        p = jnp.exp2(s - s.max(axis=-1, keepdims=True)).astype(BF16)
        res.append(jnp.dot(p, vv, preferred_element_type=F32))
    r0, r1 = res
    o = r0[:, :V_DIM] / r0[:, V_DIM:] - lam * (r1[:, :V_DIM] / r1[:, V_DIM:])
    ms = jnp.mean(o * o, axis=-1, keepdims=True)
    return o * lax.rsqrt(ms + EPS) * subg * (1.0 - LAM_INIT)


def _attn_kernel(lam_ref, subg_ref, q_ref, k_ref, v_ref, *rest, has_ctx):
    n_cast = (len(rest) - (4 if has_ctx else 2)) // 2
    if has_ctx:
        ckt_ref, cv_ref = rest[:2]
        rest = rest[2:]
    cast_in = rest[:n_cast]
    o_ref = rest[n_cast]
    cast_out = rest[n_cast + 1:2 * n_cast + 1]
    (vv_ref,) = rest[2 * n_cast + 1:]
    nb, heads, s_self, _ = k_ref.shape

    for src, dst in zip(cast_in, cast_out):
        dst[...] = src[...].astype(BF16)

    @pl.when(pl.program_id(1) == 0)
    def _():
        for b in range(nb):
            for hd in range(heads):
                vv_ref[b, hd, :s_self, :V_DIM] = v_ref[b, hd].astype(BF16)
                if has_ctx:
                    vv_ref[b, hd, s_self:, :V_DIM] = cv_ref[b, hd].astype(BF16)
                vv_ref[b, hd, :, V_DIM:] = jnp.ones((vv_ref.shape[2], V_DIM), BF16)

    lam = _lam_value(lam_ref)
    subg = subg_ref[...]
    for b in range(nb):
        for hd in range(heads):
            ckt = ckt_ref[b, hd].astype(BF16) if has_ctx else None
            o = _attend_head(q_ref[b, hd], k_ref[b, hd].astype(BF16), ckt,
                             vv_ref[b, hd], lam, subg)
            o_ref[b, :, hd * V_DIM:(hd + 1) * V_DIM] = o.astype(o_ref.dtype)


def _attention(lamv, subg2, q, k, v, ctx, nb, tq, cast=()):
    b, nh, s, _ = q.shape
    has_ctx = ctx is not None
    sk = s + (ctx[1].shape[2] if has_ctx else 0)
    nt = s // tq
    steps = (b // nb) * nt
    qspec = pl.BlockSpec((nb, nh, tq, V_DIM), lambda i, t: (i, 0, t, 0))
    kvspec = pl.BlockSpec((nb, nh, s, V_DIM), lambda i, t: (i, 0, 0, 0))
    in_specs = [
        pl.BlockSpec((4, HEAD_DIM), lambda i, t: (0, 0)),
        pl.BlockSpec((1, V_DIM), lambda i, t: (0, 0)),
        qspec, kvspec, kvspec,
    ]
    args = [lamv, subg2, q, k, v]
    if has_ctx:
        p = ctx[1].shape[2]
        in_specs += [pl.BlockSpec((nb, nh, V_DIM, p), lambda i, t: (i, 0, 0, 0)),
                     pl.BlockSpec((nb, nh, p, V_DIM), lambda i, t: (i, 0, 0, 0))]
        args += list(ctx)
    cast_specs = [pl.BlockSpec((w.shape[0] // steps, w.shape[1]),
                               lambda i, t: (i * nt + t, 0)) for w in cast]
    outs = pl.pallas_call(
        functools.partial(_attn_kernel, has_ctx=has_ctx),
        out_shape=(jax.ShapeDtypeStruct((b, s, nh * V_DIM), BF16),
                   *(jax.ShapeDtypeStruct(w.shape, BF16) for w in cast)),
        grid=(b // nb, nt),
        in_specs=in_specs + cast_specs,
        out_specs=(pl.BlockSpec((nb, tq, nh * V_DIM), lambda i, t: (i, t, 0)),
                   *cast_specs),
        scratch_shapes=[pltpu.VMEM((nb, nh, sk, 2 * V_DIM), BF16)],
        compiler_params=_params(2),
        name="diff_attn_ctx" if has_ctx else "diff_attn",
    )(*args, *cast)
    return outs


def _shifted_rows(gs_ref, slot, g, seq):
    rows, w = g.shape
    pitch = seq + SUBLANES
    for s in range(rows // seq):
        base = SUBLANES + s * pitch
        gs_ref[slot, base:base + seq, :w] = g[s * seq:(s + 1) * seq]
    shifted = []
    for off in (-1, 1):
        parts = [gs_ref[slot, SUBLANES + s * pitch + off:
                        SUBLANES + s * pitch + off + seq, :w]
                 for s in range(rows // seq)]
        shifted.append(parts[0] if len(parts) == 1 else jnp.concatenate(parts, axis=0))
    return shifted


def _mixffn_kernel(x_ref, yf_ref, o_ref, mod_ref, n2_ref, wo_ref, wg_ref, wu_ref,
                   cw_ref, cb_ref, wd_ref, y_ref, h2_ref, act_ref, gs_ref, *, seq,
                   mod_row):
    mod = _mod_rows(mod_ref, mod_row(pl.program_id(0)))
    rows = x_ref.shape[0]

    mix = (jnp.dot(yf_ref[...], wo_ref[:D_FOURIER, :].astype(BF16),
                   preferred_element_type=F32)
           + jnp.dot(o_ref[...], wo_ref[D_FOURIER:, :].astype(BF16),
                     preferred_element_type=F32))
    x1 = x_ref[...] + mod[2] * mix
    y_ref[...] = x1
    h2 = _rms_rows(x1, n2_ref[...]) * (1.0 + mod[4]) + mod[3]
    h2_ref[...] = h2.astype(BF16)
    for slot in range(gs_ref.shape[0]):
        for s in range(rows // seq + 1):
            r0 = s * (seq + SUBLANES)
            gs_ref[slot, r0:r0 + SUBLANES, :] = jnp.zeros(
                (SUBLANES, gs_ref.shape[2]), F32)

    h2 = h2_ref[...]
    cw = cw_ref[...]
    cb = cb_ref[...]
    for n, (c0, w) in enumerate(FF_SUBCHUNKS):
        if w == MXU_COLS:
            g = jnp.dot(h2, wg_ref[:, c0:c0 + w], preferred_element_type=F32)
            u = jnp.dot(h2, wu_ref[:, c0:c0 + w], preferred_element_type=F32)
        else:
            wgu = jnp.concatenate([wg_ref[:, c0:c0 + w], wu_ref[:, c0:c0 + w]], axis=1)
            gu = jnp.dot(h2, wgu, preferred_element_type=F32)
            g, u = gu[:, :w], gu[:, w:]
        g_prev, g_next = _shifted_rows(gs_ref, n % gs_ref.shape[0], g, seq)
        gc = (g * cw[1:2, c0:c0 + w] + g_prev * cw[0:1, c0:c0 + w]
              + g_next * cw[2:3, c0:c0 + w] + cb[:, c0:c0 + w])
        hg = 0.5 * gc
        act_ref[:, c0:c0 + w] = ((hg * u) * (1.0 + jnp.tanh(hg))).astype(BF16)
    y_ref[...] += mod[5] * jnp.dot(act_ref[...], wd_ref[...],
                                     preferred_element_type=F32)


def _mixffn(x2, yf2, o2, mod, mod_row, n2, w_out, w_gate, w_up, conv_w, conv_b2,
            w_down, seq):
    m, d = x2.shape
    tm = FFN_ROWS
    row = lambda i: (i, 0)
    const = lambda shape: pl.BlockSpec(shape, lambda i: (0, 0))
    stage_rows = SUBLANES + (tm // seq) * (seq + SUBLANES)
    return pl.pallas_call(
        functools.partial(_mixffn_kernel, seq=seq, mod_row=mod_row),
        out_shape=jax.ShapeDtypeStruct((m, d), F32),
        grid=(m // tm,),
        in_specs=[
            pl.BlockSpec((tm, d), row),
            pl.BlockSpec((tm, D_FOURIER), row),
            pl.BlockSpec((tm, D_ATTN), row),
            pl.BlockSpec((N_MOD, MOD_ROWS, d), lambda i: (0, 0, 0)),
            const((1, d)), const((d, d)), const((d, D_FF)), const((d, D_FF)),
            const((3, D_FF)), const((1, D_FF)), const((D_FF, d)),
        ],
        out_specs=pl.BlockSpec((tm, d), row),
        scratch_shapes=[
            pltpu.VMEM((tm, d), BF16),
            pltpu.VMEM((tm, D_FF), BF16),
            pltpu.VMEM((2, stage_rows, MXU_COLS), F32),
        ],
        compiler_params=_params(1),
        name="mix_ffn",
    )(x2, yf2, o2, mod, n2, w_out, w_gate, w_up, conv_w, conv_b2, w_down)


def _rope_tables(n_tokens):
    rows = n_tokens // GRID_W
    row = np.repeat(np.arange(rows, dtype=np.float64), GRID_W)
    col = np.tile(np.arange(GRID_W, dtype=np.float64), rows)
    n_freq = HEAD_DIM // 4
    inv = ROPE_THETA ** (-np.arange(n_freq, dtype=np.float64) / n_freq)
    ang_r = row[:, None] * inv
    ang_c = col[:, None] * inv
    zero = np.zeros_like(ang_r)
    cos64 = np.concatenate([np.cos(ang_r)] * 2 + [np.cos(ang_c)] * 2, axis=-1)
    sdn64 = np.concatenate([-np.sin(ang_r), zero, -np.sin(ang_c), zero], axis=-1)
    sup64 = np.concatenate([zero, np.sin(ang_r), zero, np.sin(ang_c)], axis=-1)
    two = lambda t: jnp.asarray(np.concatenate([t, t], axis=-1).astype(np.float32))
    return two(cos64), two(sdn64), two(sup64)


def kernel(x_prompt, x_sample, cache_k, cache_v, c, c_ctx, norm1_g, norm2_g, w_ada,
           b_ada, w_in, q_norm_g, k_norm_g, lam_q1, lam_k1, lam_q2, lam_k2, subln_g,
           w_out, w_gate, w_up, conv_w, conv_b, w_down):
    bp, sp, d = x_prompt.shape
    bs, ss, _ = x_sample.shape
    assert FFN_ROWS % sp == 0 and FFN_ROWS == ss

    cond = jnp.concatenate(
        [c_ctx[None, :], c, jnp.zeros((MOD_ROWS - 1 - bs, d), F32)], axis=0)
    mod = _ada(cond, w_ada[0], b_ada[0])

    n1, n2, subg2 = norm1_g[0][None, :], norm2_g[0][None, :], subln_g[0][None, :]
    qkg = jnp.tile(jnp.concatenate([q_norm_g, k_norm_g], axis=0),
                   (1, MXU_COLS // HEAD_DIM))
    lamv = jnp.stack([lam_q1[0], lam_k1[0], lam_q2[0], lam_k2[0]])
    row_p = lambda i: 0
    row_s = lambda i: i + 1
    p = cache_k.shape[3]
    ck_t = jnp.transpose(cache_k[:, 0], (0, 1, 3, 4, 2)).reshape(bs, N_HEADS, V_DIM, p)
    ctx = (ck_t, cache_v[:, 0])

    yf_p, o_p, k_p, v_p = _inproj(x_prompt, mod, row_p, n1, w_in[0], qkg, None,
                                  FFN_ROWS // sp, F32, attn=(lamv, subg2))
    yf_s, q_s, k_s, v_s = _inproj(x_sample, mod, row_s, n1, w_in[0], qkg,
                                  _rope_tables(ss), FFN_ROWS // ss, BF16)
    o_s, wg_b, wu_b, wd_b = _attention(lamv, subg2, q_s, k_s, v_s, ctx, 1, ATTN_TQ,
                                       cast=(w_gate[0], w_up[0], w_down[0]))

    def mixer(x, yf, o, mod_row):
        b, s, _ = x.shape
        y = _mixffn(x.reshape(b * s, d), yf.reshape(b * s, D_FOURIER),
                    o.reshape(b * s, D_ATTN), mod, mod_row, n2, w_out[0], wg_b, wu_b,
                    conv_w[0], conv_b[0][None, :], wd_b, s)
        return y.reshape(b, s, d)

    y_p = mixer(x_prompt, yf_p, o_p, row_p)
    y_s = mixer(x_sample, yf_s, o_s, row_s)

    new_k = jnp.transpose(k_p.reshape(bp, 1, N_HEADS, 2, HEAD_DIM, sp),
                          (0, 1, 2, 5, 3, 4))
    new_v = v_p.reshape(bp, 1, N_HEADS, sp, V_DIM)
    return (y_p, y_s, new_k, new_v)
```

```python
import functools
import math

import jax
import jax.numpy as jnp
import numpy as np
from jax import lax
from jax.experimental import pallas as pl
from jax.experimental.pallas import tpu as pltpu

F32 = jnp.float32
BF16 = jnp.bfloat16

D_MODEL = 1024
GRID_W = 64
N_HEADS = 6
HEAD_DIM = 64
V_DIM = 2 * HEAD_DIM
D_FOURIER = 256
FOURIER_HEAD_DIM = 64
D_QK = N_HEADS * V_DIM
D_ATTN = N_HEADS * V_DIM
D_IN = D_FOURIER + 2 * D_QK + D_ATTN
D_FF = 2816
ROPE_THETA = 10000.0
EPS = 1e-6
LAM_INIT = 0.8 - 0.6 * math.exp(-0.3 * 0)
Q_SCALE = HEAD_DIM ** -0.5 * math.log2(math.e)
N_MOD = 6
MOD_ROWS = 16
MOD_CTX_ROW = 0
MOD_REQ_ROW0 = 8
ADA_K_ROWS = 256

SUBLANES = 8
MXU_COLS = 256
FF_SUBCHUNKS = tuple((c, min(MXU_COLS, D_FF - c)) for c in range(0, D_FF, MXU_COLS))
FFN_ROWS = 1024
ATTN_TQ = 512
VMEM_LIMIT = 58 * 1024 * 1024


def _params(n_axes):
    return pltpu.CompilerParams(
        dimension_semantics=("arbitrary",) * n_axes,
        vmem_limit_bytes=VMEM_LIMIT,
    )


def _ada_kernel(ctx_ref, c_ref, w_ref, b_ref, o_ref, acc_ref):
    j = pl.program_id(0)
    c = jnp.concatenate(
        [jnp.broadcast_to(ctx_ref[...], (SUBLANES, ctx_ref.shape[1])), c_ref[...]], axis=0)
    s = c * jax.nn.sigmoid(c)
    part = jnp.dot(s.astype(BF16), w_ref[...].astype(BF16), preferred_element_type=F32)

    @pl.when(j == 0)
    def _():
        acc_ref[...] = part + b_ref[...]

    @pl.when(j > 0)
    def _():
        acc_ref[...] += part

    @pl.when(j == pl.num_programs(0) - 1)
    def _():
        for k in range(N_MOD):
            o_ref[k] = acc_ref[:, k * D_MODEL:(k + 1) * D_MODEL]


def _ada(c_ctx, c, w_ada, b_ada):
    d, n = w_ada.shape
    tk = ADA_K_ROWS
    assert c.shape[0] == MOD_ROWS - SUBLANES
    return pl.pallas_call(
        _ada_kernel,
        out_shape=jax.ShapeDtypeStruct((N_MOD, MOD_ROWS, D_MODEL), F32),
        grid=(d // tk,),
        in_specs=[
            pl.BlockSpec((1, tk), lambda j: (0, j)),
            pl.BlockSpec((c.shape[0], tk), lambda j: (0, j)),
            pl.BlockSpec((tk, n), lambda j: (j, 0)),
            pl.BlockSpec((1, n), lambda j: (0, 0)),
        ],
        out_specs=pl.BlockSpec((N_MOD, MOD_ROWS, D_MODEL), lambda j: (0, 0, 0)),
        scratch_shapes=[pltpu.VMEM((MOD_ROWS, n), F32)],
        compiler_params=_params(1),
        name="ada_mod",
    )(c_ctx.reshape(1, d), c, w_ada, b_ada.reshape(1, n))


def _mod_rows(mod_ref, row):
    return [mod_ref[k, pl.ds(row, 1), :] for k in range(N_MOD)]


def _rms_rows(x, gain):
    ms = jnp.mean(x * x, axis=-1, keepdims=True)
    return x * lax.rsqrt(ms + EPS) * gain


def _group_rmsnorm(x, gain, ones_blk):
    ss = jnp.dot((x * x).astype(BF16), ones_blk, preferred_element_type=F32)
    return x * lax.rsqrt(ss * (1.0 / HEAD_DIM) + EPS) * gain


def _rope(x, cos, sin_dn, sin_up):
    return (x * cos + pltpu.roll(x, V_DIM - 16, 1) * sin_dn
            + pltpu.roll(x, 16, 1) * sin_up)


def _inproj_kernel(x_ref, mod_ref, n1_ref, w_ref, qkg_ref, ones_ref, cc_ref,
                   sc_ref, cs_ref, ss_ref, *rest, use_rope, attend, kv_dtype, mod_row):
    if use_rope:
        cos_ref, sdn_ref, sup_ref = rest[:3]
        rest = rest[3:]
    if attend:
        lam_ref, subg_ref, yf_ref, o_ref, k_ref, v_ref = rest
        lam = _lam_value(lam_ref)
        ones_v = jnp.ones((x_ref.shape[1], V_DIM), BF16)
    else:
        yf_ref, q_ref, k_ref, v_ref = rest
    nb, seq, d = x_ref.shape
    x = x_ref[...].reshape(nb * seq, d)
    mod = _mod_rows(mod_ref, mod_row(pl.program_id(0)))
    h = (_rms_rows(x, n1_ref[...]) * (1.0 + mod[1]) + mod[0]).astype(BF16)

    def project(c0):
        return jnp.dot(h, w_ref[:, c0:c0 + MXU_COLS].astype(BF16),
                       preferred_element_type=F32)

    xf = project(0).astype(BF16)
    xc = jnp.dot(xf, cc_ref[...], preferred_element_type=F32).astype(BF16)
    xs = jnp.dot(xf, sc_ref[...], preferred_element_type=F32).astype(BF16)
    for b in range(nb):
        r = slice(b * seq, (b + 1) * seq)
        yf = (jnp.dot(cs_ref[...], xc[r], preferred_element_type=F32)
              - jnp.dot(ss_ref[...], xs[r], preferred_element_type=F32))
        yf_ref[b] = yf.astype(yf_ref.dtype)

    def store_heads(out_ref, head0, y, dtype):
        for hh in range(y.shape[1] // V_DIM):
            for b in range(nb):
                out_ref[b, head0 + hh] = y[b * seq:(b + 1) * seq,
                                           hh * V_DIM:(hh + 1) * V_DIM].astype(dtype)

    ones_blk = ones_ref[...]
    for pair in range(N_HEADS // 2):
        c0 = D_FOURIER + pair * MXU_COLS
        zq = project(c0)
        zk = project(c0 + D_QK)
        zv = project(c0 + 2 * D_QK)
        q = _group_rmsnorm(zq, qkg_ref[0:1, :], ones_blk)
        k = _group_rmsnorm(zk, qkg_ref[1:2, :], ones_blk)
        for hh in range(2):
            ls = slice(hh * V_DIM, (hh + 1) * V_DIM)
            qh, kh = q[:, ls], k[:, ls]
            if use_rope:
                qh = _rope(qh, cos_ref[...], sdn_ref[...], sup_ref[...])
                kh = _rope(kh, cos_ref[...], sdn_ref[...], sup_ref[...])
            hd = 2 * pair + hh
            qs = (qh * Q_SCALE).astype(BF16)
            if not attend:
                store_heads(k_ref, hd, kh, kv_dtype)
                store_heads(q_ref, hd, qs, BF16)
                continue
            kb, vb = kh.astype(BF16), zv[:, ls].astype(BF16)
            for b in range(nb):
                r = slice(b * seq, (b + 1) * seq)
                k_ref[b, hd] = kh[r].T.astype(kv_dtype)
                o = _attend_head(qs[r], kb[r], None,
                                 jnp.concatenate([vb[r], ones_v], axis=1), lam,
                                 subg_ref[...])
                o_ref[b, :, hd * V_DIM:(hd + 1) * V_DIM] = o.astype(o_ref.dtype)
        store_heads(v_ref, 2 * pair, zv, kv_dtype)


def _dft_tables(n, scale):
    idx = np.arange(n, dtype=np.int64)
    ang = ((idx[:, None] * idx[None, :]) % n).astype(np.float64) * (2.0 * math.pi / n)
    return ((np.cos(ang) * scale).astype(np.float32),
            (np.sin(ang) * scale).astype(np.float32))


def _fourier_tables(s):
    c64, s64 = _dft_tables(FOURIER_HEAD_DIM, (s * FOURIER_HEAD_DIM) ** -0.5)
    eye = np.eye(D_FOURIER // FOURIER_HEAD_DIM, dtype=np.float32)
    tabs = (np.kron(eye, c64), np.kron(eye, s64)) + _dft_tables(s, 1.0)
    return tuple(jnp.asarray(t).astype(BF16) for t in tabs)


def _inproj(x, mod, mod_row, n1, w_in, qkg, rope, nb, kv_dtype, attn=None):
    b, s, d = x.shape
    use_rope = rope is not None
    attend = attn is not None
    ones_blk = jnp.asarray(np.kron(np.eye(MXU_COLS // HEAD_DIM, dtype=np.float32),
                                   np.ones((HEAD_DIM, HEAD_DIM), np.float32))).astype(BF16)
    const = lambda shape: pl.BlockSpec(shape, lambda i: (0, 0))
    in_specs = [
        pl.BlockSpec((nb, s, d), lambda i: (i, 0, 0)),
        pl.BlockSpec((N_MOD, MOD_ROWS, d), lambda i: (0, 0, 0)),
        const((1, d)), const((d, D_IN)), const((2, MXU_COLS)),
        const((MXU_COLS, MXU_COLS)), const((D_FOURIER, D_FOURIER)),
        const((D_FOURIER, D_FOURIER)), const((s, s)), const((s, s)),
    ]
    args = [x, mod, n1, w_in, qkg, ones_blk, *_fourier_tables(s)]
    if use_rope:
        assert nb == 1
        in_specs += [const((s, V_DIM))] * 3
        args += list(rope)
    if attend:
        in_specs += [const((4, HEAD_DIM)), const((1, V_DIM))]
        args += list(attn)
    head_spec = pl.BlockSpec((nb, N_HEADS, s, V_DIM), lambda i: (i, 0, 0, 0))
    head_shape = (b, N_HEADS, s, V_DIM)
    if attend:
        second = (jax.ShapeDtypeStruct((b, s, D_ATTN), BF16),
                  pl.BlockSpec((nb, s, D_ATTN), lambda i: (i, 0, 0)))
        keys = (jax.ShapeDtypeStruct((b, N_HEADS, V_DIM, s), kv_dtype),
                pl.BlockSpec((nb, N_HEADS, V_DIM, s), lambda i: (i, 0, 0, 0)))
    else:
        second = (jax.ShapeDtypeStruct(head_shape, BF16), head_spec)
        keys = (jax.ShapeDtypeStruct(head_shape, kv_dtype), head_spec)
    return pl.pallas_call(
        functools.partial(_inproj_kernel, use_rope=use_rope, attend=attend,
                          kv_dtype=kv_dtype, mod_row=mod_row),
        out_shape=(
            jax.ShapeDtypeStruct((b, s, D_FOURIER), BF16),
            second[0], keys[0],
            jax.ShapeDtypeStruct(head_shape, kv_dtype),
        ),
        grid=(b // nb,),
        in_specs=in_specs,
        out_specs=(
            pl.BlockSpec((nb, s, D_FOURIER), lambda i: (i, 0, 0)),
            second[1], keys[1], head_spec,
        ),
        compiler_params=_params(1),
        name="inproj_rope" if use_rope else "inproj_attn" if attend else "inproj",
    )(*args)


def _scores(q, k):
    return lax.dot_general(q, k, (((1,), (1,)), ((), ())),
                           preferred_element_type=F32)


def _lam_value(lam_ref):
    lv = lam_ref[...]
    e1 = jnp.exp(jnp.sum(lv[0:1] * lv[1:2], axis=-1, keepdims=True))
    e2 = jnp.exp(jnp.sum(lv[2:3] * lv[3:4], axis=-1, keepdims=True))
    return e1 - e2 + LAM_INIT


def _attend_head(q, kk, ckt, vv, lam, subg):
    lo_mask = lax.broadcasted_iota(jnp.int32, (1, V_DIM), 1) < HEAD_DIM
    zero = jnp.zeros_like(q)
    res = []
    for qc in (jnp.where(lo_mask, q, zero), jnp.where(lo_mask, zero, q)):
        s = _scores(qc, kk)
        if ckt is not None:
            s = jnp.concatenate([s, jnp.dot(qc, ckt, preferred_element_type=F32)],
                                axis=1)
        p = jnp.exp2(s - s.max(axis=-1, keepdims=True)).astype(BF16)
        res.append(jnp.dot(p, vv, preferred_element_type=F32))
    r0, r1 = res
    o = r0[:, :V_DIM] / r0[:, V_DIM:] - lam * (r1[:, :V_DIM] / r1[:, V_DIM:])
    ms = jnp.mean(o * o, axis=-1, keepdims=True)
    return o * lax.rsqrt(ms + EPS) * subg * (1.0 - LAM_INIT)


def _attn_kernel(lam_ref, subg_ref, q_ref, k_ref, v_ref, *rest, has_ctx):
    n_cast = (len(rest) - (4 if has_ctx else 2)) // 2
    if has_ctx:
        ckt_ref, cv_ref = rest[:2]
        rest = rest[2:]
    cast_in = rest[:n_cast]
    o_ref = rest[n_cast]
    cast_out = rest[n_cast + 1:2 * n_cast + 1]
    (vv_ref,) = rest[2 * n_cast + 1:]
    nb, heads, s_self, _ = k_ref.shape

    for src, dst in zip(cast_in, cast_out):
        dst[...] = src[...].astype(BF16)

    @pl.when(pl.program_id(1) == 0)
    def _():
        for b in range(nb):
            for hd in range(heads):
                vv_ref[b, hd, :s_self, :V_DIM] = v_ref[b, hd].astype(BF16)
                if has_ctx:
                    vv_ref[b, hd, s_self:, :V_DIM] = cv_ref[b, hd].astype(BF16)
                vv_ref[b, hd, :, V_DIM:] = jnp.ones((vv_ref.shape[2], V_DIM), BF16)

    lam = _lam_value(lam_ref)
    subg = subg_ref[...]
    for b in range(nb):
        for hd in range(heads):
            ckt = ckt_ref[b, hd].astype(BF16) if has_ctx else None
            o = _attend_head(q_ref[b, hd], k_ref[b, hd].astype(BF16), ckt,
                             vv_ref[b, hd], lam, subg)
            o_ref[b, :, hd * V_DIM:(hd + 1) * V_DIM] = o.astype(o_ref.dtype)


def _attention(lamv, subg2, q, k, v, ctx, nb, tq, cast=()):
    b, nh, s, _ = q.shape
    has_ctx = ctx is not None
    sk = s + (ctx[1].shape[2] if has_ctx else 0)
    nt = s // tq
    steps = (b // nb) * nt
    qspec = pl.BlockSpec((nb, nh, tq, V_DIM), lambda i, t: (i, 0, t, 0))
    kvspec = pl.BlockSpec((nb, nh, s, V_DIM), lambda i, t: (i, 0, 0, 0))
    in_specs = [
        pl.BlockSpec((4, HEAD_DIM), lambda i, t: (0, 0)),
        pl.BlockSpec((1, V_DIM), lambda i, t: (0, 0)),
        qspec, kvspec, kvspec,
    ]
    args = [lamv, subg2, q, k, v]
    if has_ctx:
        p = ctx[1].shape[2]
        in_specs += [pl.BlockSpec((nb, nh, V_DIM, p), lambda i, t: (i, 0, 0, 0)),
                     pl.BlockSpec((nb, nh, p, V_DIM), lambda i, t: (i, 0, 0, 0))]
        args += list(ctx)
    cast_specs = [pl.BlockSpec((w.shape[0] // steps, w.shape[1]),
                               lambda i, t: (i * nt + t, 0)) for w in cast]
    outs = pl.pallas_call(
        functools.partial(_attn_kernel, has_ctx=has_ctx),
        out_shape=(jax.ShapeDtypeStruct((b, s, nh * V_DIM), BF16),
                   *(jax.ShapeDtypeStruct(w.shape, BF16) for w in cast)),
        grid=(b // nb, nt),
        in_specs=in_specs + cast_specs,
        out_specs=(pl.BlockSpec((nb, tq, nh * V_DIM), lambda i, t: (i, t, 0)),
                   *cast_specs),
        scratch_shapes=[pltpu.VMEM((nb, nh, sk, 2 * V_DIM), BF16)],
        compiler_params=_params(2),
        name="diff_attn_ctx" if has_ctx else "diff_attn",
    )(*args, *cast)
    return outs


def _shifted_rows(gs_ref, slot, g, seq):
    rows, w = g.shape
    pitch = seq + SUBLANES
    for s in range(rows // seq):
        base = SUBLANES + s * pitch
        gs_ref[slot, base:base + seq, :w] = g[s * seq:(s + 1) * seq]
    shifted = []
    for off in (-1, 1):
        parts = [gs_ref[slot, SUBLANES + s * pitch + off:
                        SUBLANES + s * pitch + off + seq, :w]
                 for s in range(rows // seq)]
        shifted.append(parts[0] if len(parts) == 1 else jnp.concatenate(parts, axis=0))
    return shifted


def _mixffn_kernel(x_ref, yf_ref, o_ref, mod_ref, n2_ref, wo_ref, wg_ref, wu_ref,
                   cw_ref, cb_ref, wd_ref, y_ref, h2_ref, act_ref, gs_ref, *, seq,
                   mod_row):
    mod = _mod_rows(mod_ref, mod_row(pl.program_id(0)))
    rows = x_ref.shape[0]

    mix = (jnp.dot(yf_ref[...], wo_ref[:D_FOURIER, :].astype(BF16),
                   preferred_element_type=F32)
           + jnp.dot(o_ref[...], wo_ref[D_FOURIER:, :].astype(BF16),
                     preferred_element_type=F32))
    x1 = x_ref[...] + mod[2] * mix
    y_ref[...] = x1
    h2 = _rms_rows(x1, n2_ref[...]) * (1.0 + mod[4]) + mod[3]
    h2_ref[...] = h2.astype(BF16)
    for slot in range(gs_ref.shape[0]):
        for s in range(rows // seq + 1):
            r0 = s * (seq + SUBLANES)
            gs_ref[slot, r0:r0 + SUBLANES, :] = jnp.zeros(
                (SUBLANES, gs_ref.shape[2]), F32)

    h2 = h2_ref[...]
    cw = cw_ref[...]
    cb = cb_ref[...]
    for n, (c0, w) in enumerate(FF_SUBCHUNKS):
        if w == MXU_COLS:
            g = jnp.dot(h2, wg_ref[:, c0:c0 + w], preferred_element_type=F32)
            u = jnp.dot(h2, wu_ref[:, c0:c0 + w], preferred_element_type=F32)
        else:
            wgu = jnp.concatenate([wg_ref[:, c0:c0 + w], wu_ref[:, c0:c0 + w]], axis=1)
            gu = jnp.dot(h2, wgu, preferred_element_type=F32)
            g, u = gu[:, :w], gu[:, w:]
        g_prev, g_next = _shifted_rows(gs_ref, n % gs_ref.shape[0], g, seq)
        gc = (g * cw[1:2, c0:c0 + w] + g_prev * cw[0:1, c0:c0 + w]
              + g_next * cw[2:3, c0:c0 + w] + cb[:, c0:c0 + w])
        hg = 0.5 * gc
        act_ref[:, c0:c0 + w] = ((hg * u) * (1.0 + jnp.tanh(hg))).astype(BF16)
    y_ref[...] += mod[5] * jnp.dot(act_ref[...], wd_ref[...],
                                     preferred_element_type=F32)


def _mixffn(x2, yf2, o2, mod, mod_row, n2, w_out, w_gate, w_up, conv_w, conv_b2,
            w_down, seq):
    m, d = x2.shape
    tm = FFN_ROWS
    row = lambda i: (i, 0)
    const = lambda shape: pl.BlockSpec(shape, lambda i: (0, 0))
    stage_rows = SUBLANES + (tm // seq) * (seq + SUBLANES)
    return pl.pallas_call(
        functools.partial(_mixffn_kernel, seq=seq, mod_row=mod_row),
        out_shape=jax.ShapeDtypeStruct((m, d), F32),
        grid=(m // tm,),
        in_specs=[
            pl.BlockSpec((tm, d), row),
            pl.BlockSpec((tm, D_FOURIER), row),
            pl.BlockSpec((tm, D_ATTN), row),
            pl.BlockSpec((N_MOD, MOD_ROWS, d), lambda i: (0, 0, 0)),
            const((1, d)), const((d, d)), const((d, D_FF)), const((d, D_FF)),
            const((3, D_FF)), const((1, D_FF)), const((D_FF, d)),
        ],
        out_specs=pl.BlockSpec((tm, d), row),
        scratch_shapes=[
            pltpu.VMEM((tm, d), BF16),
            pltpu.VMEM((tm, D_FF), BF16),
            pltpu.VMEM((2, stage_rows, MXU_COLS), F32),
        ],
        compiler_params=_params(1),
        name="mix_ffn",
    )(x2, yf2, o2, mod, n2, w_out, w_gate, w_up, conv_w, conv_b2, w_down)


def _rope_tables(n_tokens):
    rows = n_tokens // GRID_W
    row = np.repeat(np.arange(rows, dtype=np.float64), GRID_W)
    col = np.tile(np.arange(GRID_W, dtype=np.float64), rows)
    n_freq = HEAD_DIM // 4
    inv = ROPE_THETA ** (-np.arange(n_freq, dtype=np.float64) / n_freq)
    ang_r = row[:, None] * inv
    ang_c = col[:, None] * inv
    zero = np.zeros_like(ang_r)
    cos64 = np.concatenate([np.cos(ang_r)] * 2 + [np.cos(ang_c)] * 2, axis=-1)
    sdn64 = np.concatenate([-np.sin(ang_r), zero, -np.sin(ang_c), zero], axis=-1)
    sup64 = np.concatenate([zero, np.sin(ang_r), zero, np.sin(ang_c)], axis=-1)
    two = lambda t: jnp.asarray(np.concatenate([t, t], axis=-1).astype(np.float32))
    return two(cos64), two(sdn64), two(sup64)


def kernel(x_prompt, x_sample, cache_k, cache_v, c, c_ctx, norm1_g, norm2_g, w_ada,
           b_ada, w_in, q_norm_g, k_norm_g, lam_q1, lam_k1, lam_q2, lam_k2, subln_g,
           w_out, w_gate, w_up, conv_w, conv_b, w_down):
    bp, sp, d = x_prompt.shape
    bs, ss, _ = x_sample.shape
    assert FFN_ROWS % sp == 0 and FFN_ROWS == ss

    mod = _ada(c_ctx, c, w_ada[0], b_ada[0])

    n1, n2, subg2 = norm1_g[0][None, :], norm2_g[0][None, :], subln_g[0][None, :]
    qkg = jnp.tile(jnp.concatenate([q_norm_g, k_norm_g], axis=0),
                   (1, MXU_COLS // HEAD_DIM))
    lamv = jnp.stack([lam_q1[0], lam_k1[0], lam_q2[0], lam_k2[0]])
    row_p = lambda i: MOD_CTX_ROW
    row_s = lambda i: MOD_REQ_ROW0 + i
    p = cache_k.shape[3]
    ck_t = jnp.transpose(cache_k[:, 0], (0, 1, 3, 4, 2)).reshape(bs, N_HEADS, V_DIM, p)
    ctx = (ck_t, cache_v[:, 0])

    yf_p, o_p, k_p, v_p = _inproj(x_prompt, mod, row_p, n1, w_in[0], qkg, None,
                                  FFN_ROWS // sp, F32, attn=(lamv, subg2))
    yf_s, q_s, k_s, v_s = _inproj(x_sample, mod, row_s, n1, w_in[0], qkg,
                                  _rope_tables(ss), FFN_ROWS // ss, BF16)
    o_s, wg_b, wu_b, wd_b = _attention(lamv, subg2, q_s, k_s, v_s, ctx, 1, ATTN_TQ,
                                       cast=(w_gate[0], w_up[0], w_down[0]))

    def mixer(x, yf, o, mod_row):
        b, s, _ = x.shape
        y = _mixffn(x.reshape(b * s, d), yf.reshape(b * s, D_FOURIER),
                    o.reshape(b * s, D_ATTN), mod, mod_row, n2, w_out[0], wg_b, wu_b,
                    conv_w[0], conv_b[0][None, :], wd_b, s)
        return y.reshape(b, s, d)

    y_p = mixer(x_prompt, yf_p, o_p, row_p)
    y_s = mixer(x_sample, yf_s, o_s, row_s)

    new_k = jnp.transpose(k_p.reshape(bp, 1, N_HEADS, 2, HEAD_DIM, sp),
                          (0, 1, 2, 5, 3, 4))
    new_v = v_p.reshape(bp, 1, N_HEADS, sp, V_DIM)
    return (y_p, y_s, new_k, new_v)
```

```python
import functools
import math

import jax
import jax.numpy as jnp
import numpy as np
from jax import lax
from jax.experimental import pallas as pl
from jax.experimental.pallas import tpu as pltpu

F32 = jnp.float32
BF16 = jnp.bfloat16

D_MODEL = 1024
GRID_W = 64
N_HEADS = 6
HEAD_DIM = 64
V_DIM = 2 * HEAD_DIM
D_FOURIER = 256
FOURIER_HEAD_DIM = 64
D_QK = N_HEADS * V_DIM
D_ATTN = N_HEADS * V_DIM
D_IN = D_FOURIER + 2 * D_QK + D_ATTN
D_FF = 2816
ROPE_THETA = 10000.0
EPS = 1e-6
LAM_INIT = 0.8 - 0.6 * math.exp(-0.3 * 0)
Q_SCALE = HEAD_DIM ** -0.5 * math.log2(math.e)
N_MOD = 6
MOD_ROWS = 16
MOD_CTX_ROW = 0
MOD_REQ_ROW0 = 8
ADA_K_ROWS = 256

SUBLANES = 8
MXU_COLS = 256
FF_SUBCHUNKS = tuple((c, min(MXU_COLS, D_FF - c)) for c in range(0, D_FF, MXU_COLS))
FFN_ROWS = 1024
ATTN_TQ = 512
VMEM_LIMIT = 58 * 1024 * 1024


def _params(n_axes):
    return pltpu.CompilerParams(
        dimension_semantics=("arbitrary",) * n_axes,
        vmem_limit_bytes=VMEM_LIMIT,
    )


def _ada_kernel(ctx_ref, c_ref, w_ref, b_ref, o_ref, acc_ref):
    j = pl.program_id(0)
    c = jnp.concatenate(
        [jnp.broadcast_to(ctx_ref[...], (SUBLANES, ctx_ref.shape[1])), c_ref[...]], axis=0)
    s = c * jax.nn.sigmoid(c)
    part = jnp.dot(s.astype(BF16), w_ref[...].astype(BF16), preferred_element_type=F32)

    @pl.when(j == 0)
    def _():
        acc_ref[...] = part + b_ref[...]

    @pl.when(j > 0)
    def _():
        acc_ref[...] += part

    @pl.when(j == pl.num_programs(0) - 1)
    def _():
        for k in range(N_MOD):
            o_ref[k] = acc_ref[:, k * D_MODEL:(k + 1) * D_MODEL]


def _ada(c_ctx, c, w_ada, b_ada):
    d, n = w_ada.shape
    tk = ADA_K_ROWS
    assert c.shape[0] == MOD_ROWS - SUBLANES
    return pl.pallas_call(
        _ada_kernel,
        out_shape=jax.ShapeDtypeStruct((N_MOD, MOD_ROWS, D_MODEL), F32),
        grid=(d // tk,),
        in_specs=[
            pl.BlockSpec((1, tk), lambda j: (0, j)),
            pl.BlockSpec((c.shape[0], tk), lambda j: (0, j)),
            pl.BlockSpec((tk, n), lambda j: (j, 0)),
            pl.BlockSpec((1, n), lambda j: (0, 0)),
        ],
        out_specs=pl.BlockSpec((N_MOD, MOD_ROWS, D_MODEL), lambda j: (0, 0, 0)),
        scratch_shapes=[pltpu.VMEM((MOD_ROWS, n), F32)],
        compiler_params=_params(1),
        name="ada_mod",
    )(c_ctx.reshape(1, d), c, w_ada, b_ada.reshape(1, n))


def _mod_rows(mod_ref, row):
    return [mod_ref[k, pl.ds(row, 1), :] for k in range(N_MOD)]


def _rms_rows(x, gain):
    ms = jnp.mean(x * x, axis=-1, keepdims=True)
    return x * lax.rsqrt(ms + EPS) * gain


def _group_rmsnorm(x, gain, ones_blk):
    ss = jnp.dot((x * x).astype(BF16), ones_blk, preferred_element_type=F32)
    return x * lax.rsqrt(ss * (1.0 / HEAD_DIM) + EPS) * gain


def _rope(x, cos, sin_dn, sin_up):
    return (x * cos + pltpu.roll(x, V_DIM - 16, 1) * sin_dn
            + pltpu.roll(x, 16, 1) * sin_up)


def _inproj_kernel(x_ref, mod_ref, n1_ref, w_ref, qkg_ref, ones_ref, cc_ref,
                   sc_ref, cs_ref, ss_ref, *rest, use_rope, attend, kv_dtype, mod_row):
    if use_rope:
        cos_ref, sdn_ref, sup_ref = rest[:3]
        rest = rest[3:]
    if attend:
        lam_ref, subg_ref, yf_ref, o_ref, k_ref, v_ref = rest
        lam = _lam_value(lam_ref)
        ones_v = jnp.ones((x_ref.shape[1], V_DIM), BF16)
    else:
        yf_ref, q_ref, k_ref, v_ref = rest
    nb, seq, d = x_ref.shape
    x = x_ref[...].reshape(nb * seq, d)
    mod = _mod_rows(mod_ref, mod_row(pl.program_id(0)))
    h = (_rms_rows(x, n1_ref[...]) * (1.0 + mod[1]) + mod[0]).astype(BF16)

    def project(c0):
        return jnp.dot(h, w_ref[:, c0:c0 + MXU_COLS].astype(BF16),
                       preferred_element_type=F32)

    xf = project(0).astype(BF16)
    xc = jnp.dot(xf, cc_ref[...], preferred_element_type=F32).astype(BF16)
    xs = jnp.dot(xf, sc_ref[...], preferred_element_type=F32).astype(BF16)
    for b in range(nb):
        r = slice(b * seq, (b + 1) * seq)
        yf = (jnp.dot(cs_ref[...], xc[r], preferred_element_type=F32)
              - jnp.dot(ss_ref[...], xs[r], preferred_element_type=F32))
        yf_ref[b] = yf.astype(yf_ref.dtype)

    def store_heads(out_ref, head0, y, dtype):
        for hh in range(y.shape[1] // V_DIM):
            for b in range(nb):
                out_ref[b, head0 + hh] = y[b * seq:(b + 1) * seq,
                                           hh * V_DIM:(hh + 1) * V_DIM].astype(dtype)

    ones_blk = ones_ref[...]
    def project_pair(pair):
        c0 = D_FOURIER + pair * MXU_COLS
        return project(c0), project(c0 + D_QK), project(c0 + 2 * D_QK)

    n_pairs = N_HEADS // 2
    lookahead = 1 if attend else 0
    done = [project_pair(p) for p in range(lookahead)]
    for pair in range(n_pairs):
        if pair + lookahead < n_pairs:
            done.append(project_pair(pair + lookahead))
        zq, zk, zv = done[pair]
        q = _group_rmsnorm(zq, qkg_ref[0:1, :], ones_blk)
        k = _group_rmsnorm(zk, qkg_ref[1:2, :], ones_blk)
        for hh in range(2):
            ls = slice(hh * V_DIM, (hh + 1) * V_DIM)
            qh, kh = q[:, ls], k[:, ls]
            if use_rope:
                qh = _rope(qh, cos_ref[...], sdn_ref[...], sup_ref[...])
                kh = _rope(kh, cos_ref[...], sdn_ref[...], sup_ref[...])
            hd = 2 * pair + hh
            qs = (qh * Q_SCALE).astype(BF16)
            if not attend:
                store_heads(k_ref, hd, kh, kv_dtype)
                store_heads(q_ref, hd, qs, BF16)
                continue
            kb, vb = kh.astype(BF16), zv[:, ls].astype(BF16)
            for b in range(nb):
                r = slice(b * seq, (b + 1) * seq)
                k_ref[b, hd] = kh[r].T.astype(kv_dtype)
                o = _attend_head(qs[r], kb[r], None,
                                 jnp.concatenate([vb[r], ones_v], axis=1), lam,
                                 subg_ref[...])
                o_ref[b, :, hd * V_DIM:(hd + 1) * V_DIM] = o.astype(o_ref.dtype)
        store_heads(v_ref, 2 * pair, zv, kv_dtype)


def _dft_tables(n, scale):
    idx = np.arange(n, dtype=np.int64)
    ang = ((idx[:, None] * idx[None, :]) % n).astype(np.float64) * (2.0 * math.pi / n)
    return ((np.cos(ang) * scale).astype(np.float32),
            (np.sin(ang) * scale).astype(np.float32))


def _fourier_tables(s):
    c64, s64 = _dft_tables(FOURIER_HEAD_DIM, (s * FOURIER_HEAD_DIM) ** -0.5)
    eye = np.eye(D_FOURIER // FOURIER_HEAD_DIM, dtype=np.float32)
    tabs = (np.kron(eye, c64), np.kron(eye, s64)) + _dft_tables(s, 1.0)
    return tuple(jnp.asarray(t).astype(BF16) for t in tabs)


def _inproj(x, mod, mod_row, n1, w_in, qkg, rope, nb, kv_dtype, attn=None):
    b, s, d = x.shape
    use_rope = rope is not None
    attend = attn is not None
    ones_blk = jnp.asarray(np.kron(np.eye(MXU_COLS // HEAD_DIM, dtype=np.float32),
                                   np.ones((HEAD_DIM, HEAD_DIM), np.float32))).astype(BF16)
    const = lambda shape: pl.BlockSpec(shape, lambda i: (0, 0))
    in_specs = [
        pl.BlockSpec((nb, s, d), lambda i: (i, 0, 0)),
        pl.BlockSpec((N_MOD, MOD_ROWS, d), lambda i: (0, 0, 0)),
        const((1, d)), const((d, D_IN)), const((2, MXU_COLS)),
        const((MXU_COLS, MXU_COLS)), const((D_FOURIER, D_FOURIER)),
        const((D_FOURIER, D_FOURIER)), const((s, s)), const((s, s)),
    ]
    args = [x, mod, n1, w_in, qkg, ones_blk, *_fourier_tables(s)]
    if use_rope:
        assert nb == 1
        in_specs += [const((s, V_DIM))] * 3
        args += list(rope)
    if attend:
        in_specs += [const((4, HEAD_DIM)), const((1, V_DIM))]
        args += list(attn)
    head_spec = pl.BlockSpec((nb, N_HEADS, s, V_DIM), lambda i: (i, 0, 0, 0))
    head_shape = (b, N_HEADS, s, V_DIM)
    if attend:
        second = (jax.ShapeDtypeStruct((b, s, D_ATTN), BF16),
                  pl.BlockSpec((nb, s, D_ATTN), lambda i: (i, 0, 0)))
        keys = (jax.ShapeDtypeStruct((b, N_HEADS, V_DIM, s), kv_dtype),
                pl.BlockSpec((nb, N_HEADS, V_DIM, s), lambda i: (i, 0, 0, 0)))
    else:
        second = (jax.ShapeDtypeStruct(head_shape, BF16), head_spec)
        keys = (jax.ShapeDtypeStruct(head_shape, kv_dtype), head_spec)
    return pl.pallas_call(
        functools.partial(_inproj_kernel, use_rope=use_rope, attend=attend,
                          kv_dtype=kv_dtype, mod_row=mod_row),
        out_shape=(
            jax.ShapeDtypeStruct((b, s, D_FOURIER), BF16),
            second[0], keys[0],
            jax.ShapeDtypeStruct(head_shape, kv_dtype),
        ),
        grid=(b // nb,),
        in_specs=in_specs,
        out_specs=(
            pl.BlockSpec((nb, s, D_FOURIER), lambda i: (i, 0, 0)),
            second[1], keys[1], head_spec,
        ),
        compiler_params=_params(1),
        name="inproj_rope" if use_rope else "inproj_attn" if attend else "inproj",
    )(*args)


def _scores(q, k):
    return lax.dot_general(q, k, (((1,), (1,)), ((), ())),
                           preferred_element_type=F32)


def _lam_value(lam_ref):
    lv = lam_ref[...]
    e1 = jnp.exp(jnp.sum(lv[0:1] * lv[1:2], axis=-1, keepdims=True))
    e2 = jnp.exp(jnp.sum(lv[2:3] * lv[3:4], axis=-1, keepdims=True))
    return e1 - e2 + LAM_INIT


def _attend_head(q, kk, ckt, vv, lam, subg):
    lo_mask = lax.broadcasted_iota(jnp.int32, (1, V_DIM), 1) < HEAD_DIM
    zero = jnp.zeros_like(q)
    res = []
    for qc in (jnp.where(lo_mask, q, zero), jnp.where(lo_mask, zero, q)):
        s = _scores(qc, kk)
        if ckt is not None:
            s = jnp.concatenate([s, jnp.dot(qc, ckt, preferred_element_type=F32)],
                                axis=1)
        p = jnp.exp2(s - s.max(axis=-1, keepdims=True)).astype(BF16)
        res.append(jnp.dot(p, vv, preferred_element_type=F32))
    r0, r1 = res
    o = r0[:, :V_DIM] / r0[:, V_DIM:] - lam * (r1[:, :V_DIM] / r1[:, V_DIM:])
    ms = jnp.mean(o * o, axis=-1, keepdims=True)
    return o * lax.rsqrt(ms + EPS) * subg * (1.0 - LAM_INIT)


def _attn_kernel(lam_ref, subg_ref, q_ref, k_ref, v_ref, *rest, has_ctx):
    n_cast = (len(rest) - (4 if has_ctx else 2)) // 2
    if has_ctx:
        ckt_ref, cv_ref = rest[:2]
        rest = rest[2:]
    cast_in = rest[:n_cast]
    o_ref = rest[n_cast]
    cast_out = rest[n_cast + 1:2 * n_cast + 1]
    (vv_ref,) = rest[2 * n_cast + 1:]
    nb, heads, s_self, _ = k_ref.shape

    for src, dst in zip(cast_in, cast_out):
        dst[...] = src[...].astype(BF16)

    @pl.when(pl.program_id(1) == 0)
    def _():
        for b in range(nb):
            for hd in range(heads):
                vv_ref[b, hd, :s_self, :V_DIM] = v_ref[b, hd].astype(BF16)
                if has_ctx:
                    vv_ref[b, hd, s_self:, :V_DIM] = cv_ref[b, hd].astype(BF16)
                vv_ref[b, hd, :, V_DIM:] = jnp.ones((vv_ref.shape[2], V_DIM), BF16)

    lam = _lam_value(lam_ref)
    subg = subg_ref[...]
    for b in range(nb):
        for hd in range(heads):
            ckt = ckt_ref[b, hd].astype(BF16) if has_ctx else None
            o = _attend_head(q_ref[b, hd], k_ref[b, hd].astype(BF16), ckt,
                             vv_ref[b, hd], lam, subg)
            o_ref[b, :, hd * V_DIM:(hd + 1) * V_DIM] = o.astype(o_ref.dtype)


def _attention(lamv, subg2, q, k, v, ctx, nb, tq, cast=()):
    b, nh, s, _ = q.shape
    has_ctx = ctx is not None
    sk = s + (ctx[1].shape[2] if has_ctx else 0)
    nt = s // tq
    steps = (b // nb) * nt
    qspec = pl.BlockSpec((nb, nh, tq, V_DIM), lambda i, t: (i, 0, t, 0))
    kvspec = pl.BlockSpec((nb, nh, s, V_DIM), lambda i, t: (i, 0, 0, 0))
    in_specs = [
        pl.BlockSpec((4, HEAD_DIM), lambda i, t: (0, 0)),
        pl.BlockSpec((1, V_DIM), lambda i, t: (0, 0)),
        qspec, kvspec, kvspec,
    ]
    args = [lamv, subg2, q, k, v]
    if has_ctx:
        p = ctx[1].shape[2]
        in_specs += [pl.BlockSpec((nb, nh, V_DIM, p), lambda i, t: (i, 0, 0, 0)),
                     pl.BlockSpec((nb, nh, p, V_DIM), lambda i, t: (i, 0, 0, 0))]
        args += list(ctx)
    cast_specs = [pl.BlockSpec((w.shape[0] // steps, w.shape[1]),
                               lambda i, t: (i * nt + t, 0)) for w in cast]
    outs = pl.pallas_call(
        functools.partial(_attn_kernel, has_ctx=has_ctx),
        out_shape=(jax.ShapeDtypeStruct((b, s, nh * V_DIM), BF16),
                   *(jax.ShapeDtypeStruct(w.shape, BF16) for w in cast)),
        grid=(b // nb, nt),
        in_specs=in_specs + cast_specs,
        out_specs=(pl.BlockSpec((nb, tq, nh * V_DIM), lambda i, t: (i, t, 0)),
                   *cast_specs),
        scratch_shapes=[pltpu.VMEM((nb, nh, sk, 2 * V_DIM), BF16)],
        compiler_params=_params(2),
        name="diff_attn_ctx" if has_ctx else "diff_attn",
    )(*args, *cast)
    return outs


def _shifted_rows(gs_ref, slot, g, seq):
    rows, w = g.shape
    pitch = seq + SUBLANES
    for s in range(rows // seq):
        base = SUBLANES + s * pitch
        gs_ref[slot, base:base + seq, :w] = g[s * seq:(s + 1) * seq]
    shifted = []
    for off in (-1, 1):
        parts = [gs_ref[slot, SUBLANES + s * pitch + off:
                        SUBLANES + s * pitch + off + seq, :w]
                 for s in range(rows // seq)]
        shifted.append(parts[0] if len(parts) == 1 else jnp.concatenate(parts, axis=0))
    return shifted


def _mixffn_kernel(x_ref, yf_ref, o_ref, mod_ref, n2_ref, wo_ref, wg_ref, wu_ref,
                   cw_ref, cb_ref, wd_ref, y_ref, h2_ref, act_ref, gs_ref, *, seq,
                   mod_row):
    mod = _mod_rows(mod_ref, mod_row(pl.program_id(0)))
    rows = x_ref.shape[0]

    mix = (jnp.dot(yf_ref[...], wo_ref[:D_FOURIER, :].astype(BF16),
                   preferred_element_type=F32)
           + jnp.dot(o_ref[...], wo_ref[D_FOURIER:, :].astype(BF16),
                     preferred_element_type=F32))
    x1 = x_ref[...] + mod[2] * mix
    y_ref[...] = x1
    h2 = _rms_rows(x1, n2_ref[...]) * (1.0 + mod[4]) + mod[3]
    h2_ref[...] = h2.astype(BF16)
    for slot in range(gs_ref.shape[0]):
        for s in range(rows // seq + 1):
            r0 = s * (seq + SUBLANES)
            gs_ref[slot, r0:r0 + SUBLANES, :] = jnp.zeros(
                (SUBLANES, gs_ref.shape[2]), F32)

    h2 = h2_ref[...]
    cw = cw_ref[...]
    cb = cb_ref[...]
    for n, (c0, w) in enumerate(FF_SUBCHUNKS):
        if w == MXU_COLS:
            g = jnp.dot(h2, wg_ref[:, c0:c0 + w], preferred_element_type=F32)
            u = jnp.dot(h2, wu_ref[:, c0:c0 + w], preferred_element_type=F32)
        else:
            wgu = jnp.concatenate([wg_ref[:, c0:c0 + w], wu_ref[:, c0:c0 + w]], axis=1)
            gu = jnp.dot(h2, wgu, preferred_element_type=F32)
            g, u = gu[:, :w], gu[:, w:]
        g_prev, g_next = _shifted_rows(gs_ref, n % gs_ref.shape[0], g, seq)
        gc = (g * cw[1:2, c0:c0 + w] + g_prev * cw[0:1, c0:c0 + w]
              + g_next * cw[2:3, c0:c0 + w] + cb[:, c0:c0 + w])
        hg = 0.5 * gc
        act_ref[:, c0:c0 + w] = ((hg * u) * (1.0 + jnp.tanh(hg))).astype(BF16)
    y_ref[...] += mod[5] * jnp.dot(act_ref[...], wd_ref[...],
                                     preferred_element_type=F32)


def _mixffn(x2, yf2, o2, mod, mod_row, n2, w_out, w_gate, w_up, conv_w, conv_b2,
            w_down, seq):
    m, d = x2.shape
    tm = FFN_ROWS
    row = lambda i: (i, 0)
    const = lambda shape: pl.BlockSpec(shape, lambda i: (0, 0))
    stage_rows = SUBLANES + (tm // seq) * (seq + SUBLANES)
    return pl.pallas_call(
        functools.partial(_mixffn_kernel, seq=seq, mod_row=mod_row),
        out_shape=jax.ShapeDtypeStruct((m, d), F32),
        grid=(m // tm,),
        in_specs=[
            pl.BlockSpec((tm, d), row),
            pl.BlockSpec((tm, D_FOURIER), row),
            pl.BlockSpec((tm, D_ATTN), row),
            pl.BlockSpec((N_MOD, MOD_ROWS, d), lambda i: (0, 0, 0)),
            const((1, d)), const((d, d)), const((d, D_FF)), const((d, D_FF)),
            const((3, D_FF)), const((1, D_FF)), const((D_FF, d)),
        ],
        out_specs=pl.BlockSpec((tm, d), row),
        scratch_shapes=[
            pltpu.VMEM((tm, d), BF16),
            pltpu.VMEM((tm, D_FF), BF16),
            pltpu.VMEM((2, stage_rows, MXU_COLS), F32),
        ],
        compiler_params=_params(1),
        name="mix_ffn",
    )(x2, yf2, o2, mod, n2, w_out, w_gate, w_up, conv_w, conv_b2, w_down)


def _rope_tables(n_tokens):
    rows = n_tokens // GRID_W
    row = np.repeat(np.arange(rows, dtype=np.float64), GRID_W)
    col = np.tile(np.arange(GRID_W, dtype=np.float64), rows)
    n_freq = HEAD_DIM // 4
    inv = ROPE_THETA ** (-np.arange(n_freq, dtype=np.float64) / n_freq)
    ang_r = row[:, None] * inv
    ang_c = col[:, None] * inv
    zero = np.zeros_like(ang_r)
    cos64 = np.concatenate([np.cos(ang_r)] * 2 + [np.cos(ang_c)] * 2, axis=-1)
    sdn64 = np.concatenate([-np.sin(ang_r), zero, -np.sin(ang_c), zero], axis=-1)
    sup64 = np.concatenate([zero, np.sin(ang_r), zero, np.sin(ang_c)], axis=-1)
    two = lambda t: jnp.asarray(np.concatenate([t, t], axis=-1).astype(np.float32))
    return two(cos64), two(sdn64), two(sup64)


def kernel(x_prompt, x_sample, cache_k, cache_v, c, c_ctx, norm1_g, norm2_g, w_ada,
           b_ada, w_in, q_norm_g, k_norm_g, lam_q1, lam_k1, lam_q2, lam_k2, subln_g,
           w_out, w_gate, w_up, conv_w, conv_b, w_down):
    bp, sp, d = x_prompt.shape
    bs, ss, _ = x_sample.shape
    assert FFN_ROWS % sp == 0 and FFN_ROWS == ss

    mod = _ada(c_ctx, c, w_ada[0], b_ada[0])

    n1, n2, subg2 = norm1_g[0][None, :], norm2_g[0][None, :], subln_g[0][None, :]
    qkg = jnp.tile(jnp.concatenate([q_norm_g, k_norm_g], axis=0),
                   (1, MXU_COLS // HEAD_DIM))
    lamv = jnp.stack([lam_q1[0], lam_k1[0], lam_q2[0], lam_k2[0]])
    row_p = lambda i: MOD_CTX_ROW
    row_s = lambda i: MOD_REQ_ROW0 + i
    p = cache_k.shape[3]
    ck_t = jnp.transpose(cache_k[:, 0], (0, 1, 3, 4, 2)).reshape(bs, N_HEADS, V_DIM, p)
    ctx = (ck_t, cache_v[:, 0])

    yf_p, o_p, k_p, v_p = _inproj(x_prompt, mod, row_p, n1, w_in[0], qkg, None,
                                  FFN_ROWS // sp, F32, attn=(lamv, subg2))
    yf_s, q_s, k_s, v_s = _inproj(x_sample, mod, row_s, n1, w_in[0], qkg,
                                  _rope_tables(ss), FFN_ROWS // ss, BF16)
    o_s, wg_b, wu_b, wd_b = _attention(lamv, subg2, q_s, k_s, v_s, ctx, 1, ATTN_TQ,
                                       cast=(w_gate[0], w_up[0], w_down[0]))

    def mixer(x, yf, o, mod_row):
        b, s, _ = x.shape
        y = _mixffn(x.reshape(b * s, d), yf.reshape(b * s, D_FOURIER),
                    o.reshape(b * s, D_ATTN), mod, mod_row, n2, w_out[0], wg_b, wu_b,
                    conv_w[0], conv_b[0][None, :], wd_b, s)
        return y.reshape(b, s, d)

    y_p = mixer(x_prompt, yf_p, o_p, row_p)
    y_s = mixer(x_sample, yf_s, o_s, row_s)

    new_k = jnp.transpose(k_p.reshape(bp, 1, N_HEADS, 2, HEAD_DIM, sp),
                          (0, 1, 2, 5, 3, 4))
    new_v = v_p.reshape(bp, 1, N_HEADS, sp, V_DIM)
    return (y_p, y_s, new_k, new_v)
```

```python
import functools
import math

import jax
import jax.numpy as jnp
import numpy as np
from jax import lax
from jax.experimental import pallas as pl
from jax.experimental.pallas import tpu as pltpu

F32 = jnp.float32
BF16 = jnp.bfloat16

D_MODEL = 1024
GRID_W = 64
N_HEADS = 6
HEAD_DIM = 64
V_DIM = 2 * HEAD_DIM
D_FOURIER = 256
FOURIER_HEAD_DIM = 64
D_QK = N_HEADS * V_DIM
D_ATTN = N_HEADS * V_DIM
D_IN = D_FOURIER + 2 * D_QK + D_ATTN
D_FF = 2816
ROPE_THETA = 10000.0
EPS = 1e-6
LAM_INIT = 0.8 - 0.6 * math.exp(-0.3 * 0)
Q_SCALE = HEAD_DIM ** -0.5 * math.log2(math.e)
N_MOD = 6
MOD_ROWS = 16
MOD_CTX_ROW = 0
MOD_REQ_ROW0 = 8
ADA_K_ROWS = 256

SUBLANES = 8
MXU_COLS = 256
FF_SUBCHUNKS = tuple((c, min(MXU_COLS, D_FF - c)) for c in range(0, D_FF, MXU_COLS))
FFN_ROWS = 1024
ATTN_TQ = 512
VMEM_LIMIT = 58 * 1024 * 1024


def _params(n_axes):
    return pltpu.CompilerParams(
        dimension_semantics=("arbitrary",) * n_axes,
        vmem_limit_bytes=VMEM_LIMIT,
    )


def _ada_kernel(ctx_ref, c_ref, w_ref, b_ref, o_ref, acc_ref):
    j = pl.program_id(0)
    c = jnp.concatenate(
        [jnp.broadcast_to(ctx_ref[...], (SUBLANES, ctx_ref.shape[1])), c_ref[...]], axis=0)
    s = c * jax.nn.sigmoid(c)
    part = jnp.dot(s.astype(BF16), w_ref[...].astype(BF16), preferred_element_type=F32)

    @pl.when(j == 0)
    def _():
        acc_ref[...] = part + b_ref[...]

    @pl.when(j > 0)
    def _():
        acc_ref[...] += part

    @pl.when(j == pl.num_programs(0) - 1)
    def _():
        for k in range(N_MOD):
            o_ref[k] = acc_ref[:, k * D_MODEL:(k + 1) * D_MODEL]


def _ada(c_ctx, c, w_ada, b_ada):
    d, n = w_ada.shape
    tk = ADA_K_ROWS
    assert c.shape[0] == MOD_ROWS - SUBLANES
    return pl.pallas_call(
        _ada_kernel,
        out_shape=jax.ShapeDtypeStruct((N_MOD, MOD_ROWS, D_MODEL), F32),
        grid=(d // tk,),
        in_specs=[
            pl.BlockSpec((1, tk), lambda j: (0, j)),
            pl.BlockSpec((c.shape[0], tk), lambda j: (0, j)),
            pl.BlockSpec((tk, n), lambda j: (j, 0)),
            pl.BlockSpec((1, n), lambda j: (0, 0)),
        ],
        out_specs=pl.BlockSpec((N_MOD, MOD_ROWS, D_MODEL), lambda j: (0, 0, 0)),
        scratch_shapes=[pltpu.VMEM((MOD_ROWS, n), F32)],
        compiler_params=_params(1),
        name="ada_mod",
    )(c_ctx.reshape(1, d), c, w_ada, b_ada.reshape(1, n))


def _mod_rows(mod_ref, row):
    return [mod_ref[k, pl.ds(row, 1), :] for k in range(N_MOD)]


def _rms_rows(x, gain):
    ms = jnp.mean(x * x, axis=-1, keepdims=True)
    return x * lax.rsqrt(ms + EPS) * gain


def _group_rmsnorm(x, gain, ones_blk):
    ss = jnp.dot((x * x).astype(BF16), ones_blk, preferred_element_type=F32)
    return x * lax.rsqrt(ss * (1.0 / HEAD_DIM) + EPS) * gain


def _rope(x, cos, sin_dn, sin_up):
    return (x * cos + pltpu.roll(x, V_DIM - 16, 1) * sin_dn
            + pltpu.roll(x, 16, 1) * sin_up)


def _inproj_kernel(x_ref, mod_ref, n1_ref, w_ref, qkg_ref, ones_ref, cc_ref,
                   sc_ref, cs_ref, ss_ref, *rest, use_rope, attend, kv_dtype, mod_row):
    if use_rope:
        cos_ref, sdn_ref, sup_ref = rest[:3]
        rest = rest[3:]
    if attend:
        lam_ref, subg_ref, yf_ref, o_ref, k_ref, v_ref = rest
        lam = _lam_value(lam_ref)
        ones_v = jnp.ones((x_ref.shape[1], V_DIM), BF16)
    else:
        yf_ref, q_ref, k_ref, v_ref = rest
    nb, seq, d = x_ref.shape
    x = x_ref[...].reshape(nb * seq, d)
    mod = _mod_rows(mod_ref, mod_row(pl.program_id(0)))
    h = (_rms_rows(x, n1_ref[...]) * (1.0 + mod[1]) + mod[0]).astype(BF16)

    def project(c0):
        return jnp.dot(h, w_ref[:, c0:c0 + MXU_COLS].astype(BF16),
                       preferred_element_type=F32)

    xf = project(0).astype(BF16)
    xc = jnp.dot(xf, cc_ref[...], preferred_element_type=F32).astype(BF16)
    xs = jnp.dot(xf, sc_ref[...], preferred_element_type=F32).astype(BF16)
    for b in range(nb):
        r = slice(b * seq, (b + 1) * seq)
        yf = (jnp.dot(cs_ref[...], xc[r], preferred_element_type=F32)
              - jnp.dot(ss_ref[...], xs[r], preferred_element_type=F32))
        yf_ref[b] = yf.astype(yf_ref.dtype)

    def store_heads(out_ref, head0, y, dtype):
        for hh in range(y.shape[1] // V_DIM):
            for b in range(nb):
                out_ref[b, head0 + hh] = y[b * seq:(b + 1) * seq,
                                           hh * V_DIM:(hh + 1) * V_DIM].astype(dtype)

    ones_blk = ones_ref[...]
    def project_pair(pair):
        c0 = D_FOURIER + pair * MXU_COLS
        return project(c0), project(c0 + D_QK), project(c0 + 2 * D_QK)

    n_pairs = N_HEADS // 2
    lookahead = 1 if attend else 0
    done = [project_pair(p) for p in range(lookahead)]
    for pair in range(n_pairs):
        if pair + lookahead < n_pairs:
            done.append(project_pair(pair + lookahead))
        zq, zk, zv = done[pair]
        q = _group_rmsnorm(zq, qkg_ref[0:1, :], ones_blk)
        k = _group_rmsnorm(zk, qkg_ref[1:2, :], ones_blk)
        for hh in range(2):
            ls = slice(hh * V_DIM, (hh + 1) * V_DIM)
            qh, kh = q[:, ls], k[:, ls]
            if use_rope:
                qh = _rope(qh, cos_ref[...], sdn_ref[...], sup_ref[...])
                kh = _rope(kh, cos_ref[...], sdn_ref[...], sup_ref[...])
            hd = 2 * pair + hh
            qs = (qh * Q_SCALE).astype(BF16)
            if not attend:
                store_heads(k_ref, hd, kh, kv_dtype)
                store_heads(q_ref, hd, qs, BF16)
                continue
            kb, vb = kh.astype(BF16), zv[:, ls].astype(BF16)
            for b in range(nb):
                r = slice(b * seq, (b + 1) * seq)
                k_ref[b, hd] = kh[r].T.astype(kv_dtype)
                o = _attend_head(qs[r], kb[r], None,
                                 jnp.concatenate([vb[r], ones_v], axis=1), lam,
                                 subg_ref[...])
                o_ref[b, :, hd * V_DIM:(hd + 1) * V_DIM] = o.astype(o_ref.dtype)
        store_heads(v_ref, 2 * pair, zv, kv_dtype)


def _dft_tables(n, scale):
    idx = np.arange(n, dtype=np.int64)
    ang = ((idx[:, None] * idx[None, :]) % n).astype(np.float64) * (2.0 * math.pi / n)
    return ((np.cos(ang) * scale).astype(np.float32),
            (np.sin(ang) * scale).astype(np.float32))


def _fourier_tables(s):
    c64, s64 = _dft_tables(FOURIER_HEAD_DIM, (s * FOURIER_HEAD_DIM) ** -0.5)
    eye = np.eye(D_FOURIER // FOURIER_HEAD_DIM, dtype=np.float32)
    tabs = (np.kron(eye, c64), np.kron(eye, s64)) + _dft_tables(s, 1.0)
    return tuple(jnp.asarray(t).astype(BF16) for t in tabs)


def _inproj(x, mod, mod_row, n1, w_in, qkg, rope, nb, kv_dtype, attn=None):
    b, s, d = x.shape
    use_rope = rope is not None
    attend = attn is not None
    ones_blk = jnp.asarray(np.kron(np.eye(MXU_COLS // HEAD_DIM, dtype=np.float32),
                                   np.ones((HEAD_DIM, HEAD_DIM), np.float32))).astype(BF16)
    const = lambda shape: pl.BlockSpec(shape, lambda i: (0, 0))
    in_specs = [
        pl.BlockSpec((nb, s, d), lambda i: (i, 0, 0)),
        pl.BlockSpec((N_MOD, MOD_ROWS, d), lambda i: (0, 0, 0)),
        const((1, d)), const((d, D_IN)), const((2, MXU_COLS)),
        const((MXU_COLS, MXU_COLS)), const((D_FOURIER, D_FOURIER)),
        const((D_FOURIER, D_FOURIER)), const((s, s)), const((s, s)),
    ]
    args = [x, mod, n1, w_in, qkg, ones_blk, *_fourier_tables(s)]
    if use_rope:
        assert nb == 1
        in_specs += [const((s, V_DIM))] * 3
        args += list(rope)
    if attend:
        in_specs += [const((4, HEAD_DIM)), const((1, V_DIM))]
        args += list(attn)
    head_spec = pl.BlockSpec((nb, N_HEADS, s, V_DIM), lambda i: (i, 0, 0, 0))
    head_shape = (b, N_HEADS, s, V_DIM)
    if attend:
        second = (jax.ShapeDtypeStruct((b, s, D_ATTN), BF16),
                  pl.BlockSpec((nb, s, D_ATTN), lambda i: (i, 0, 0)))
        keys = (jax.ShapeDtypeStruct((b, N_HEADS, V_DIM, s), kv_dtype),
                pl.BlockSpec((nb, N_HEADS, V_DIM, s), lambda i: (i, 0, 0, 0)))
    else:
        second = (jax.ShapeDtypeStruct(head_shape, BF16), head_spec)
        keys = (jax.ShapeDtypeStruct(head_shape, kv_dtype), head_spec)
    return pl.pallas_call(
        functools.partial(_inproj_kernel, use_rope=use_rope, attend=attend,
                          kv_dtype=kv_dtype, mod_row=mod_row),
        out_shape=(
            jax.ShapeDtypeStruct((b, s, D_FOURIER), BF16),
            second[0], keys[0],
            jax.ShapeDtypeStruct(head_shape, kv_dtype),
        ),
        grid=(b // nb,),
        in_specs=in_specs,
        out_specs=(
            pl.BlockSpec((nb, s, D_FOURIER), lambda i: (i, 0, 0)),
            second[1], keys[1], head_spec,
        ),
        compiler_params=_params(1),
        name="inproj_rope" if use_rope else "inproj_attn" if attend else "inproj",
    )(*args)


def _scores(q, k):
    return lax.dot_general(q, k, (((1,), (1,)), ((), ())),
                           preferred_element_type=F32)


def _lam_value(lam_ref):
    lv = lam_ref[...]
    e1 = jnp.exp(jnp.sum(lv[0:1] * lv[1:2], axis=-1, keepdims=True))
    e2 = jnp.exp(jnp.sum(lv[2:3] * lv[3:4], axis=-1, keepdims=True))
    return e1 - e2 + LAM_INIT


def _attend_head(q, kk, ckt, vv, lam, subg):
    res = []
    for qc in _split_components(q):
        p = _softmax_numerator(_component_scores(qc, kk, ckt))
        res.append(jnp.dot(p, vv, preferred_element_type=F32))
    return _combine_components(res, lam, subg)


def _split_components(q):
    lo_mask = lax.broadcasted_iota(jnp.int32, (1, V_DIM), 1) < HEAD_DIM
    zero = jnp.zeros_like(q)
    return jnp.where(lo_mask, q, zero), jnp.where(lo_mask, zero, q)


def _component_scores(qc, kk, ckt):
    s = _scores(qc, kk)
    if ckt is not None:
        s = jnp.concatenate([s, jnp.dot(qc, ckt, preferred_element_type=F32)], axis=1)
    return s


def _softmax_numerator(s):
    return jnp.exp2(s - s.max(axis=-1, keepdims=True)).astype(BF16)


def _combine_components(res, lam, subg):
    r0, r1 = res
    o = r0[:, :V_DIM] / r0[:, V_DIM:] - lam * (r1[:, :V_DIM] / r1[:, V_DIM:])
    ms = jnp.mean(o * o, axis=-1, keepdims=True)
    return o * lax.rsqrt(ms + EPS) * subg * (1.0 - LAM_INIT)


def _attn_kernel(lam_ref, subg_ref, q_ref, k_ref, v_ref, *rest, has_ctx):
    n_cast = (len(rest) - (3 if has_ctx else 1)) // 2
    if has_ctx:
        ckt_ref, cv_ref = rest[:2]
        rest = rest[2:]
    cast_in = rest[:n_cast]
    o_ref = rest[n_cast]
    cast_out = rest[n_cast + 1:2 * n_cast + 1]
    nb, heads, s_self, _ = k_ref.shape

    for src, dst in zip(cast_in, cast_out):
        dst[...] = src[...].astype(BF16)

    lam = _lam_value(lam_ref)
    subg = subg_ref[...]

    def head_values(b, hd):
        v = v_ref[b, hd].astype(BF16)
        if has_ctx:
            v = jnp.concatenate([v, cv_ref[b, hd].astype(BF16)], axis=0)
        return jnp.concatenate([v, jnp.ones_like(v)], axis=1)

    def head_scores(b, hd):
        ckt = ckt_ref[b, hd].astype(BF16) if has_ctx else None
        kk = k_ref[b, hd].astype(BF16)
        return [_component_scores(qc, kk, ckt) for qc in _split_components(q_ref[b, hd])]

    units = [(b, hd) for b in range(nb) for hd in range(heads)]
    ahead = head_scores(*units[0])
    for n, (b, hd) in enumerate(units):
        scores = ahead
        if n + 1 < len(units):
            ahead = head_scores(*units[n + 1])
        vv = head_values(b, hd)
        res = [jnp.dot(_softmax_numerator(s), vv, preferred_element_type=F32)
               for s in scores]
        o = _combine_components(res, lam, subg)
        o_ref[b, :, hd * V_DIM:(hd + 1) * V_DIM] = o.astype(o_ref.dtype)


def _attention(lamv, subg2, q, k, v, ctx, nb, tq, cast=()):
    b, nh, s, _ = q.shape
    has_ctx = ctx is not None
    nt = s // tq
    steps = (b // nb) * nt
    qspec = pl.BlockSpec((nb, nh, tq, V_DIM), lambda i, t: (i, 0, t, 0))
    kvspec = pl.BlockSpec((nb, nh, s, V_DIM), lambda i, t: (i, 0, 0, 0))
    in_specs = [
        pl.BlockSpec((4, HEAD_DIM), lambda i, t: (0, 0)),
        pl.BlockSpec((1, V_DIM), lambda i, t: (0, 0)),
        qspec, kvspec, kvspec,
    ]
    args = [lamv, subg2, q, k, v]
    if has_ctx:
        p = ctx[1].shape[2]
        in_specs += [pl.BlockSpec((nb, nh, V_DIM, p), lambda i, t: (i, 0, 0, 0)),
                     pl.BlockSpec((nb, nh, p, V_DIM), lambda i, t: (i, 0, 0, 0))]
        args += list(ctx)
    cast_specs = [pl.BlockSpec((w.shape[0] // steps, w.shape[1]),
                               lambda i, t: (i * nt + t, 0)) for w in cast]
    outs = pl.pallas_call(
        functools.partial(_attn_kernel, has_ctx=has_ctx),
        out_shape=(jax.ShapeDtypeStruct((b, s, nh * V_DIM), BF16),
                   *(jax.ShapeDtypeStruct(w.shape, BF16) for w in cast)),
        grid=(b // nb, nt),
        in_specs=in_specs + cast_specs,
        out_specs=(pl.BlockSpec((nb, tq, nh * V_DIM), lambda i, t: (i, t, 0)),
                   *cast_specs),
        compiler_params=_params(2),
        name="diff_attn_ctx" if has_ctx else "diff_attn",
    )(*args, *cast)
    return outs


def _shifted_rows(gs_ref, slot, g, seq):
    rows, w = g.shape
    pitch = seq + SUBLANES
    for s in range(rows // seq):
        base = SUBLANES + s * pitch
        gs_ref[slot, base:base + seq, :w] = g[s * seq:(s + 1) * seq]
    shifted = []
    for off in (-1, 1):
        parts = [gs_ref[slot, SUBLANES + s * pitch + off:
                        SUBLANES + s * pitch + off + seq, :w]
                 for s in range(rows // seq)]
        shifted.append(parts[0] if len(parts) == 1 else jnp.concatenate(parts, axis=0))
    return shifted


def _mixffn_kernel(x_ref, yf_ref, o_ref, mod_ref, n2_ref, wo_ref, wg_ref, wu_ref,
                   cw_ref, cb_ref, wd_ref, y_ref, h2_ref, act_ref, gs_ref, *, seq,
                   mod_row):
    mod = _mod_rows(mod_ref, mod_row(pl.program_id(0)))
    rows = x_ref.shape[0]

    mix = (jnp.dot(yf_ref[...], wo_ref[:D_FOURIER, :].astype(BF16),
                   preferred_element_type=F32)
           + jnp.dot(o_ref[...], wo_ref[D_FOURIER:, :].astype(BF16),
                     preferred_element_type=F32))
    x1 = x_ref[...] + mod[2] * mix
    y_ref[...] = x1
    h2 = _rms_rows(x1, n2_ref[...]) * (1.0 + mod[4]) + mod[3]
    h2_ref[...] = h2.astype(BF16)
    for slot in range(gs_ref.shape[0]):
        for s in range(rows // seq + 1):
            r0 = s * (seq + SUBLANES)
            gs_ref[slot, r0:r0 + SUBLANES, :] = jnp.zeros(
                (SUBLANES, gs_ref.shape[2]), F32)

    h2 = h2_ref[...]
    cw = cw_ref[...]
    cb = cb_ref[...]
    for n, (c0, w) in enumerate(FF_SUBCHUNKS):
        if w == MXU_COLS:
            g = jnp.dot(h2, wg_ref[:, c0:c0 + w], preferred_element_type=F32)
            u = jnp.dot(h2, wu_ref[:, c0:c0 + w], preferred_element_type=F32)
        else:
            wgu = jnp.concatenate([wg_ref[:, c0:c0 + w], wu_ref[:, c0:c0 + w]], axis=1)
            gu = jnp.dot(h2, wgu, preferred_element_type=F32)
            g, u = gu[:, :w], gu[:, w:]
        g_prev, g_next = _shifted_rows(gs_ref, n % gs_ref.shape[0], g, seq)
        gc = (g * cw[1:2, c0:c0 + w] + g_prev * cw[0:1, c0:c0 + w]
              + g_next * cw[2:3, c0:c0 + w] + cb[:, c0:c0 + w])
        hg = 0.5 * gc
        act_ref[:, c0:c0 + w] = ((hg * u) * (1.0 + jnp.tanh(hg))).astype(BF16)
    y_ref[...] += mod[5] * jnp.dot(act_ref[...], wd_ref[...],
                                     preferred_element_type=F32)


def _mixffn(x2, yf2, o2, mod, mod_row, n2, w_out, w_gate, w_up, conv_w, conv_b2,
            w_down, seq):
    m, d = x2.shape
    tm = FFN_ROWS
    row = lambda i: (i, 0)
    const = lambda shape: pl.BlockSpec(shape, lambda i: (0, 0))
    stage_rows = SUBLANES + (tm // seq) * (seq + SUBLANES)
    return pl.pallas_call(
        functools.partial(_mixffn_kernel, seq=seq, mod_row=mod_row),
        out_shape=jax.ShapeDtypeStruct((m, d), F32),
        grid=(m // tm,),
        in_specs=[
            pl.BlockSpec((tm, d), row),
            pl.BlockSpec((tm, D_FOURIER), row),
            pl.BlockSpec((tm, D_ATTN), row),
            pl.BlockSpec((N_MOD, MOD_ROWS, d), lambda i: (0, 0, 0)),
            const((1, d)), const((d, d)), const((d, D_FF)), const((d, D_FF)),
            const((3, D_FF)), const((1, D_FF)), const((D_FF, d)),
        ],
        out_specs=pl.BlockSpec((tm, d), row),
        scratch_shapes=[
            pltpu.VMEM((tm, d), BF16),
            pltpu.VMEM((tm, D_FF), BF16),
            pltpu.VMEM((2, stage_rows, MXU_COLS), F32),
        ],
        compiler_params=_params(1),
        name="mix_ffn",
    )(x2, yf2, o2, mod, n2, w_out, w_gate, w_up, conv_w, conv_b2, w_down)


def _rope_tables(n_tokens):
    rows = n_tokens // GRID_W
    row = np.repeat(np.arange(rows, dtype=np.float64), GRID_W)
    col = np.tile(np.arange(GRID_W, dtype=np.float64), rows)
    n_freq = HEAD_DIM // 4
    inv = ROPE_THETA ** (-np.arange(n_freq, dtype=np.float64) / n_freq)
    ang_r = row[:, None] * inv
    ang_c = col[:, None] * inv
    zero = np.zeros_like(ang_r)
    cos64 = np.concatenate([np.cos(ang_r)] * 2 + [np.cos(ang_c)] * 2, axis=-1)
    sdn64 = np.concatenate([-np.sin(ang_r), zero, -np.sin(ang_c), zero], axis=-1)
    sup64 = np.concatenate([zero, np.sin(ang_r), zero, np.sin(ang_c)], axis=-1)
    two = lambda t: jnp.asarray(np.concatenate([t, t], axis=-1).astype(np.float32))
    return two(cos64), two(sdn64), two(sup64)


def kernel(x_prompt, x_sample, cache_k, cache_v, c, c_ctx, norm1_g, norm2_g, w_ada,
           b_ada, w_in, q_norm_g, k_norm_g, lam_q1, lam_k1, lam_q2, lam_k2, subln_g,
           w_out, w_gate, w_up, conv_w, conv_b, w_down):
    bp, sp, d = x_prompt.shape
    bs, ss, _ = x_sample.shape
    assert FFN_ROWS % sp == 0 and FFN_ROWS == ss

    mod = _ada(c_ctx, c, w_ada[0], b_ada[0])

    n1, n2, subg2 = norm1_g[0][None, :], norm2_g[0][None, :], subln_g[0][None, :]
    qkg = jnp.tile(jnp.concatenate([q_norm_g, k_norm_g], axis=0),
                   (1, MXU_COLS // HEAD_DIM))
    lamv = jnp.stack([lam_q1[0], lam_k1[0], lam_q2[0], lam_k2[0]])
    row_p = lambda i: MOD_CTX_ROW
    row_s = lambda i: MOD_REQ_ROW0 + i
    p = cache_k.shape[3]
    ck_t = jnp.transpose(cache_k[:, 0], (0, 1, 3, 4, 2)).reshape(bs, N_HEADS, V_DIM, p)
    ctx = (ck_t, cache_v[:, 0])

    yf_p, o_p, k_p, v_p = _inproj(x_prompt, mod, row_p, n1, w_in[0], qkg, None,
                                  FFN_ROWS // sp, F32, attn=(lamv, subg2))
    yf_s, q_s, k_s, v_s = _inproj(x_sample, mod, row_s, n1, w_in[0], qkg,
                                  _rope_tables(ss), FFN_ROWS // ss, BF16)
    o_s, wg_b, wu_b, wd_b = _attention(lamv, subg2, q_s, k_s, v_s, ctx, 1, ATTN_TQ,
                                       cast=(w_gate[0], w_up[0], w_down[0]))

    def mixer(x, yf, o, mod_row):
        b, s, _ = x.shape
        y = _mixffn(x.reshape(b * s, d), yf.reshape(b * s, D_FOURIER),
                    o.reshape(b * s, D_ATTN), mod, mod_row, n2, w_out[0], wg_b, wu_b,
                    conv_w[0], conv_b[0][None, :], wd_b, s)
        return y.reshape(b, s, d)

    y_p = mixer(x_prompt, yf_p, o_p, row_p)
    y_s = mixer(x_sample, yf_s, o_s, row_s)

    new_k = jnp.transpose(k_p.reshape(bp, 1, N_HEADS, 2, HEAD_DIM, sp),
                          (0, 1, 2, 5, 3, 4))
    new_v = v_p.reshape(bp, 1, N_HEADS, sp, V_DIM)
    return (y_p, y_s, new_k, new_v)
```

```python
import functools
import math

import jax
import jax.numpy as jnp
import numpy as np
from jax import lax
from jax.experimental import pallas as pl
from jax.experimental.pallas import tpu as pltpu

F32 = jnp.float32
BF16 = jnp.bfloat16

D_MODEL = 1024
GRID_W = 64
N_HEADS = 6
HEAD_DIM = 64
V_DIM = 2 * HEAD_DIM
D_FOURIER = 256
FOURIER_HEAD_DIM = 64
D_QK = N_HEADS * V_DIM
D_ATTN = N_HEADS * V_DIM
D_IN = D_FOURIER + 2 * D_QK + D_ATTN
D_FF = 2816
ROPE_THETA = 10000.0
EPS = 1e-6
LAM_INIT = 0.8 - 0.6 * math.exp(-0.3 * 0)
Q_SCALE = HEAD_DIM ** -0.5 * math.log2(math.e)
N_MOD = 6
MOD_ROWS = 16
MOD_CTX_ROW = 0
MOD_REQ_ROW0 = 8
ADA_K_ROWS = 256

SUBLANES = 8
MXU_COLS = 256
assert D_FF % MXU_COLS == 0
FFN_ROWS = 1024
ATTN_TQ = 512
VMEM_LIMIT = 58 * 1024 * 1024


def _params(n_axes):
    return pltpu.CompilerParams(
        dimension_semantics=("arbitrary",) * n_axes,
        vmem_limit_bytes=VMEM_LIMIT,
    )


def _ada_kernel(ctx_ref, c_ref, w_ref, b_ref, o_ref, acc_ref):
    j = pl.program_id(0)
    c = jnp.concatenate(
        [jnp.broadcast_to(ctx_ref[...], (SUBLANES, ctx_ref.shape[1])), c_ref[...]], axis=0)
    s = c * jax.nn.sigmoid(c)
    part = jnp.dot(s.astype(BF16), w_ref[...].astype(BF16), preferred_element_type=F32)

    @pl.when(j == 0)
    def _():
        acc_ref[...] = part + b_ref[...]

    @pl.when(j > 0)
    def _():
        acc_ref[...] += part

    @pl.when(j == pl.num_programs(0) - 1)
    def _():
        for k in range(N_MOD):
            o_ref[k] = acc_ref[:, k * D_MODEL:(k + 1) * D_MODEL]


def _ada(c_ctx, c, w_ada, b_ada):
    d, n = w_ada.shape
    tk = ADA_K_ROWS
    assert c.shape[0] == MOD_ROWS - SUBLANES
    return pl.pallas_call(
        _ada_kernel,
        out_shape=jax.ShapeDtypeStruct((N_MOD, MOD_ROWS, D_MODEL), F32),
        grid=(d // tk,),
        in_specs=[
            pl.BlockSpec((1, tk), lambda j: (0, j)),
            pl.BlockSpec((c.shape[0], tk), lambda j: (0, j)),
            pl.BlockSpec((tk, n), lambda j: (j, 0)),
            pl.BlockSpec((1, n), lambda j: (0, 0)),
        ],
        out_specs=pl.BlockSpec((N_MOD, MOD_ROWS, D_MODEL), lambda j: (0, 0, 0)),
        scratch_shapes=[pltpu.VMEM((MOD_ROWS, n), F32)],
        compiler_params=_params(1),
        name="ada_mod",
    )(c_ctx.reshape(1, d), c, w_ada, b_ada.reshape(1, n))


def _mod_rows(mod_ref, row):
    return [mod_ref[k, pl.ds(row, 1), :] for k in range(N_MOD)]


def _rms_rows(x, gain):
    ms = jnp.mean(x * x, axis=-1, keepdims=True)
    return x * lax.rsqrt(ms + EPS) * gain


def _group_rmsnorm(x, gain, ones_blk):
    ss = jnp.dot((x * x).astype(BF16), ones_blk, preferred_element_type=F32)
    return x * lax.rsqrt(ss * (1.0 / HEAD_DIM) + EPS) * gain


def _rope(x, cos, sin_dn, sin_up):
    return (x * cos + pltpu.roll(x, V_DIM - 16, 1) * sin_dn
            + pltpu.roll(x, 16, 1) * sin_up)


def _inproj_kernel(x_ref, mod_ref, n1_ref, w_ref, qkg_ref, ones_ref, cc_ref,
                   sc_ref, cs_ref, ss_ref, *rest, use_rope, attend, kv_dtype, mod_row):
    if use_rope:
        cos_ref, sdn_ref, sup_ref = rest[:3]
        rest = rest[3:]
    if attend:
        lam_ref, subg_ref, yf_ref, o_ref, k_ref, v_ref = rest
        lam = _lam_value(lam_ref)
        ones_v = jnp.ones((x_ref.shape[1], V_DIM), BF16)
    else:
        yf_ref, q_ref, k_ref, v_ref = rest
    nb, seq, d = x_ref.shape
    x = x_ref[...].reshape(nb * seq, d)
    mod = _mod_rows(mod_ref, mod_row(pl.program_id(0)))
    h = (_rms_rows(x, n1_ref[...]) * (1.0 + mod[1]) + mod[0]).astype(BF16)

    def project(c0):
        return jnp.dot(h, w_ref[:, c0:c0 + MXU_COLS].astype(BF16),
                       preferred_element_type=F32)

    xf = project(0).astype(BF16)
    xc = jnp.dot(xf, cc_ref[...], preferred_element_type=F32).astype(BF16)
    xs = jnp.dot(xf, sc_ref[...], preferred_element_type=F32).astype(BF16)
    for b in range(nb):
        r = slice(b * seq, (b + 1) * seq)
        yf = (jnp.dot(cs_ref[...], xc[r], preferred_element_type=F32)
              - jnp.dot(ss_ref[...], xs[r], preferred_element_type=F32))
        yf_ref[b] = yf.astype(yf_ref.dtype)

    def store_heads(out_ref, head0, y, dtype):
        for hh in range(y.shape[1] // V_DIM):
            for b in range(nb):
                out_ref[b, head0 + hh] = y[b * seq:(b + 1) * seq,
                                           hh * V_DIM:(hh + 1) * V_DIM].astype(dtype)

    ones_blk = ones_ref[...]
    def project_pair(pair):
        c0 = D_FOURIER + pair * MXU_COLS
        return project(c0), project(c0 + D_QK), project(c0 + 2 * D_QK)

    n_pairs = N_HEADS // 2
    lookahead = 1 if attend else 0
    done = [project_pair(p) for p in range(lookahead)]
    for pair in range(n_pairs):
        if pair + lookahead < n_pairs:
            done.append(project_pair(pair + lookahead))
        zq, zk, zv = done[pair]
        q = _group_rmsnorm(zq, qkg_ref[0:1, :], ones_blk)
        k = _group_rmsnorm(zk, qkg_ref[1:2, :], ones_blk)
        for hh in range(2):
            ls = slice(hh * V_DIM, (hh + 1) * V_DIM)
            qh, kh = q[:, ls], k[:, ls]
            if use_rope:
                qh = _rope(qh, cos_ref[...], sdn_ref[...], sup_ref[...])
                kh = _rope(kh, cos_ref[...], sdn_ref[...], sup_ref[...])
            hd = 2 * pair + hh
            qs = (qh * Q_SCALE).astype(BF16)
            if not attend:
                store_heads(k_ref, hd, kh, kv_dtype)
                store_heads(q_ref, hd, qs, BF16)
                continue
            kb, vb = kh.astype(BF16), zv[:, ls].astype(BF16)
            units = []
            for b in range(nb):
                r = slice(b * seq, (b + 1) * seq)
                k_ref[b, hd] = kh[r].T.astype(kv_dtype)
                units.append((
                    functools.partial(_head_scores, qs[r], kb[r], None),
                    functools.partial(jnp.concatenate, [vb[r], ones_v], axis=1),
                    functools.partial(_store_head, o_ref, b, hd)))
            _attend_units(units, lam, subg_ref[...])
        store_heads(v_ref, 2 * pair, zv, kv_dtype)


def _dft_tables(n, scale):
    idx = np.arange(n, dtype=np.int64)
    ang = ((idx[:, None] * idx[None, :]) % n).astype(np.float64) * (2.0 * math.pi / n)
    return ((np.cos(ang) * scale).astype(np.float32),
            (np.sin(ang) * scale).astype(np.float32))


def _fourier_tables(s):
    c64, s64 = _dft_tables(FOURIER_HEAD_DIM, (s * FOURIER_HEAD_DIM) ** -0.5)
    eye = np.eye(D_FOURIER // FOURIER_HEAD_DIM, dtype=np.float32)
    tabs = (np.kron(eye, c64), np.kron(eye, s64)) + _dft_tables(s, 1.0)
    return tuple(jnp.asarray(t).astype(BF16) for t in tabs)


def _inproj(x, mod, mod_row, n1, w_in, qkg, rope, nb, kv_dtype, attn=None):
    b, s, d = x.shape
    use_rope = rope is not None
    attend = attn is not None
    ones_blk = jnp.asarray(np.kron(np.eye(MXU_COLS // HEAD_DIM, dtype=np.float32),
                                   np.ones((HEAD_DIM, HEAD_DIM), np.float32))).astype(BF16)
    const = lambda shape: pl.BlockSpec(shape, lambda i: (0, 0))
    in_specs = [
        pl.BlockSpec((nb, s, d), lambda i: (i, 0, 0)),
        pl.BlockSpec((N_MOD, MOD_ROWS, d), lambda i: (0, 0, 0)),
        const((1, d)), const((d, D_IN)), const((2, MXU_COLS)),
        const((MXU_COLS, MXU_COLS)), const((D_FOURIER, D_FOURIER)),
        const((D_FOURIER, D_FOURIER)), const((s, s)), const((s, s)),
    ]
    args = [x, mod, n1, w_in, qkg, ones_blk, *_fourier_tables(s)]
    if use_rope:
        assert nb == 1
        in_specs += [const((s, V_DIM))] * 3
        args += list(rope)
    if attend:
        in_specs += [const((4, HEAD_DIM)), const((1, V_DIM))]
        args += list(attn)
    head_spec = pl.BlockSpec((nb, N_HEADS, s, V_DIM), lambda i: (i, 0, 0, 0))
    head_shape = (b, N_HEADS, s, V_DIM)
    if attend:
        second = (jax.ShapeDtypeStruct((b, s, D_ATTN), BF16),
                  pl.BlockSpec((nb, s, D_ATTN), lambda i: (i, 0, 0)))
        keys = (jax.ShapeDtypeStruct((b, N_HEADS, V_DIM, s), kv_dtype),
                pl.BlockSpec((nb, N_HEADS, V_DIM, s), lambda i: (i, 0, 0, 0)))
    else:
        second = (jax.ShapeDtypeStruct(head_shape, BF16), head_spec)
        keys = (jax.ShapeDtypeStruct(head_shape, kv_dtype), head_spec)
    return pl.pallas_call(
        functools.partial(_inproj_kernel, use_rope=use_rope, attend=attend,
                          kv_dtype=kv_dtype, mod_row=mod_row),
        out_shape=(
            jax.ShapeDtypeStruct((b, s, D_FOURIER), BF16),
            second[0], keys[0],
            jax.ShapeDtypeStruct(head_shape, kv_dtype),
        ),
        grid=(b // nb,),
        in_specs=in_specs,
        out_specs=(
            pl.BlockSpec((nb, s, D_FOURIER), lambda i: (i, 0, 0)),
            second[1], keys[1], head_spec,
        ),
        compiler_params=_params(1),
        name="inproj_rope" if use_rope else "inproj_attn" if attend else "inproj",
    )(*args)


def _scores(q, k):
    return lax.dot_general(q, k, (((1,), (1,)), ((), ())),
                           preferred_element_type=F32)


def _lam_value(lam_ref):
    lv = lam_ref[...]
    e1 = jnp.exp(jnp.sum(lv[0:1] * lv[1:2], axis=-1, keepdims=True))
    e2 = jnp.exp(jnp.sum(lv[2:3] * lv[3:4], axis=-1, keepdims=True))
    return e1 - e2 + LAM_INIT


def _attend_units(units, lam, subg):
    ahead = units[0][0]()
    for n, (_, values_fn, store_fn) in enumerate(units):
        scores = ahead
        if n + 1 < len(units):
            ahead = units[n + 1][0]()
        vv = values_fn()
        r0, r1 = [jnp.dot(jnp.exp2(s - s.max(axis=-1, keepdims=True)).astype(BF16), vv,
                          preferred_element_type=F32) for s in scores]
        o = r0[:, :V_DIM] / r0[:, V_DIM:] - lam * (r1[:, :V_DIM] / r1[:, V_DIM:])
        ms = jnp.mean(o * o, axis=-1, keepdims=True)
        store_fn(o * lax.rsqrt(ms + EPS) * subg * (1.0 - LAM_INIT))


def _head_scores(q, kk, ckt):
    lo_mask = lax.broadcasted_iota(jnp.int32, (1, V_DIM), 1) < HEAD_DIM
    zero = jnp.zeros_like(q)
    out = []
    for qc in (jnp.where(lo_mask, q, zero), jnp.where(lo_mask, zero, q)):
        s = _scores(qc, kk)
        if ckt is not None:
            s = jnp.concatenate([s, jnp.dot(qc, ckt, preferred_element_type=F32)],
                                axis=1)
        out.append(s)
    return out


def _store_head(o_ref, b, hd, o):
    o_ref[b, :, hd * V_DIM:(hd + 1) * V_DIM] = o.astype(o_ref.dtype)


def _attn_kernel(lam_ref, subg_ref, q_ref, k_ref, v_ref, *rest, has_ctx):
    n_cast = (len(rest) - (3 if has_ctx else 1)) // 2
    if has_ctx:
        ckt_ref, cv_ref = rest[:2]
        rest = rest[2:]
    cast_in = rest[:n_cast]
    o_ref = rest[n_cast]
    cast_out = rest[n_cast + 1:2 * n_cast + 1]
    nb, heads, s_self, _ = k_ref.shape

    for src, dst in zip(cast_in, cast_out):
        dst[...] = src[...].astype(BF16)

    lam = _lam_value(lam_ref)
    subg = subg_ref[...]

    def head_values(b, hd):
        v = v_ref[b, hd].astype(BF16)
        if has_ctx:
            v = jnp.concatenate([v, cv_ref[b, hd].astype(BF16)], axis=0)
        return jnp.concatenate([v, jnp.ones_like(v)], axis=1)

    def head_scores(b, hd):
        ckt = ckt_ref[b, hd].astype(BF16) if has_ctx else None
        return _head_scores(q_ref[b, hd], k_ref[b, hd].astype(BF16), ckt)

    _attend_units(
        [(functools.partial(head_scores, b, hd), functools.partial(head_values, b, hd),
          functools.partial(_store_head, o_ref, b, hd))
         for b in range(nb) for hd in range(heads)], lam, subg)


def _attention(lamv, subg2, q, k, v, ctx, nb, tq, cast=()):
    b, nh, s, _ = q.shape
    has_ctx = ctx is not None
    nt = s // tq
    steps = (b // nb) * nt
    qspec = pl.BlockSpec((nb, nh, tq, V_DIM), lambda i, t: (i, 0, t, 0))
    kvspec = pl.BlockSpec((nb, nh, s, V_DIM), lambda i, t: (i, 0, 0, 0))
    in_specs = [
        pl.BlockSpec((4, HEAD_DIM), lambda i, t: (0, 0)),
        pl.BlockSpec((1, V_DIM), lambda i, t: (0, 0)),
        qspec, kvspec, kvspec,
    ]
    args = [lamv, subg2, q, k, v]
    if has_ctx:
        p = ctx[1].shape[2]
        in_specs += [pl.BlockSpec((nb, nh, V_DIM, p), lambda i, t: (i, 0, 0, 0)),
                     pl.BlockSpec((nb, nh, p, V_DIM), lambda i, t: (i, 0, 0, 0))]
        args += list(ctx)
    cast_specs = [pl.BlockSpec((w.shape[0] // steps, w.shape[1]),
                               lambda i, t: (i * nt + t, 0)) for w in cast]
    outs = pl.pallas_call(
        functools.partial(_attn_kernel, has_ctx=has_ctx),
        out_shape=(jax.ShapeDtypeStruct((b, s, nh * V_DIM), BF16),
                   *(jax.ShapeDtypeStruct(w.shape, BF16) for w in cast)),
        grid=(b // nb, nt),
        in_specs=in_specs + cast_specs,
        out_specs=(pl.BlockSpec((nb, tq, nh * V_DIM), lambda i, t: (i, t, 0)),
                   *cast_specs),
        compiler_params=_params(2),
        name="diff_attn_ctx" if has_ctx else "diff_attn",
    )(*args, *cast)
    return outs


def _shifted_rows(gs_ref, slot, g, seq):
    rows, w = g.shape
    pitch = seq + SUBLANES
    for s in range(rows // seq):
        base = SUBLANES + s * pitch
        gs_ref[slot, base:base + seq, :w] = g[s * seq:(s + 1) * seq]
    shifted = []
    for off in (-1, 1):
        parts = [gs_ref[slot, SUBLANES + s * pitch + off:
                        SUBLANES + s * pitch + off + seq, :w]
                 for s in range(rows // seq)]
        shifted.append(parts[0] if len(parts) == 1 else jnp.concatenate(parts, axis=0))
    return shifted


def _mixffn_kernel(x_ref, yf_ref, o_ref, mod_ref, n2_ref, wo_ref, wg_ref, wu_ref,
                   cw_ref, cb_ref, wd_ref, y_ref, h2_ref, act_ref, gs_ref, *, seq,
                   mod_row):
    mod = _mod_rows(mod_ref, mod_row(pl.program_id(0)))
    rows = x_ref.shape[0]

    mix = (jnp.dot(yf_ref[...], wo_ref[:D_FOURIER, :].astype(BF16),
                   preferred_element_type=F32)
           + jnp.dot(o_ref[...], wo_ref[D_FOURIER:, :].astype(BF16),
                     preferred_element_type=F32))
    x1 = x_ref[...] + mod[2] * mix
    y_ref[...] = x1
    h2 = _rms_rows(x1, n2_ref[...]) * (1.0 + mod[4]) + mod[3]
    h2_ref[...] = h2.astype(BF16)
    for slot in range(gs_ref.shape[0]):
        for s in range(rows // seq + 1):
            r0 = s * (seq + SUBLANES)
            gs_ref[slot, r0:r0 + SUBLANES, :] = jnp.zeros(
                (SUBLANES, gs_ref.shape[2]), F32)

    h2 = h2_ref[...]
    cw = cw_ref[...]
    cb = cb_ref[...]
    w = MXU_COLS
    for n, c0 in enumerate(range(0, D_FF, w)):
        g = jnp.dot(h2, wg_ref[:, c0:c0 + w], preferred_element_type=F32)
        u = jnp.dot(h2, wu_ref[:, c0:c0 + w], preferred_element_type=F32)
        g_prev, g_next = _shifted_rows(gs_ref, n % gs_ref.shape[0], g, seq)
        gc = (g * cw[1:2, c0:c0 + w] + g_prev * cw[0:1, c0:c0 + w]
              + g_next * cw[2:3, c0:c0 + w] + cb[:, c0:c0 + w])
        hg = 0.5 * gc
        act_ref[:, c0:c0 + w] = ((hg * u) * (1.0 + jnp.tanh(hg))).astype(BF16)
    y_ref[...] += mod[5] * jnp.dot(act_ref[...], wd_ref[...],
                                     preferred_element_type=F32)


def _mixffn(x2, yf2, o2, mod, mod_row, n2, w_out, w_gate, w_up, conv_w, conv_b2,
            w_down, seq):
    m, d = x2.shape
    tm = FFN_ROWS
    row = lambda i: (i, 0)
    const = lambda shape: pl.BlockSpec(shape, lambda i: (0, 0))
    stage_rows = SUBLANES + (tm // seq) * (seq + SUBLANES)
    return pl.pallas_call(
        functools.partial(_mixffn_kernel, seq=seq, mod_row=mod_row),
        out_shape=jax.ShapeDtypeStruct((m, d), F32),
        grid=(m // tm,),
        in_specs=[
            pl.BlockSpec((tm, d), row),
            pl.BlockSpec((tm, D_FOURIER), row),
            pl.BlockSpec((tm, D_ATTN), row),
            pl.BlockSpec((N_MOD, MOD_ROWS, d), lambda i: (0, 0, 0)),
            const((1, d)), const((d, d)), const((d, D_FF)), const((d, D_FF)),
            const((3, D_FF)), const((1, D_FF)), const((D_FF, d)),
        ],
        out_specs=pl.BlockSpec((tm, d), row),
        scratch_shapes=[
            pltpu.VMEM((tm, d), BF16),
            pltpu.VMEM((tm, D_FF), BF16),
            pltpu.VMEM((2, stage_rows, MXU_COLS), F32),
        ],
        compiler_params=_params(1),
        name="mix_ffn",
    )(x2, yf2, o2, mod, n2, w_out, w_gate, w_up, conv_w, conv_b2, w_down)


def _rope_tables(n_tokens):
    rows = n_tokens // GRID_W
    row = np.repeat(np.arange(rows, dtype=np.float64), GRID_W)
    col = np.tile(np.arange(GRID_W, dtype=np.float64), rows)
    n_freq = HEAD_DIM // 4
    inv = ROPE_THETA ** (-np.arange(n_freq, dtype=np.float64) / n_freq)
    ang_r = row[:, None] * inv
    ang_c = col[:, None] * inv
    zero = np.zeros_like(ang_r)
    cos64 = np.concatenate([np.cos(ang_r)] * 2 + [np.cos(ang_c)] * 2, axis=-1)
    sdn64 = np.concatenate([-np.sin(ang_r), zero, -np.sin(ang_c), zero], axis=-1)
    sup64 = np.concatenate([zero, np.sin(ang_r), zero, np.sin(ang_c)], axis=-1)
    two = lambda t: jnp.asarray(np.concatenate([t, t], axis=-1).astype(np.float32))
    return two(cos64), two(sdn64), two(sup64)


def kernel(x_prompt, x_sample, cache_k, cache_v, c, c_ctx, norm1_g, norm2_g, w_ada,
           b_ada, w_in, q_norm_g, k_norm_g, lam_q1, lam_k1, lam_q2, lam_k2, subln_g,
           w_out, w_gate, w_up, conv_w, conv_b, w_down):
    bp, sp, d = x_prompt.shape
    bs, ss, _ = x_sample.shape
    assert FFN_ROWS % sp == 0 and FFN_ROWS == ss

    mod = _ada(c_ctx, c, w_ada[0], b_ada[0])

    n1, n2, subg2 = norm1_g[0][None, :], norm2_g[0][None, :], subln_g[0][None, :]
    qkg = jnp.tile(jnp.concatenate([q_norm_g, k_norm_g], axis=0),
                   (1, MXU_COLS // HEAD_DIM))
    lamv = jnp.stack([lam_q1[0], lam_k1[0], lam_q2[0], lam_k2[0]])
    row_p = lambda i: MOD_CTX_ROW
    row_s = lambda i: MOD_REQ_ROW0 + i
    p = cache_k.shape[3]
    ck_t = jnp.transpose(cache_k[:, 0], (0, 1, 3, 4, 2)).reshape(bs, N_HEADS, V_DIM, p)
    ctx = (ck_t, cache_v[:, 0])

    yf_p, o_p, k_p, v_p = _inproj(x_prompt, mod, row_p, n1, w_in[0], qkg, None,
                                  FFN_ROWS // sp, F32, attn=(lamv, subg2))
    yf_s, q_s, k_s, v_s = _inproj(x_sample, mod, row_s, n1, w_in[0], qkg,
                                  _rope_tables(ss), FFN_ROWS // ss, BF16)
    o_s, wg_b, wu_b, wd_b = _attention(lamv, subg2, q_s, k_s, v_s, ctx, 1, ATTN_TQ,
                                       cast=(w_gate[0], w_up[0], w_down[0]))

    def mixer(x, yf, o, mod_row):
        b, s, _ = x.shape
        y = _mixffn(x.reshape(b * s, d), yf.reshape(b * s, D_FOURIER),
                    o.reshape(b * s, D_ATTN), mod, mod_row, n2, w_out[0], wg_b, wu_b,
                    conv_w[0], conv_b[0][None, :], wd_b, s)
        return y.reshape(b, s, d)

    y_p = mixer(x_prompt, yf_p, o_p, row_p)
    y_s = mixer(x_sample, yf_s, o_s, row_s)

    new_k = jnp.transpose(k_p.reshape(bp, 1, N_HEADS, 2, HEAD_DIM, sp),
                          (0, 1, 2, 5, 3, 4))
    new_v = v_p.reshape(bp, 1, N_HEADS, sp, V_DIM)
    return (y_p, y_s, new_k, new_v)
```

```python
import functools
import math

import jax
import jax.numpy as jnp
import numpy as np
from jax import lax
from jax.experimental import pallas as pl
from jax.experimental.pallas import tpu as pltpu

F32 = jnp.float32
BF16 = jnp.bfloat16

D_MODEL = 1024
GRID_W = 64
N_HEADS = 6
HEAD_DIM = 64
V_DIM = 2 * HEAD_DIM
D_FOURIER = 256
FOURIER_HEAD_DIM = 64
D_QK = N_HEADS * V_DIM
D_ATTN = N_HEADS * V_DIM
D_IN = D_FOURIER + 2 * D_QK + D_ATTN
D_FF = 2816
ROPE_THETA = 10000.0
EPS = 1e-6
LAM_INIT = 0.8 - 0.6 * math.exp(-0.3 * 0)
Q_SCALE = HEAD_DIM ** -0.5 * math.log2(math.e)
N_MOD = 6
MOD_ROWS = 16
MOD_CTX_ROW = 0
MOD_REQ_ROW0 = 8
ADA_K_ROWS = 256

SUBLANES = 8
MXU_COLS = 256
assert D_FF % MXU_COLS == 0
FFN_ROWS = 1024
ATTN_TQ = 512
VMEM_LIMIT = 58 * 1024 * 1024


def _params(n_axes):
    return pltpu.CompilerParams(
        dimension_semantics=("arbitrary",) * n_axes,
        vmem_limit_bytes=VMEM_LIMIT,
    )


def _ada_kernel(ctx_ref, c_ref, w_ref, b_ref, o_ref, acc_ref):
    j = pl.program_id(0)
    c = jnp.concatenate(
        [jnp.broadcast_to(ctx_ref[...], (SUBLANES, ctx_ref.shape[1])), c_ref[...]], axis=0)
    s = c * jax.nn.sigmoid(c)
    part = jnp.dot(s.astype(BF16), w_ref[...].astype(BF16), preferred_element_type=F32)

    @pl.when(j == 0)
    def _():
        acc_ref[...] = part + b_ref[...]

    @pl.when(j > 0)
    def _():
        acc_ref[...] += part

    @pl.when(j == pl.num_programs(0) - 1)
    def _():
        for k in range(N_MOD):
            o_ref[k] = acc_ref[:, k * D_MODEL:(k + 1) * D_MODEL]


def _ada(c_ctx, c, w_ada, b_ada):
    d, n = w_ada.shape
    tk = ADA_K_ROWS
    assert c.shape[0] == MOD_ROWS - SUBLANES
    return pl.pallas_call(
        _ada_kernel,
        out_shape=jax.ShapeDtypeStruct((N_MOD, MOD_ROWS, D_MODEL), F32),
        grid=(d // tk,),
        in_specs=[
            pl.BlockSpec((1, tk), lambda j: (0, j)),
            pl.BlockSpec((c.shape[0], tk), lambda j: (0, j)),
            pl.BlockSpec((tk, n), lambda j: (j, 0)),
            pl.BlockSpec((1, n), lambda j: (0, 0)),
        ],
        out_specs=pl.BlockSpec((N_MOD, MOD_ROWS, D_MODEL), lambda j: (0, 0, 0)),
        scratch_shapes=[pltpu.VMEM((MOD_ROWS, n), F32)],
        compiler_params=_params(1),
        name="ada_mod",
    )(c_ctx.reshape(1, d), c, w_ada, b_ada.reshape(1, n))


def _mod_rows(mod_ref, row):
    return [mod_ref[k, pl.ds(row, 1), :] for k in range(N_MOD)]


def _rms_rows(x, gain):
    ms = jnp.mean(x * x, axis=-1, keepdims=True)
    return x * lax.rsqrt(ms + EPS) * gain


def _group_rmsnorm(x, gain, ones_blk):
    ss = jnp.dot((x * x).astype(BF16), ones_blk, preferred_element_type=F32)
    return x * lax.rsqrt(ss * (1.0 / HEAD_DIM) + EPS) * gain


def _rope(x, cos, sin_dn, sin_up):
    return (x * cos + pltpu.roll(x, V_DIM - 16, 1) * sin_dn
            + pltpu.roll(x, 16, 1) * sin_up)


def _inproj_kernel(x_ref, mod_ref, n1_ref, w_ref, qkg_ref, ones_ref, cc_ref,
                   sc_ref, cs_ref, ss_ref, *rest, use_rope, attend, kv_dtype, mod_row):
    if use_rope:
        cos_ref, sdn_ref, sup_ref = rest[:3]
        rest = rest[3:]
    if attend:
        lam_ref, subg_ref, yf_ref, o_ref, k_ref, v_ref = rest
        lam = _lam_value(lam_ref)
        ones_v = jnp.ones((x_ref.shape[1], V_DIM), BF16)
    else:
        yf_ref, q_ref, k_ref, v_ref = rest
    nb, seq, d = x_ref.shape
    x = x_ref[...].reshape(nb * seq, d)
    mod = _mod_rows(mod_ref, mod_row(pl.program_id(0)))
    h = (_rms_rows(x, n1_ref[...]) * (1.0 + mod[1]) + mod[0]).astype(BF16)

    def project(c0):
        return jnp.dot(h, w_ref[:, c0:c0 + MXU_COLS].astype(BF16),
                       preferred_element_type=F32)

    def fourier_mix():
        xf = project(0).astype(BF16)
        xc = jnp.dot(xf, cc_ref[...], preferred_element_type=F32).astype(BF16)
        xs = jnp.dot(xf, sc_ref[...], preferred_element_type=F32).astype(BF16)
        for b in range(nb):
            r = slice(b * seq, (b + 1) * seq)
            yf = (jnp.dot(cs_ref[...], xc[r], preferred_element_type=F32)
                  - jnp.dot(ss_ref[...], xs[r], preferred_element_type=F32))
            yf_ref[b] = yf.astype(yf_ref.dtype)

    def store_heads(out_ref, head0, y, dtype):
        for hh in range(y.shape[1] // V_DIM):
            for b in range(nb):
                out_ref[b, head0 + hh] = y[b * seq:(b + 1) * seq,
                                           hh * V_DIM:(hh + 1) * V_DIM].astype(dtype)

    ones_blk = ones_ref[...]
    def project_pair(pair):
        c0 = D_FOURIER + pair * MXU_COLS
        return project(c0), project(c0 + D_QK), project(c0 + 2 * D_QK)

    n_pairs = N_HEADS // 2
    lookahead = 1 if attend else 0
    done = [project_pair(p) for p in range(lookahead)]
    for pair in range(n_pairs):
        if pair + lookahead < n_pairs:
            done.append(project_pair(pair + lookahead))
        zq, zk, zv = done[pair]
        q = _group_rmsnorm(zq, qkg_ref[0:1, :], ones_blk)
        k = _group_rmsnorm(zk, qkg_ref[1:2, :], ones_blk)
        for hh in range(2):
            ls = slice(hh * V_DIM, (hh + 1) * V_DIM)
            qh, kh = q[:, ls], k[:, ls]
            if use_rope:
                qh = _rope(qh, cos_ref[...], sdn_ref[...], sup_ref[...])
                kh = _rope(kh, cos_ref[...], sdn_ref[...], sup_ref[...])
            hd = 2 * pair + hh
            qs = (qh * Q_SCALE).astype(BF16)
            if not attend:
                store_heads(k_ref, hd, kh, kv_dtype)
                store_heads(q_ref, hd, qs, BF16)
                continue
            kb, vb = kh.astype(BF16), zv[:, ls].astype(BF16)
            units = []
            for b in range(nb):
                r = slice(b * seq, (b + 1) * seq)
                k_ref[b, hd] = kh[r].T.astype(kv_dtype)
                units.append((
                    functools.partial(_head_scores, qs[r], kb[r], None),
                    functools.partial(jnp.concatenate, [vb[r], ones_v], axis=1),
                    functools.partial(_store_head, o_ref, b, hd)))
            _attend_units(units, lam, subg_ref[...])
        store_heads(v_ref, 2 * pair, zv, kv_dtype)
    fourier_mix()


def _dft_tables(n, scale):
    idx = np.arange(n, dtype=np.int64)
    ang = ((idx[:, None] * idx[None, :]) % n).astype(np.float64) * (2.0 * math.pi / n)
    return ((np.cos(ang) * scale).astype(np.float32),
            (np.sin(ang) * scale).astype(np.float32))


def _fourier_tables(s):
    c64, s64 = _dft_tables(FOURIER_HEAD_DIM, (s * FOURIER_HEAD_DIM) ** -0.5)
    eye = np.eye(D_FOURIER // FOURIER_HEAD_DIM, dtype=np.float32)
    tabs = (np.kron(eye, c64), np.kron(eye, s64)) + _dft_tables(s, 1.0)
    return tuple(jnp.asarray(t).astype(BF16) for t in tabs)


def _inproj(x, mod, mod_row, n1, w_in, qkg, rope, nb, kv_dtype, attn=None):
    b, s, d = x.shape
    use_rope = rope is not None
    attend = attn is not None
    ones_blk = jnp.asarray(np.kron(np.eye(MXU_COLS // HEAD_DIM, dtype=np.float32),
                                   np.ones((HEAD_DIM, HEAD_DIM), np.float32))).astype(BF16)
    const = lambda shape: pl.BlockSpec(shape, lambda i: (0, 0))
    in_specs = [
        pl.BlockSpec((nb, s, d), lambda i: (i, 0, 0)),
        pl.BlockSpec((N_MOD, MOD_ROWS, d), lambda i: (0, 0, 0)),
        const((1, d)), const((d, D_IN)), const((2, MXU_COLS)),
        const((MXU_COLS, MXU_COLS)), const((D_FOURIER, D_FOURIER)),
        const((D_FOURIER, D_FOURIER)), const((s, s)), const((s, s)),
    ]
    args = [x, mod, n1, w_in, qkg, ones_blk, *_fourier_tables(s)]
    if use_rope:
        assert nb == 1
        in_specs += [const((s, V_DIM))] * 3
        args += list(rope)
    if attend:
        in_specs += [const((4, HEAD_DIM)), const((1, V_DIM))]
        args += list(attn)
    head_spec = pl.BlockSpec((nb, N_HEADS, s, V_DIM), lambda i: (i, 0, 0, 0))
    head_shape = (b, N_HEADS, s, V_DIM)
    if attend:
        second = (jax.ShapeDtypeStruct((b, s, D_ATTN), BF16),
                  pl.BlockSpec((nb, s, D_ATTN), lambda i: (i, 0, 0)))
        keys = (jax.ShapeDtypeStruct((b, N_HEADS, V_DIM, s), kv_dtype),
                pl.BlockSpec((nb, N_HEADS, V_DIM, s), lambda i: (i, 0, 0, 0)))
    else:
        second = (jax.ShapeDtypeStruct(head_shape, BF16), head_spec)
        keys = (jax.ShapeDtypeStruct(head_shape, kv_dtype), head_spec)
    return pl.pallas_call(
        functools.partial(_inproj_kernel, use_rope=use_rope, attend=attend,
                          kv_dtype=kv_dtype, mod_row=mod_row),
        out_shape=(
            jax.ShapeDtypeStruct((b, s, D_FOURIER), BF16),
            second[0], keys[0],
            jax.ShapeDtypeStruct(head_shape, kv_dtype),
        ),
        grid=(b // nb,),
        in_specs=in_specs,
        out_specs=(
            pl.BlockSpec((nb, s, D_FOURIER), lambda i: (i, 0, 0)),
            second[1], keys[1], head_spec,
        ),
        compiler_params=_params(1),
        name="inproj_rope" if use_rope else "inproj_attn" if attend else "inproj",
    )(*args)


def _scores(q, k):
    return lax.dot_general(q, k, (((1,), (1,)), ((), ())),
                           preferred_element_type=F32)


def _lam_value(lam_ref):
    lv = lam_ref[...]
    e1 = jnp.exp(jnp.sum(lv[0:1] * lv[1:2], axis=-1, keepdims=True))
    e2 = jnp.exp(jnp.sum(lv[2:3] * lv[3:4], axis=-1, keepdims=True))
    return e1 - e2 + LAM_INIT


def _attend_units(units, lam, subg):
    ahead = units[0][0]()
    for n, (_, values_fn, store_fn) in enumerate(units):
        scores = ahead
        if n + 1 < len(units):
            ahead = units[n + 1][0]()
        vv = values_fn()
        r0, r1 = [jnp.dot(jnp.exp2(s - s.max(axis=-1, keepdims=True)).astype(BF16), vv,
                          preferred_element_type=F32) for s in scores]
        o = r0[:, :V_DIM] / r0[:, V_DIM:] - lam * (r1[:, :V_DIM] / r1[:, V_DIM:])
        ms = jnp.mean(o * o, axis=-1, keepdims=True)
        store_fn(o * lax.rsqrt(ms + EPS) * subg * (1.0 - LAM_INIT))


def _head_scores(q, kk, ckt):
    lo_mask = lax.broadcasted_iota(jnp.int32, (1, V_DIM), 1) < HEAD_DIM
    zero = jnp.zeros_like(q)
    out = []
    for qc in (jnp.where(lo_mask, q, zero), jnp.where(lo_mask, zero, q)):
        s = _scores(qc, kk)
        if ckt is not None:
            s = jnp.concatenate([s, jnp.dot(qc, ckt, preferred_element_type=F32)],
                                axis=1)
        out.append(s)
    return out


def _store_head(o_ref, b, hd, o):
    o_ref[b, :, hd * V_DIM:(hd + 1) * V_DIM] = o.astype(o_ref.dtype)


def _attn_kernel(lam_ref, subg_ref, q_ref, k_ref, v_ref, *rest, has_ctx):
    n_cast = (len(rest) - (3 if has_ctx else 1)) // 2
    if has_ctx:
        ckt_ref, cv_ref = rest[:2]
        rest = rest[2:]
    cast_in = rest[:n_cast]
    o_ref = rest[n_cast]
    cast_out = rest[n_cast + 1:2 * n_cast + 1]
    nb, heads, s_self, _ = k_ref.shape

    for src, dst in zip(cast_in, cast_out):
        dst[...] = src[...].astype(BF16)

    lam = _lam_value(lam_ref)
    subg = subg_ref[...]

    def head_values(b, hd):
        v = v_ref[b, hd].astype(BF16)
        if has_ctx:
            v = jnp.concatenate([v, cv_ref[b, hd].astype(BF16)], axis=0)
        return jnp.concatenate([v, jnp.ones_like(v)], axis=1)

    def head_scores(b, hd):
        ckt = ckt_ref[b, hd].astype(BF16) if has_ctx else None
        return _head_scores(q_ref[b, hd], k_ref[b, hd].astype(BF16), ckt)

    _attend_units(
        [(functools.partial(head_scores, b, hd), functools.partial(head_values, b, hd),
          functools.partial(_store_head, o_ref, b, hd))
         for b in range(nb) for hd in range(heads)], lam, subg)


def _attention(lamv, subg2, q, k, v, ctx, nb, tq, cast=()):
    b, nh, s, _ = q.shape
    has_ctx = ctx is not None
    nt = s // tq
    steps = (b // nb) * nt
    qspec = pl.BlockSpec((nb, nh, tq, V_DIM), lambda i, t: (i, 0, t, 0))
    kvspec = pl.BlockSpec((nb, nh, s, V_DIM), lambda i, t: (i, 0, 0, 0))
    in_specs = [
        pl.BlockSpec((4, HEAD_DIM), lambda i, t: (0, 0)),
        pl.BlockSpec((1, V_DIM), lambda i, t: (0, 0)),
        qspec, kvspec, kvspec,
    ]
    args = [lamv, subg2, q, k, v]
    if has_ctx:
        p = ctx[1].shape[2]
        in_specs += [pl.BlockSpec((nb, nh, V_DIM, p), lambda i, t: (i, 0, 0, 0)),
                     pl.BlockSpec((nb, nh, p, V_DIM), lambda i, t: (i, 0, 0, 0))]
        args += list(ctx)
    cast_specs = [pl.BlockSpec((w.shape[0] // steps, w.shape[1]),
                               lambda i, t: (i * nt + t, 0)) for w in cast]
    outs = pl.pallas_call(
        functools.partial(_attn_kernel, has_ctx=has_ctx),
        out_shape=(jax.ShapeDtypeStruct((b, s, nh * V_DIM), BF16),
                   *(jax.ShapeDtypeStruct(w.shape, BF16) for w in cast)),
        grid=(b // nb, nt),
        in_specs=in_specs + cast_specs,
        out_specs=(pl.BlockSpec((nb, tq, nh * V_DIM), lambda i, t: (i, t, 0)),
                   *cast_specs),
        compiler_params=_params(2),
        name="diff_attn_ctx" if has_ctx else "diff_attn",
    )(*args, *cast)
    return outs


def _shifted_rows(gs_ref, slot, g, seq):
    rows, w = g.shape
    pitch = seq + SUBLANES
    for s in range(rows // seq):
        base = SUBLANES + s * pitch
        gs_ref[slot, base:base + seq, :w] = g[s * seq:(s + 1) * seq]
    shifted = []
    for off in (-1, 1):
        parts = [gs_ref[slot, SUBLANES + s * pitch + off:
                        SUBLANES + s * pitch + off + seq, :w]
                 for s in range(rows // seq)]
        shifted.append(parts[0] if len(parts) == 1 else jnp.concatenate(parts, axis=0))
    return shifted


def _mixffn_kernel(x_ref, yf_ref, o_ref, mod_ref, n2_ref, wo_ref, wg_ref, wu_ref,
                   cw_ref, cb_ref, wd_ref, y_ref, h2_ref, act_ref, gs_ref, *, seq,
                   mod_row):
    mod = _mod_rows(mod_ref, mod_row(pl.program_id(0)))
    rows = x_ref.shape[0]

    mix = (jnp.dot(yf_ref[...], wo_ref[:D_FOURIER, :].astype(BF16),
                   preferred_element_type=F32)
           + jnp.dot(o_ref[...], wo_ref[D_FOURIER:, :].astype(BF16),
                     preferred_element_type=F32))
    x1 = x_ref[...] + mod[2] * mix
    y_ref[...] = x1
    h2 = _rms_rows(x1, n2_ref[...]) * (1.0 + mod[4]) + mod[3]
    h2_ref[...] = h2.astype(BF16)
    for slot in range(gs_ref.shape[0]):
        for s in range(rows // seq + 1):
            r0 = s * (seq + SUBLANES)
            gs_ref[slot, r0:r0 + SUBLANES, :] = jnp.zeros(
                (SUBLANES, gs_ref.shape[2]), F32)

    h2 = h2_ref[...]
    cw = cw_ref[...]
    cb = cb_ref[...]
    w = MXU_COLS
    for n, c0 in enumerate(range(0, D_FF, w)):
        g = jnp.dot(h2, wg_ref[:, c0:c0 + w], preferred_element_type=F32)
        u = jnp.dot(h2, wu_ref[:, c0:c0 + w], preferred_element_type=F32)
        g_prev, g_next = _shifted_rows(gs_ref, n % gs_ref.shape[0], g, seq)
        gc = (g * cw[1:2, c0:c0 + w] + g_prev * cw[0:1, c0:c0 + w]
              + g_next * cw[2:3, c0:c0 + w] + cb[:, c0:c0 + w])
        hg = 0.5 * gc
        act_ref[:, c0:c0 + w] = ((hg * u) * (1.0 + jnp.tanh(hg))).astype(BF16)
    y_ref[...] += mod[5] * jnp.dot(act_ref[...], wd_ref[...],
                                     preferred_element_type=F32)


def _mixffn(x2, yf2, o2, mod, mod_row, n2, w_out, w_gate, w_up, conv_w, conv_b2,
            w_down, seq):
    m, d = x2.shape
    tm = FFN_ROWS
    row = lambda i: (i, 0)
    const = lambda shape: pl.BlockSpec(shape, lambda i: (0, 0))
    stage_rows = SUBLANES + (tm // seq) * (seq + SUBLANES)
    return pl.pallas_call(
        functools.partial(_mixffn_kernel, seq=seq, mod_row=mod_row),
        out_shape=jax.ShapeDtypeStruct((m, d), F32),
        grid=(m // tm,),
        in_specs=[
            pl.BlockSpec((tm, d), row),
            pl.BlockSpec((tm, D_FOURIER), row),
            pl.BlockSpec((tm, D_ATTN), row),
            pl.BlockSpec((N_MOD, MOD_ROWS, d), lambda i: (0, 0, 0)),
            const((1, d)), const((d, d)), const((d, D_FF)), const((d, D_FF)),
            const((3, D_FF)), const((1, D_FF)), const((D_FF, d)),
        ],
        out_specs=pl.BlockSpec((tm, d), row),
        scratch_shapes=[
            pltpu.VMEM((tm, d), BF16),
            pltpu.VMEM((tm, D_FF), BF16),
            pltpu.VMEM((2, stage_rows, MXU_COLS), F32),
        ],
        compiler_params=_params(1),
        name="mix_ffn",
    )(x2, yf2, o2, mod, n2, w_out, w_gate, w_up, conv_w, conv_b2, w_down)


def _rope_tables(n_tokens):
    rows = n_tokens // GRID_W
    row = np.repeat(np.arange(rows, dtype=np.float64), GRID_W)
    col = np.tile(np.arange(GRID_W, dtype=np.float64), rows)
    n_freq = HEAD_DIM // 4
    inv = ROPE_THETA ** (-np.arange(n_freq, dtype=np.float64) / n_freq)
    ang_r = row[:, None] * inv
    ang_c = col[:, None] * inv
    zero = np.zeros_like(ang_r)
    cos64 = np.concatenate([np.cos(ang_r)] * 2 + [np.cos(ang_c)] * 2, axis=-1)
    sdn64 = np.concatenate([-np.sin(ang_r), zero, -np.sin(ang_c), zero], axis=-1)
    sup64 = np.concatenate([zero, np.sin(ang_r), zero, np.sin(ang_c)], axis=-1)
    two = lambda t: jnp.asarray(np.concatenate([t, t], axis=-1).astype(np.float32))
    return two(cos64), two(sdn64), two(sup64)


def kernel(x_prompt, x_sample, cache_k, cache_v, c, c_ctx, norm1_g, norm2_g, w_ada,
           b_ada, w_in, q_norm_g, k_norm_g, lam_q1, lam_k1, lam_q2, lam_k2, subln_g,
           w_out, w_gate, w_up, conv_w, conv_b, w_down):
    bp, sp, d = x_prompt.shape
    bs, ss, _ = x_sample.shape
    assert FFN_ROWS % sp == 0 and FFN_ROWS == ss

    mod = _ada(c_ctx, c, w_ada[0], b_ada[0])

    n1, n2, subg2 = norm1_g[0][None, :], norm2_g[0][None, :], subln_g[0][None, :]
    qkg = jnp.tile(jnp.concatenate([q_norm_g, k_norm_g], axis=0),
                   (1, MXU_COLS // HEAD_DIM))
    lamv = jnp.stack([lam_q1[0], lam_k1[0], lam_q2[0], lam_k2[0]])
    row_p = lambda i: MOD_CTX_ROW
    row_s = lambda i: MOD_REQ_ROW0 + i
    p = cache_k.shape[3]
    ck_t = jnp.transpose(cache_k[:, 0], (0, 1, 3, 4, 2)).reshape(bs, N_HEADS, V_DIM, p)
    ctx = (ck_t, cache_v[:, 0])

    yf_p, o_p, k_p, v_p = _inproj(x_prompt, mod, row_p, n1, w_in[0], qkg, None,
                                  FFN_ROWS // sp, F32, attn=(lamv, subg2))
    yf_s, q_s, k_s, v_s = _inproj(x_sample, mod, row_s, n1, w_in[0], qkg,
                                  _rope_tables(ss), FFN_ROWS // ss, BF16)
    o_s, wg_b, wu_b, wd_b = _attention(lamv, subg2, q_s, k_s, v_s, ctx, 1, ATTN_TQ,
                                       cast=(w_gate[0], w_up[0], w_down[0]))

    def mixer(x, yf, o, mod_row):
        b, s, _ = x.shape
        y = _mixffn(x.reshape(b * s, d), yf.reshape(b * s, D_FOURIER),
                    o.reshape(b * s, D_ATTN), mod, mod_row, n2, w_out[0], wg_b, wu_b,
                    conv_w[0], conv_b[0][None, :], wd_b, s)
        return y.reshape(b, s, d)

    y_p = mixer(x_prompt, yf_p, o_p, row_p)
    y_s = mixer(x_sample, yf_s, o_s, row_s)

    new_k = jnp.transpose(k_p.reshape(bp, 1, N_HEADS, 2, HEAD_DIM, sp),
                          (0, 1, 2, 5, 3, 4))
    new_v = v_p.reshape(bp, 1, N_HEADS, sp, V_DIM)
    return (y_p, y_s, new_k, new_v)
```

```python
import functools
import math

import jax
import jax.numpy as jnp
import numpy as np
from jax import lax
from jax.experimental import pallas as pl
from jax.experimental.pallas import tpu as pltpu

F32 = jnp.float32
BF16 = jnp.bfloat16

D_MODEL = 1024
GRID_W = 64
N_HEADS = 6
HEAD_DIM = 64
V_DIM = 2 * HEAD_DIM
D_FOURIER = 256
FOURIER_HEAD_DIM = 64
D_QK = N_HEADS * V_DIM
D_ATTN = N_HEADS * V_DIM
D_IN = D_FOURIER + 2 * D_QK + D_ATTN
D_FF = 2816
ROPE_THETA = 10000.0
EPS = 1e-6
LAM_INIT = 0.8 - 0.6 * math.exp(-0.3 * 0)
Q_SCALE = HEAD_DIM ** -0.5 * math.log2(math.e)
N_MOD = 6
MOD_ROWS = 16
MOD_CTX_ROW = 0
MOD_REQ_ROW0 = 8
ADA_K_ROWS = 256

SUBLANES = 8
MXU_COLS = 256
assert D_FF % MXU_COLS == 0
FFN_ROWS = 1024
ATTN_TQ = 512
VMEM_LIMIT = 58 * 1024 * 1024


def _params(n_axes):
    return pltpu.CompilerParams(
        dimension_semantics=("arbitrary",) * n_axes,
        vmem_limit_bytes=VMEM_LIMIT,
    )


def _ada_kernel(ctx_ref, c_ref, w_ref, b_ref, o_ref, acc_ref):
    j = pl.program_id(0)
    c = jnp.concatenate(
        [jnp.broadcast_to(ctx_ref[...], (SUBLANES, ctx_ref.shape[1])), c_ref[...]], axis=0)
    s = c * jax.nn.sigmoid(c)
    part = jnp.dot(s.astype(BF16), w_ref[...].astype(BF16), preferred_element_type=F32)

    @pl.when(j == 0)
    def _():
        acc_ref[...] = part + b_ref[...]

    @pl.when(j > 0)
    def _():
        acc_ref[...] += part

    @pl.when(j == pl.num_programs(0) - 1)
    def _():
        for k in range(N_MOD):
            o_ref[k] = acc_ref[:, k * D_MODEL:(k + 1) * D_MODEL]


def _ada(c_ctx, c, w_ada, b_ada):
    d, n = w_ada.shape
    tk = ADA_K_ROWS
    assert c.shape[0] == MOD_ROWS - SUBLANES
    return pl.pallas_call(
        _ada_kernel,
        out_shape=jax.ShapeDtypeStruct((N_MOD, MOD_ROWS, D_MODEL), F32),
        grid=(d // tk,),
        in_specs=[
            pl.BlockSpec((1, tk), lambda j: (0, j)),
            pl.BlockSpec((c.shape[0], tk), lambda j: (0, j)),
            pl.BlockSpec((tk, n), lambda j: (j, 0)),
            pl.BlockSpec((1, n), lambda j: (0, 0)),
        ],
        out_specs=pl.BlockSpec((N_MOD, MOD_ROWS, D_MODEL), lambda j: (0, 0, 0)),
        scratch_shapes=[pltpu.VMEM((MOD_ROWS, n), F32)],
        compiler_params=_params(1),
        name="ada_mod",
    )(c_ctx.reshape(1, d), c, w_ada, b_ada.reshape(1, n))


def _mod_rows(mod_ref, row):
    return [mod_ref[k, pl.ds(row, 1), :] for k in range(N_MOD)]


def _rms_rows(x, gain):
    ms = jnp.mean(x * x, axis=-1, keepdims=True)
    return x * lax.rsqrt(ms + EPS) * gain


def _group_rmsnorm(x, gain, ones_blk):
    ss = jnp.dot((x * x).astype(BF16), ones_blk, preferred_element_type=F32)
    return x * lax.rsqrt(ss * (1.0 / HEAD_DIM) + EPS) * gain


def _rope(x, cos, sin_dn, sin_up):
    return (x * cos + pltpu.roll(x, V_DIM - 16, 1) * sin_dn
            + pltpu.roll(x, 16, 1) * sin_up)


def _inproj_kernel(x_ref, mod_ref, n1_ref, w_ref, qkg_ref, ones_ref, cc_ref,
                   sc_ref, cs_ref, ss_ref, *rest, use_rope, attend, kv_dtype, mod_row):
    if use_rope:
        cos_ref, sdn_ref, sup_ref = rest[:3]
        rest = rest[3:]
    if attend:
        lam_ref, subg_ref, yf_ref, o_ref, k_ref, v_ref = rest
        lam = _lam_value(lam_ref)
        ones_v = jnp.ones((x_ref.shape[1], V_DIM), BF16)
    else:
        yf_ref, q_ref, k_ref, v_ref = rest
    nb, seq, d = x_ref.shape
    x = x_ref[...].reshape(nb * seq, d)
    mod = _mod_rows(mod_ref, mod_row(pl.program_id(0)))
    h = (_rms_rows(x, n1_ref[...]) * (1.0 + mod[1]) + mod[0]).astype(BF16)

    def project(c0):
        return jnp.dot(h, w_ref[:, c0:c0 + MXU_COLS].astype(BF16),
                       preferred_element_type=F32)

    def fourier_mix():
        xf = project(0).astype(BF16)
        xc = jnp.dot(xf, cc_ref[...], preferred_element_type=F32).astype(BF16)
        xs = jnp.dot(xf, sc_ref[...], preferred_element_type=F32).astype(BF16)
        for b in range(nb):
            r = slice(b * seq, (b + 1) * seq)
            yf = (jnp.dot(cs_ref[...], xc[r], preferred_element_type=F32)
                  - jnp.dot(ss_ref[...], xs[r], preferred_element_type=F32))
            yf_ref[b] = yf.astype(yf_ref.dtype)

    def store_heads(out_ref, head0, y, dtype):
        for hh in range(y.shape[1] // V_DIM):
            for b in range(nb):
                out_ref[b, head0 + hh] = y[b * seq:(b + 1) * seq,
                                           hh * V_DIM:(hh + 1) * V_DIM].astype(dtype)

    ones_blk = ones_ref[...]
    def project_pair(pair):
        c0 = D_FOURIER + pair * MXU_COLS
        return project(c0), project(c0 + D_QK), project(c0 + 2 * D_QK)

    n_pairs = N_HEADS // 2
    lookahead = 1 if attend else 0
    done = [project_pair(p) for p in range(lookahead)]
    for pair in range(n_pairs):
        if pair + lookahead < n_pairs:
            done.append(project_pair(pair + lookahead))
        zq, zk, zv = done[pair]
        q = _group_rmsnorm(zq, qkg_ref[0:1, :], ones_blk)
        k = _group_rmsnorm(zk, qkg_ref[1:2, :], ones_blk)
        for hh in range(2):
            ls = slice(hh * V_DIM, (hh + 1) * V_DIM)
            qh, kh = q[:, ls], k[:, ls]
            if use_rope:
                qh = _rope(qh, cos_ref[...], sdn_ref[...], sup_ref[...])
                kh = _rope(kh, cos_ref[...], sdn_ref[...], sup_ref[...])
            hd = 2 * pair + hh
            qs = (qh * Q_SCALE).astype(BF16)
            kts = [kh[b * seq:(b + 1) * seq].T for b in range(nb)]
            for b in range(nb):
                k_ref[b, hd] = kts[b].astype(kv_dtype)
            if not attend:
                store_heads(q_ref, hd, qs, BF16)
                continue
            vb = zv[:, ls].astype(BF16)
            units = []
            for b in range(nb):
                r = slice(b * seq, (b + 1) * seq)
                units.append((
                    functools.partial(_head_scores, qs[r], kts[b].astype(BF16)),
                    functools.partial(jnp.concatenate, [vb[r], ones_v], axis=1),
                    functools.partial(_store_head, o_ref, b, hd)))
            _attend_units(units, lam, subg_ref[...])
        store_heads(v_ref, 2 * pair, zv, kv_dtype)
    fourier_mix()


def _dft_tables(n, scale):
    idx = np.arange(n, dtype=np.int64)
    ang = ((idx[:, None] * idx[None, :]) % n).astype(np.float64) * (2.0 * math.pi / n)
    return ((np.cos(ang) * scale).astype(np.float32),
            (np.sin(ang) * scale).astype(np.float32))


def _fourier_tables(s):
    c64, s64 = _dft_tables(FOURIER_HEAD_DIM, (s * FOURIER_HEAD_DIM) ** -0.5)
    eye = np.eye(D_FOURIER // FOURIER_HEAD_DIM, dtype=np.float32)
    tabs = (np.kron(eye, c64), np.kron(eye, s64)) + _dft_tables(s, 1.0)
    return tuple(jnp.asarray(t).astype(BF16) for t in tabs)


def _inproj(x, mod, mod_row, n1, w_in, qkg, rope, nb, kv_dtype, attn=None):
    b, s, d = x.shape
    use_rope = rope is not None
    attend = attn is not None
    ones_blk = jnp.asarray(np.kron(np.eye(MXU_COLS // HEAD_DIM, dtype=np.float32),
                                   np.ones((HEAD_DIM, HEAD_DIM), np.float32))).astype(BF16)
    const = lambda shape: pl.BlockSpec(shape, lambda i: (0, 0))
    in_specs = [
        pl.BlockSpec((nb, s, d), lambda i: (i, 0, 0)),
        pl.BlockSpec((N_MOD, MOD_ROWS, d), lambda i: (0, 0, 0)),
        const((1, d)), const((d, D_IN)), const((2, MXU_COLS)),
        const((MXU_COLS, MXU_COLS)), const((D_FOURIER, D_FOURIER)),
        const((D_FOURIER, D_FOURIER)), const((s, s)), const((s, s)),
    ]
    args = [x, mod, n1, w_in, qkg, ones_blk, *_fourier_tables(s)]
    if use_rope:
        assert nb == 1
        in_specs += [const((s, V_DIM))] * 3
        args += list(rope)
    if attend:
        in_specs += [const((4, HEAD_DIM)), const((1, V_DIM))]
        args += list(attn)
    head_spec = pl.BlockSpec((nb, N_HEADS, s, V_DIM), lambda i: (i, 0, 0, 0))
    head_shape = (b, N_HEADS, s, V_DIM)
    if attend:
        second = (jax.ShapeDtypeStruct((b, s, D_ATTN), BF16),
                  pl.BlockSpec((nb, s, D_ATTN), lambda i: (i, 0, 0)))
    else:
        second = (jax.ShapeDtypeStruct(head_shape, BF16), head_spec)
    keys = (jax.ShapeDtypeStruct((b, N_HEADS, V_DIM, s), kv_dtype),
            pl.BlockSpec((nb, N_HEADS, V_DIM, s), lambda i: (i, 0, 0, 0)))
    return pl.pallas_call(
        functools.partial(_inproj_kernel, use_rope=use_rope, attend=attend,
                          kv_dtype=kv_dtype, mod_row=mod_row),
        out_shape=(
            jax.ShapeDtypeStruct((b, s, D_FOURIER), BF16),
            second[0], keys[0],
            jax.ShapeDtypeStruct(head_shape, kv_dtype),
        ),
        grid=(b // nb,),
        in_specs=in_specs,
        out_specs=(
            pl.BlockSpec((nb, s, D_FOURIER), lambda i: (i, 0, 0)),
            second[1], keys[1], head_spec,
        ),
        compiler_params=_params(1),
        name="inproj_rope" if use_rope else "inproj_attn" if attend else "inproj",
    )(*args)


def _lam_value(lam_ref):
    lv = lam_ref[...]
    e1 = jnp.exp(jnp.sum(lv[0:1] * lv[1:2], axis=-1, keepdims=True))
    e2 = jnp.exp(jnp.sum(lv[2:3] * lv[3:4], axis=-1, keepdims=True))
    return e1 - e2 + LAM_INIT


def _attend_units(units, lam, subg):
    ahead = units[0][0]()
    for n, (_, values_fn, store_fn) in enumerate(units):
        scores = ahead
        if n + 1 < len(units):
            ahead = units[n + 1][0]()
        vv = values_fn()
        r0, r1 = [jnp.dot(jnp.exp2(s - s.max(axis=-1, keepdims=True)).astype(BF16), vv,
                          preferred_element_type=F32) for s in scores]
        o = r0[:, :V_DIM] / r0[:, V_DIM:] - lam * (r1[:, :V_DIM] / r1[:, V_DIM:])
        ms = jnp.mean(o * o, axis=-1, keepdims=True)
        store_fn(o * lax.rsqrt(ms + EPS) * subg * (1.0 - LAM_INIT))


def _head_scores(q, kt):
    lo_mask = lax.broadcasted_iota(jnp.int32, (1, V_DIM), 1) < HEAD_DIM
    zero = jnp.zeros_like(q)
    return [jnp.dot(qc, kt, preferred_element_type=F32)
            for qc in (jnp.where(lo_mask, q, zero), jnp.where(lo_mask, zero, q))]


def _store_head(o_ref, b, hd, o):
    o_ref[b, :, hd * V_DIM:(hd + 1) * V_DIM] = o.astype(o_ref.dtype)


def _attn_kernel(lam_ref, subg_ref, q_ref, k_ref, v_ref, *rest, has_ctx):
    n_cast = (len(rest) - (3 if has_ctx else 1)) // 2
    if has_ctx:
        ckt_ref, cv_ref = rest[:2]
        rest = rest[2:]
    cast_in = rest[:n_cast]
    o_ref = rest[n_cast]
    cast_out = rest[n_cast + 1:2 * n_cast + 1]
    nb, heads = q_ref.shape[:2]

    for src, dst in zip(cast_in, cast_out):
        dst[...] = src[...].astype(BF16)

    lam = _lam_value(lam_ref)
    subg = subg_ref[...]

    def head_values(b, hd):
        v = v_ref[b, hd].astype(BF16)
        if has_ctx:
            v = jnp.concatenate([v, cv_ref[b, hd].astype(BF16)], axis=0)
        return jnp.concatenate([v, jnp.ones_like(v)], axis=1)

    def head_scores(b, hd):
        kt = k_ref[b, hd].astype(BF16)
        if has_ctx:
            kt = jnp.concatenate([kt, ckt_ref[b, hd].astype(BF16)], axis=1)
        return _head_scores(q_ref[b, hd], kt)

    _attend_units(
        [(functools.partial(head_scores, b, hd), functools.partial(head_values, b, hd),
          functools.partial(_store_head, o_ref, b, hd))
         for b in range(nb) for hd in range(heads)], lam, subg)


def _attention(lamv, subg2, q, k, v, ctx, nb, tq, cast=()):
    b, nh, s, _ = q.shape
    has_ctx = ctx is not None
    nt = s // tq
    steps = (b // nb) * nt
    qspec = pl.BlockSpec((nb, nh, tq, V_DIM), lambda i, t: (i, 0, t, 0))
    ktspec = pl.BlockSpec((nb, nh, V_DIM, s), lambda i, t: (i, 0, 0, 0))
    vspec = pl.BlockSpec((nb, nh, s, V_DIM), lambda i, t: (i, 0, 0, 0))
    in_specs = [
        pl.BlockSpec((4, HEAD_DIM), lambda i, t: (0, 0)),
        pl.BlockSpec((1, V_DIM), lambda i, t: (0, 0)),
        qspec, ktspec, vspec,
    ]
    args = [lamv, subg2, q, k, v]
    if has_ctx:
        p = ctx[1].shape[2]
        in_specs += [pl.BlockSpec((nb, nh, V_DIM, p), lambda i, t: (i, 0, 0, 0)),
                     pl.BlockSpec((nb, nh, p, V_DIM), lambda i, t: (i, 0, 0, 0))]
        args += list(ctx)
    cast_specs = [pl.BlockSpec((w.shape[0] // steps, w.shape[1]),
                               lambda i, t: (i * nt + t, 0)) for w in cast]
    outs = pl.pallas_call(
        functools.partial(_attn_kernel, has_ctx=has_ctx),
        out_shape=(jax.ShapeDtypeStruct((b, s, nh * V_DIM), BF16),
                   *(jax.ShapeDtypeStruct(w.shape, BF16) for w in cast)),
        grid=(b // nb, nt),
        in_specs=in_specs + cast_specs,
        out_specs=(pl.BlockSpec((nb, tq, nh * V_DIM), lambda i, t: (i, t, 0)),
                   *cast_specs),
        compiler_params=_params(2),
        name="diff_attn_ctx" if has_ctx else "diff_attn",
    )(*args, *cast)
    return outs


def _shifted_rows(gs_ref, slot, g, seq):
    rows, w = g.shape
    pitch = seq + SUBLANES
    for s in range(rows // seq):
        base = SUBLANES + s * pitch
        gs_ref[slot, base:base + seq, :w] = g[s * seq:(s + 1) * seq]
    shifted = []
    for off in (-1, 1):
        parts = [gs_ref[slot, SUBLANES + s * pitch + off:
                        SUBLANES + s * pitch + off + seq, :w]
                 for s in range(rows // seq)]
        shifted.append(parts[0] if len(parts) == 1 else jnp.concatenate(parts, axis=0))
    return shifted


def _mixffn_kernel(x_ref, yf_ref, o_ref, mod_ref, n2_ref, wo_ref, wg_ref, wu_ref,
                   cw_ref, cb_ref, wd_ref, y_ref, h2_ref, act_ref, gs_ref, *, seq,
                   mod_row):
    mod = _mod_rows(mod_ref, mod_row(pl.program_id(0)))
    rows = x_ref.shape[0]

    mix = (jnp.dot(yf_ref[...], wo_ref[:D_FOURIER, :].astype(BF16),
                   preferred_element_type=F32)
           + jnp.dot(o_ref[...], wo_ref[D_FOURIER:, :].astype(BF16),
                     preferred_element_type=F32))
    x1 = x_ref[...] + mod[2] * mix
    y_ref[...] = x1
    h2 = _rms_rows(x1, n2_ref[...]) * (1.0 + mod[4]) + mod[3]
    h2_ref[...] = h2.astype(BF16)
    for slot in range(gs_ref.shape[0]):
        for s in range(rows // seq + 1):
            r0 = s * (seq + SUBLANES)
            gs_ref[slot, r0:r0 + SUBLANES, :] = jnp.zeros(
                (SUBLANES, gs_ref.shape[2]), F32)

    h2 = h2_ref[...]
    cw = cw_ref[...]
    cb = cb_ref[...]
    w = MXU_COLS
    for n, c0 in enumerate(range(0, D_FF, w)):
        g = jnp.dot(h2, wg_ref[:, c0:c0 + w], preferred_element_type=F32)
        u = jnp.dot(h2, wu_ref[:, c0:c0 + w], preferred_element_type=F32)
        g_prev, g_next = _shifted_rows(gs_ref, n % gs_ref.shape[0], g, seq)
        gc = (g * cw[1:2, c0:c0 + w] + g_prev * cw[0:1, c0:c0 + w]
              + g_next * cw[2:3, c0:c0 + w] + cb[:, c0:c0 + w])
        hg = 0.5 * gc
        act_ref[:, c0:c0 + w] = ((hg * u) * (1.0 + jnp.tanh(hg))).astype(BF16)
    y_ref[...] += mod[5] * jnp.dot(act_ref[...], wd_ref[...],
                                     preferred_element_type=F32)


def _mixffn(x2, yf2, o2, mod, mod_row, n2, w_out, w_gate, w_up, conv_w, conv_b2,
            w_down, seq):
    m, d = x2.shape
    tm = FFN_ROWS
    row = lambda i: (i, 0)
    const = lambda shape: pl.BlockSpec(shape, lambda i: (0, 0))
    stage_rows = SUBLANES + (tm // seq) * (seq + SUBLANES)
    return pl.pallas_call(
        functools.partial(_mixffn_kernel, seq=seq, mod_row=mod_row),
        out_shape=jax.ShapeDtypeStruct((m, d), F32),
        grid=(m // tm,),
        in_specs=[
            pl.BlockSpec((tm, d), row),
            pl.BlockSpec((tm, D_FOURIER), row),
            pl.BlockSpec((tm, D_ATTN), row),
            pl.BlockSpec((N_MOD, MOD_ROWS, d), lambda i: (0, 0, 0)),
            const((1, d)), const((d, d)), const((d, D_FF)), const((d, D_FF)),
            const((3, D_FF)), const((1, D_FF)), const((D_FF, d)),
        ],
        out_specs=pl.BlockSpec((tm, d), row),
        scratch_shapes=[
            pltpu.VMEM((tm, d), BF16),
            pltpu.VMEM((tm, D_FF), BF16),
            pltpu.VMEM((2, stage_rows, MXU_COLS), F32),
        ],
        compiler_params=_params(1),
        name="mix_ffn",
    )(x2, yf2, o2, mod, n2, w_out, w_gate, w_up, conv_w, conv_b2, w_down)


def _rope_tables(n_tokens):
    rows = n_tokens // GRID_W
    row = np.repeat(np.arange(rows, dtype=np.float64), GRID_W)
    col = np.tile(np.arange(GRID_W, dtype=np.float64), rows)
    n_freq = HEAD_DIM // 4
    inv = ROPE_THETA ** (-np.arange(n_freq, dtype=np.float64) / n_freq)
    ang_r = row[:, None] * inv
    ang_c = col[:, None] * inv
    zero = np.zeros_like(ang_r)
    cos64 = np.concatenate([np.cos(ang_r)] * 2 + [np.cos(ang_c)] * 2, axis=-1)
    sdn64 = np.concatenate([-np.sin(ang_r), zero, -np.sin(ang_c), zero], axis=-1)
    sup64 = np.concatenate([zero, np.sin(ang_r), zero, np.sin(ang_c)], axis=-1)
    two = lambda t: jnp.asarray(np.concatenate([t, t], axis=-1).astype(np.float32))
    return two(cos64), two(sdn64), two(sup64)


def kernel(x_prompt, x_sample, cache_k, cache_v, c, c_ctx, norm1_g, norm2_g, w_ada,
           b_ada, w_in, q_norm_g, k_norm_g, lam_q1, lam_k1, lam_q2, lam_k2, subln_g,
           w_out, w_gate, w_up, conv_w, conv_b, w_down):
    bp, sp, d = x_prompt.shape
    bs, ss, _ = x_sample.shape
    assert FFN_ROWS % sp == 0 and FFN_ROWS == ss

    mod = _ada(c_ctx, c, w_ada[0], b_ada[0])

    n1, n2, subg2 = norm1_g[0][None, :], norm2_g[0][None, :], subln_g[0][None, :]
    qkg = jnp.tile(jnp.concatenate([q_norm_g, k_norm_g], axis=0),
                   (1, MXU_COLS // HEAD_DIM))
    lamv = jnp.stack([lam_q1[0], lam_k1[0], lam_q2[0], lam_k2[0]])
    row_p = lambda i: MOD_CTX_ROW
    row_s = lambda i: MOD_REQ_ROW0 + i
    p = cache_k.shape[3]
    ck_t = jnp.transpose(cache_k[:, 0], (0, 1, 3, 4, 2)).reshape(bs, N_HEADS, V_DIM, p)
    ctx = (ck_t, cache_v[:, 0])

    yf_p, o_p, k_p, v_p = _inproj(x_prompt, mod, row_p, n1, w_in[0], qkg, None,
                                  FFN_ROWS // sp, F32, attn=(lamv, subg2))
    yf_s, q_s, k_s, v_s = _inproj(x_sample, mod, row_s, n1, w_in[0], qkg,
                                  _rope_tables(ss), FFN_ROWS // ss, BF16)
    o_s, wg_b, wu_b, wd_b = _attention(lamv, subg2, q_s, k_s, v_s, ctx, 1, ATTN_TQ,
                                       cast=(w_gate[0], w_up[0], w_down[0]))

    def mixer(x, yf, o, mod_row):
        b, s, _ = x.shape
        y = _mixffn(x.reshape(b * s, d), yf.reshape(b * s, D_FOURIER),
                    o.reshape(b * s, D_ATTN), mod, mod_row, n2, w_out[0], wg_b, wu_b,
                    conv_w[0], conv_b[0][None, :], wd_b, s)
        return y.reshape(b, s, d)

    y_p = mixer(x_prompt, yf_p, o_p, row_p)
    y_s = mixer(x_sample, yf_s, o_s, row_s)

    new_k = jnp.transpose(k_p.reshape(bp, 1, N_HEADS, 2, HEAD_DIM, sp),
                          (0, 1, 2, 5, 3, 4))
    new_v = v_p.reshape(bp, 1, N_HEADS, sp, V_DIM)
    return (y_p, y_s, new_k, new_v)
```

```python
import functools
import math

import jax
import jax.numpy as jnp
import numpy as np
from jax import lax
from jax.experimental import pallas as pl
from jax.experimental.pallas import tpu as pltpu

F32 = jnp.float32
BF16 = jnp.bfloat16

D_MODEL = 1024
GRID_W = 64
N_HEADS = 6
HEAD_DIM = 64
V_DIM = 2 * HEAD_DIM
D_FOURIER = 256
FOURIER_HEAD_DIM = 64
D_QK = N_HEADS * V_DIM
D_ATTN = N_HEADS * V_DIM
D_IN = D_FOURIER + 2 * D_QK + D_ATTN
D_FF = 2816
ROPE_THETA = 10000.0
EPS = 1e-6
LAM_INIT = 0.8 - 0.6 * math.exp(-0.3 * 0)
Q_SCALE = HEAD_DIM ** -0.5 * math.log2(math.e)
N_MOD = 6
MOD_ROWS = 16
MOD_CTX_ROW = 0
MOD_REQ_ROW0 = 8
ADA_K_ROWS = 256

SUBLANES = 8
MXU_COLS = 256
assert D_FF % MXU_COLS == 0
FFN_ROWS = 1024
ATTN_TQ = 512
VMEM_LIMIT = 58 * 1024 * 1024


def _params(n_axes):
    return pltpu.CompilerParams(
        dimension_semantics=("arbitrary",) * n_axes,
        vmem_limit_bytes=VMEM_LIMIT,
    )


def _ada_kernel(ctx_ref, c_ref, w_ref, b_ref, o_ref, acc_ref):
    j = pl.program_id(0)
    c = jnp.concatenate(
        [jnp.broadcast_to(ctx_ref[...], (SUBLANES, ctx_ref.shape[1])), c_ref[...]], axis=0)
    s = c * jax.nn.sigmoid(c)
    part = jnp.dot(s.astype(BF16), w_ref[...].astype(BF16), preferred_element_type=F32)

    @pl.when(j == 0)
    def _():
        acc_ref[...] = part + b_ref[...]

    @pl.when(j > 0)
    def _():
        acc_ref[...] += part

    @pl.when(j == pl.num_programs(0) - 1)
    def _():
        for k in range(N_MOD):
            o_ref[k] = acc_ref[:, k * D_MODEL:(k + 1) * D_MODEL]


def _ada(c_ctx, c, w_ada, b_ada):
    d, n = w_ada.shape
    tk = ADA_K_ROWS
    assert c.shape[0] == MOD_ROWS - SUBLANES
    return pl.pallas_call(
        _ada_kernel,
        out_shape=jax.ShapeDtypeStruct((N_MOD, MOD_ROWS, D_MODEL), F32),
        grid=(d // tk,),
        in_specs=[
            pl.BlockSpec((1, tk), lambda j: (0, j)),
            pl.BlockSpec((c.shape[0], tk), lambda j: (0, j)),
            pl.BlockSpec((tk, n), lambda j: (j, 0)),
            pl.BlockSpec((1, n), lambda j: (0, 0)),
        ],
        out_specs=pl.BlockSpec((N_MOD, MOD_ROWS, D_MODEL), lambda j: (0, 0, 0)),
        scratch_shapes=[pltpu.VMEM((MOD_ROWS, n), F32)],
        compiler_params=_params(1),
        name="ada_mod",
    )(c_ctx.reshape(1, d), c, w_ada, b_ada.reshape(1, n))


def _mod_rows(mod_ref, row):
    return [mod_ref[k, pl.ds(row, 1), :] for k in range(N_MOD)]


def _rms_rows(x, gain):
    ms = jnp.mean(x * x, axis=-1, keepdims=True)
    return x * lax.rsqrt(ms + EPS) * gain


def _group_rmsnorm(x, gain, ones_blk):
    ss = jnp.dot((x * x).astype(BF16), ones_blk, preferred_element_type=F32)
    return x * lax.rsqrt(ss * (1.0 / HEAD_DIM) + EPS) * gain


def _rope(x, cos, sin_dn, sin_up):
    return (x * cos + pltpu.roll(x, V_DIM - 16, 1) * sin_dn
            + pltpu.roll(x, 16, 1) * sin_up)


def _inproj_kernel(x_ref, mod_ref, n1_ref, w_ref, qkg_ref, ones_ref, cc_ref,
                   sc_ref, cs_ref, ss_ref, *rest, use_rope, attend, kv_dtype, mod_row):
    if use_rope:
        cos_ref, sdn_ref, sup_ref = rest[:3]
        rest = rest[3:]
    if attend:
        lam_ref, subg_ref, yf_ref, o_ref, k_ref, v_ref = rest
        lam = _lam_value(lam_ref)
        ones_v = jnp.ones((x_ref.shape[1], V_DIM), BF16)
    else:
        yf_ref, q_ref, k_ref, v_ref = rest
    nb, seq, d = x_ref.shape
    x = x_ref[...].reshape(nb * seq, d)
    mod = _mod_rows(mod_ref, mod_row(pl.program_id(0)))
    h = (_rms_rows(x, n1_ref[...]) * (1.0 + mod[1]) + mod[0]).astype(BF16)

    def project(c0):
        return jnp.dot(h, w_ref[:, c0:c0 + MXU_COLS].astype(BF16),
                       preferred_element_type=F32)

    def fourier_mix():
        xf = project(0).astype(BF16)
        xc = jnp.dot(xf, cc_ref[...], preferred_element_type=F32).astype(BF16)
        xs = jnp.dot(xf, sc_ref[...], preferred_element_type=F32).astype(BF16)
        for b in range(nb):
            r = slice(b * seq, (b + 1) * seq)
            yf = (jnp.dot(cs_ref[...], xc[r], preferred_element_type=F32)
                  - jnp.dot(ss_ref[...], xs[r], preferred_element_type=F32))
            yf_ref[b] = yf.astype(yf_ref.dtype)

    def store_heads(out_ref, head0, y, dtype):
        for hh in range(y.shape[1] // V_DIM):
            for b in range(nb):
                out_ref[b, head0 + hh] = y[b * seq:(b + 1) * seq,
                                           hh * V_DIM:(hh + 1) * V_DIM].astype(dtype)

    ones_blk = ones_ref[...]
    def project_pair(pair):
        c0 = D_FOURIER + pair * MXU_COLS
        return project(c0), project(c0 + D_QK), project(c0 + 2 * D_QK)

    n_pairs = N_HEADS // 2
    lookahead = 1 if attend else 0
    done = [project_pair(p) for p in range(lookahead)]
    for pair in range(n_pairs):
        if pair + lookahead < n_pairs:
            done.append(project_pair(pair + lookahead))
        zq, zk, zv = done[pair]
        q = _group_rmsnorm(zq, qkg_ref[0:1, :], ones_blk)
        k = _group_rmsnorm(zk, qkg_ref[1:2, :], ones_blk)
        for hh in range(2):
            ls = slice(hh * V_DIM, (hh + 1) * V_DIM)
            qh, kh = q[:, ls], k[:, ls]
            if use_rope:
                qh = _rope(qh, cos_ref[...], sdn_ref[...], sup_ref[...])
                kh = _rope(kh, cos_ref[...], sdn_ref[...], sup_ref[...])
            hd = 2 * pair + hh
            qs = (qh * Q_SCALE).astype(BF16)
            if not attend:
                store_heads(k_ref, hd, kh, kv_dtype)
                store_heads(q_ref, hd, qs, BF16)
                continue
            kb, vb = kh.astype(BF16), zv[:, ls].astype(BF16)
            units = []
            for b in range(nb):
                r = slice(b * seq, (b + 1) * seq)
                k_ref[b, hd] = kh[r].T.astype(kv_dtype)
                units.append((
                    functools.partial(_head_scores, qs[r], kb[r], None),
                    functools.partial(jnp.concatenate, [vb[r], ones_v], axis=1),
                    functools.partial(_store_head, o_ref, b, hd)))
            _attend_units(units, lam, subg_ref[...])
        store_heads(v_ref, 2 * pair, zv, kv_dtype)
    fourier_mix()


def _dft_tables(n, scale):
    idx = np.arange(n, dtype=np.int64)
    ang = ((idx[:, None] * idx[None, :]) % n).astype(np.float64) * (2.0 * math.pi / n)
    return ((np.cos(ang) * scale).astype(np.float32),
            (np.sin(ang) * scale).astype(np.float32))


def _fourier_tables(s):
    c64, s64 = _dft_tables(FOURIER_HEAD_DIM, (s * FOURIER_HEAD_DIM) ** -0.5)
    eye = np.eye(D_FOURIER // FOURIER_HEAD_DIM, dtype=np.float32)
    tabs = (np.kron(eye, c64), np.kron(eye, s64)) + _dft_tables(s, 1.0)
    return tuple(jnp.asarray(t).astype(BF16) for t in tabs)


def _inproj(x, mod, mod_row, n1, w_in, qkg, rope, nb, kv_dtype, attn=None):
    b, s, d = x.shape
    use_rope = rope is not None
    attend = attn is not None
    ones_blk = jnp.asarray(np.kron(np.eye(MXU_COLS // HEAD_DIM, dtype=np.float32),
                                   np.ones((HEAD_DIM, HEAD_DIM), np.float32))).astype(BF16)
    const = lambda shape: pl.BlockSpec(shape, lambda i: (0, 0))
    in_specs = [
        pl.BlockSpec((nb, s, d), lambda i: (i, 0, 0)),
        pl.BlockSpec((N_MOD, MOD_ROWS, d), lambda i: (0, 0, 0)),
        const((1, d)), const((d, D_IN)), const((2, MXU_COLS)),
        const((MXU_COLS, MXU_COLS)), const((D_FOURIER, D_FOURIER)),
        const((D_FOURIER, D_FOURIER)), const((s, s)), const((s, s)),
    ]
    args = [x, mod, n1, w_in, qkg, ones_blk, *_fourier_tables(s)]
    if use_rope:
        assert nb == 1
        in_specs += [const((s, V_DIM))] * 3
        args += list(rope)
    if attend:
        in_specs += [const((4, HEAD_DIM)), const((1, V_DIM))]
        args += list(attn)
    head_spec = pl.BlockSpec((nb, N_HEADS, s, V_DIM), lambda i: (i, 0, 0, 0))
    head_shape = (b, N_HEADS, s, V_DIM)
    if attend:
        second = (jax.ShapeDtypeStruct((b, s, D_ATTN), BF16),
                  pl.BlockSpec((nb, s, D_ATTN), lambda i: (i, 0, 0)))
        keys = (jax.ShapeDtypeStruct((b, N_HEADS, V_DIM, s), kv_dtype),
                pl.BlockSpec((nb, N_HEADS, V_DIM, s), lambda i: (i, 0, 0, 0)))
    else:
        second = (jax.ShapeDtypeStruct(head_shape, BF16), head_spec)
        keys = (jax.ShapeDtypeStruct(head_shape, kv_dtype), head_spec)
    return pl.pallas_call(
        functools.partial(_inproj_kernel, use_rope=use_rope, attend=attend,
                          kv_dtype=kv_dtype, mod_row=mod_row),
        out_shape=(
            jax.ShapeDtypeStruct((b, s, D_FOURIER), BF16),
            second[0], keys[0],
            jax.ShapeDtypeStruct(head_shape, kv_dtype),
        ),
        grid=(b // nb,),
        in_specs=in_specs,
        out_specs=(
            pl.BlockSpec((nb, s, D_FOURIER), lambda i: (i, 0, 0)),
            second[1], keys[1], head_spec,
        ),
        compiler_params=_params(1),
        name="inproj_rope" if use_rope else "inproj_attn" if attend else "inproj",
    )(*args)


def _scores(q, k):
    return lax.dot_general(q, k, (((1,), (1,)), ((), ())),
                           preferred_element_type=F32)


def _lam_value(lam_ref):
    lv = lam_ref[...]
    e1 = jnp.exp(jnp.sum(lv[0:1] * lv[1:2], axis=-1, keepdims=True))
    e2 = jnp.exp(jnp.sum(lv[2:3] * lv[3:4], axis=-1, keepdims=True))
    return e1 - e2 + LAM_INIT


def _attend_units(units, lam, subg):
    ahead = units[0][0]()
    for n, (_, values_fn, store_fn) in enumerate(units):
        scores = ahead
        if n + 1 < len(units):
            ahead = units[n + 1][0]()
        vv = values_fn()
        r0, r1 = [jnp.dot(jnp.exp2(s - s.max(axis=-1, keepdims=True)).astype(BF16), vv,
                          preferred_element_type=F32) for s in scores]
        o = r0[:, :V_DIM] / r0[:, V_DIM:] - lam * (r1[:, :V_DIM] / r1[:, V_DIM:])
        ms = jnp.mean(o * o, axis=-1, keepdims=True)
        store_fn(o * lax.rsqrt(ms + EPS) * subg * (1.0 - LAM_INIT))


def _head_scores(q, kk, ckt):
    lo_mask = lax.broadcasted_iota(jnp.int32, (1, V_DIM), 1) < HEAD_DIM
    zero = jnp.zeros_like(q)
    out = []
    for qc in (jnp.where(lo_mask, q, zero), jnp.where(lo_mask, zero, q)):
        s = _scores(qc, kk)
        if ckt is not None:
            s = jnp.concatenate([s, jnp.dot(qc, ckt, preferred_element_type=F32)],
                                axis=1)
        out.append(s)
    return out


def _store_head(o_ref, b, hd, o):
    o_ref[b, :, hd * V_DIM:(hd + 1) * V_DIM] = o.astype(o_ref.dtype)


def _attn_kernel(lam_ref, subg_ref, q_ref, k_ref, v_ref, *rest, has_ctx):
    n_cast = (len(rest) - (3 if has_ctx else 1)) // 2
    if has_ctx:
        ckt_ref, cv_ref = rest[:2]
        rest = rest[2:]
    cast_in = rest[:n_cast]
    o_ref = rest[n_cast]
    cast_out = rest[n_cast + 1:2 * n_cast + 1]
    nb, heads, s_self, _ = k_ref.shape

    for src, dst in zip(cast_in, cast_out):
        dst[...] = src[...].astype(BF16)

    lam = _lam_value(lam_ref)
    subg = subg_ref[...]

    def head_values(b, hd):
        v = v_ref[b, hd].astype(BF16)
        if has_ctx:
            v = jnp.concatenate([v, cv_ref[b, hd].astype(BF16)], axis=0)
        return jnp.concatenate([v, jnp.ones_like(v)], axis=1)

    def head_scores(b, hd):
        ckt = ckt_ref[b, hd].astype(BF16) if has_ctx else None
        return _head_scores(q_ref[b, hd], k_ref[b, hd].astype(BF16), ckt)

    _attend_units(
        [(functools.partial(head_scores, b, hd), functools.partial(head_values, b, hd),
          functools.partial(_store_head, o_ref, b, hd))
         for b in range(nb) for hd in range(heads)], lam, subg)


def _attention(lamv, subg2, q, k, v, ctx, nb, tq, cast=()):
    b, nh, s, _ = q.shape
    has_ctx = ctx is not None
    nt = s // tq
    steps = (b // nb) * nt
    qspec = pl.BlockSpec((nb, nh, tq, V_DIM), lambda i, t: (i, 0, t, 0))
    kvspec = pl.BlockSpec((nb, nh, s, V_DIM), lambda i, t: (i, 0, 0, 0))
    in_specs = [
        pl.BlockSpec((4, HEAD_DIM), lambda i, t: (0, 0)),
        pl.BlockSpec((1, V_DIM), lambda i, t: (0, 0)),
        qspec, kvspec, kvspec,
    ]
    args = [lamv, subg2, q, k, v]
    if has_ctx:
        p = ctx[1].shape[2]
        in_specs += [pl.BlockSpec((nb, nh, V_DIM, p), lambda i, t: (i, 0, 0, 0)),
                     pl.BlockSpec((nb, nh, p, V_DIM), lambda i, t: (i, 0, 0, 0))]
        args += list(ctx)
    cast_specs = [pl.BlockSpec((w.shape[0] // steps, w.shape[1]),
                               lambda i, t: (i * nt + t, 0)) for w in cast]
    outs = pl.pallas_call(
        functools.partial(_attn_kernel, has_ctx=has_ctx),
        out_shape=(jax.ShapeDtypeStruct((b, s, nh * V_DIM), BF16),
                   *(jax.ShapeDtypeStruct(w.shape, BF16) for w in cast)),
        grid=(b // nb, nt),
        in_specs=in_specs + cast_specs,
        out_specs=(pl.BlockSpec((nb, tq, nh * V_DIM), lambda i, t: (i, t, 0)),
                   *cast_specs),
        compiler_params=_params(2),
        name="diff_attn_ctx" if has_ctx else "diff_attn",
    )(*args, *cast)
    return outs


def _shifted_rows(gs_ref, slot, g, seq):
    rows, w = g.shape
    pitch = seq + SUBLANES
    for s in range(rows // seq):
        base = SUBLANES + s * pitch
        gs_ref[slot, base:base + seq, :w] = g[s * seq:(s + 1) * seq]
    shifted = []
    for off in (-1, 1):
        parts = [gs_ref[slot, SUBLANES + s * pitch + off:
                        SUBLANES + s * pitch + off + seq, :w]
                 for s in range(rows // seq)]
        shifted.append(parts[0] if len(parts) == 1 else jnp.concatenate(parts, axis=0))
    return shifted


def _mixffn_kernel(x_ref, yf_ref, o_ref, mod_ref, n2_ref, wo_ref, wg_ref, wu_ref,
                   cw_ref, cb_ref, wd_ref, y_ref, h2_ref, act_ref, gs_ref, *, seq,
                   mod_row):
    mod = _mod_rows(mod_ref, mod_row(pl.program_id(0)))
    rows = x_ref.shape[0]

    mix = (jnp.dot(yf_ref[...], wo_ref[:D_FOURIER, :].astype(BF16),
                   preferred_element_type=F32)
           + jnp.dot(o_ref[...], wo_ref[D_FOURIER:, :].astype(BF16),
                     preferred_element_type=F32))
    x1 = x_ref[...] + mod[2] * mix
    y_ref[...] = x1
    h2 = _rms_rows(x1, n2_ref[...]) * (1.0 + mod[4]) + mod[3]
    h2_ref[...] = h2.astype(BF16)
    for slot in range(gs_ref.shape[0]):
        for s in range(rows // seq + 1):
            r0 = s * (seq + SUBLANES)
            gs_ref[slot, r0:r0 + SUBLANES, :] = jnp.zeros(
                (SUBLANES, gs_ref.shape[2]), F32)

    h2 = h2_ref[...]
    cw = cw_ref[...]
    cb = cb_ref[...]
    w = MXU_COLS
    for n, c0 in enumerate(range(0, D_FF, w)):
        g = jnp.dot(h2, wg_ref[:, c0:c0 + w], preferred_element_type=F32)
        u = jnp.dot(h2, wu_ref[:, c0:c0 + w], preferred_element_type=F32)
        g_prev, g_next = _shifted_rows(gs_ref, n % gs_ref.shape[0], g, seq)
        gc = (g * cw[1:2, c0:c0 + w] + g_prev * cw[0:1, c0:c0 + w]
              + g_next * cw[2:3, c0:c0 + w] + cb[:, c0:c0 + w])
        hg = 0.5 * gc
        act_ref[:, c0:c0 + w] = ((hg * u) * (1.0 + jnp.tanh(hg))).astype(BF16)
    y_ref[...] += mod[5] * jnp.dot(act_ref[...], wd_ref[...],
                                     preferred_element_type=F32)


def _mixffn(x2, yf2, o2, mod, mod_row, n2, w_out, w_gate, w_up, conv_w, conv_b2,
            w_down, seq):
    m, d = x2.shape
    tm = FFN_ROWS
    row = lambda i: (i, 0)
    const = lambda shape: pl.BlockSpec(shape, lambda i: (0, 0))
    stage_rows = SUBLANES + (tm // seq) * (seq + SUBLANES)
    return pl.pallas_call(
        functools.partial(_mixffn_kernel, seq=seq, mod_row=mod_row),
        out_shape=jax.ShapeDtypeStruct((m, d), F32),
        grid=(m // tm,),
        in_specs=[
            pl.BlockSpec((tm, d), row),
            pl.BlockSpec((tm, D_FOURIER), row),
            pl.BlockSpec((tm, D_ATTN), row),
            pl.BlockSpec((N_MOD, MOD_ROWS, d), lambda i: (0, 0, 0)),
            const((1, d)), const((d, d)), const((d, D_FF)), const((d, D_FF)),
            const((3, D_FF)), const((1, D_FF)), const((D_FF, d)),
        ],
        out_specs=pl.BlockSpec((tm, d), row),
        scratch_shapes=[
            pltpu.VMEM((tm, d), BF16),
            pltpu.VMEM((tm, D_FF), BF16),
            pltpu.VMEM((2, stage_rows, MXU_COLS), F32),
        ],
        compiler_params=_params(1),
        name="mix_ffn",
    )(x2, yf2, o2, mod, n2, w_out, w_gate, w_up, conv_w, conv_b2, w_down)


def _rope_tables(n_tokens):
    rows = n_tokens // GRID_W
    row = np.repeat(np.arange(rows, dtype=np.float64), GRID_W)
    col = np.tile(np.arange(GRID_W, dtype=np.float64), rows)
    n_freq = HEAD_DIM // 4
    inv = ROPE_THETA ** (-np.arange(n_freq, dtype=np.float64) / n_freq)
    ang_r = row[:, None] * inv
    ang_c = col[:, None] * inv
    zero = np.zeros_like(ang_r)
    cos64 = np.concatenate([np.cos(ang_r)] * 2 + [np.cos(ang_c)] * 2, axis=-1)
    sdn64 = np.concatenate([-np.sin(ang_r), zero, -np.sin(ang_c), zero], axis=-1)
    sup64 = np.concatenate([zero, np.sin(ang_r), zero, np.sin(ang_c)], axis=-1)
    two = lambda t: jnp.asarray(np.concatenate([t, t], axis=-1).astype(np.float32))
    return two(cos64), two(sdn64), two(sup64)


def kernel(x_prompt, x_sample, cache_k, cache_v, c, c_ctx, norm1_g, norm2_g, w_ada,
           b_ada, w_in, q_norm_g, k_norm_g, lam_q1, lam_k1, lam_q2, lam_k2, subln_g,
           w_out, w_gate, w_up, conv_w, conv_b, w_down):
    bp, sp, d = x_prompt.shape
    bs, ss, _ = x_sample.shape
    assert FFN_ROWS % sp == 0 and FFN_ROWS == ss

    mod = _ada(c_ctx, c, w_ada[0], b_ada[0])

    n1, n2, subg2 = norm1_g[0][None, :], norm2_g[0][None, :], subln_g[0][None, :]
    qkg = jnp.tile(jnp.concatenate([q_norm_g, k_norm_g], axis=0),
                   (1, MXU_COLS // HEAD_DIM))
    lamv = jnp.stack([lam_q1[0], lam_k1[0], lam_q2[0], lam_k2[0]])
    row_p = lambda i: MOD_CTX_ROW
    row_s = lambda i: MOD_REQ_ROW0 + i
    p = cache_k.shape[3]
    ck_t = jnp.transpose(cache_k[:, 0], (0, 1, 3, 4, 2)).reshape(bs, N_HEADS, V_DIM, p)
    ctx = (ck_t, cache_v[:, 0])

    yf_p, o_p, k_p, v_p = _inproj(x_prompt, mod, row_p, n1, w_in[0], qkg, None,
                                  FFN_ROWS // sp, F32, attn=(lamv, subg2))
    yf_s, q_s, k_s, v_s = _inproj(x_sample, mod, row_s, n1, w_in[0], qkg,
                                  _rope_tables(ss), FFN_ROWS // ss, BF16)
    o_s, wg_b, wu_b, wd_b = _attention(lamv, subg2, q_s, k_s, v_s, ctx, 1, ATTN_TQ,
                                       cast=(w_gate[0], w_up[0], w_down[0]))

    def mixer(x, yf, o, mod_row):
        b, s, _ = x.shape
        y = _mixffn(x.reshape(b * s, d), yf.reshape(b * s, D_FOURIER),
                    o.reshape(b * s, D_ATTN), mod, mod_row, n2, w_out[0], wg_b, wu_b,
                    conv_w[0], conv_b[0][None, :], wd_b, s)
        return y.reshape(b, s, d)

    y_p = mixer(x_prompt, yf_p, o_p, row_p)
    y_s = mixer(x_sample, yf_s, o_s, row_s)

    new_k = jnp.transpose(k_p.reshape(bp, 1, N_HEADS, 2, HEAD_DIM, sp),
                          (0, 1, 2, 5, 3, 4))
    new_v = v_p.reshape(bp, 1, N_HEADS, sp, V_DIM)
    return (y_p, y_s, new_k, new_v)
```
